```python
import jax, jax.numpy as jnp
from jax import lax
import numpy as np

D_MODEL = 2048
BATCH = 1
SEQ = 8192
DEPTH = 1

CHUNK = 128
A_GROUPS = 8
A_GROUP_DIM = 128
A_WIDTH = A_GROUPS * A_GROUP_DIM
SB_HEADS = 8
SB_HEAD_DIM = 128
SB_WIDTH = SB_HEADS * SB_HEAD_DIM
Q_BLOCK = 128
N_GROUPS = 4
EXPERTS_PER_GROUP = 8
N_EXPERTS = N_GROUPS * EXPERTS_PER_GROUP
TOP_K_IN_GROUP = 2
D_EXPERT = D_MODEL // 2
MOE_BLOCK = 128
EPS = 1e-6
PROJ_WIDTH = 2 * A_WIDTH + 3 * SB_WIDTH + 2 * D_MODEL
SPLITS = [A_WIDTH, 2 * A_WIDTH, 2 * A_WIDTH + SB_WIDTH, 2 * A_WIDTH + 2 * SB_WIDTH,
          2 * A_WIDTH + 3 * SB_WIDTH, 2 * A_WIDTH + 3 * SB_WIDTH + D_MODEL]

kernel_name = "hybrid_gmlp_stickbreak_hmoe"


def rms_norm(x, g):
    xf = x.astype(jnp.float32)
    y = xf * lax.rsqrt(jnp.mean(xf * xf, axis=-1, keepdims=True) + EPS)
    return (y * g.astype(jnp.float32)).astype(x.dtype)


def layer_norm(x, g, b):
    xf = x.astype(jnp.float32)
    mu = jnp.mean(xf, axis=-1, keepdims=True)
    xc = xf - mu
    y = xc * lax.rsqrt(jnp.mean(xc * xc, axis=-1, keepdims=True) + EPS)
    return (y * g.astype(jnp.float32) + b.astype(jnp.float32)).astype(x.dtype)


def chunked_spatial_gating(u, v, ln_g, ln_b, w_s, b_s):
    B, S, _ = v.shape
    v = layer_norm(v, ln_g, ln_b)
    vc = v.reshape(B, S // CHUNK, CHUNK, A_GROUPS, A_GROUP_DIM)
    causal = jnp.tril(jnp.ones((CHUNK, CHUNK), dtype=bool))
    ws = jnp.where(causal[None], w_s, 0)
    mixed = jnp.einsum('gts,bcsgd->bctgd', ws, vc) + b_s.T[:, :, None]
    return u * mixed.reshape(B, S, A_WIDTH)


def stick_breaking_attention(q, k, v):
    B, S, _ = q.shape
    nb = S // Q_BLOCK
    def heads(t):
        return t.reshape(B, S, SB_HEADS, SB_HEAD_DIM).transpose(0, 2, 1, 3)
    q, k, v = heads(q), heads(k), heads(v)
    v32 = v.astype(jnp.float32)
    q_blocks = q.reshape(B, SB_HEADS, nb, Q_BLOCK, SB_HEAD_DIM).transpose(2, 0, 1, 3, 4)
    kpos = jnp.arange(S)
    scale = SB_HEAD_DIM ** -0.5

    def block(args):
        qi, i = args
        z = jnp.einsum('bhqd,bhkd->bhqk', qi, k).astype(jnp.float32) * scale
        qpos = i * Q_BLOCK + jnp.arange(Q_BLOCK)
        mask = kpos[None, :] < qpos[:, None]
        log_rem = jnp.where(mask, jax.nn.log_sigmoid(-z), 0.0)
        suffix = lax.cumsum(log_rem, axis=3, reverse=True) - log_rem
        weights = jnp.where(mask, jnp.exp(jax.nn.log_sigmoid(z) + suffix), 0.0)
        return jnp.einsum('bhqk,bhkd->bhqd', weights, v32).astype(v.dtype)

    o = lax.map(block, (q_blocks, jnp.arange(nb)))
    o = o.transpose(1, 0, 3, 2, 4).reshape(B, S, SB_WIDTH)
    return o


def hierarchical_moe(h, w_rg, b_rg, w_re, b_re, w_gate, w_up, w_down):
    B, S, D = h.shape
    T = B * S
    xt = h.reshape(T, D)
    lg = jnp.einsum('td,dg->tg', xt, w_rg).astype(jnp.float32) + b_rg.astype(jnp.float32)
    pg = jax.nn.softmax(lg, axis=-1)
    grp = jnp.argmax(lg, axis=-1)
    pg_sel = jnp.take_along_axis(pg, grp[:, None], axis=1)[:, 0]
    le = (jnp.einsum('td,de->te', xt, w_re).astype(jnp.float32)
          + b_re.astype(jnp.float32)).reshape(T, N_GROUPS, EXPERTS_PER_GROUP)
    le_sel = jnp.take_along_axis(le, grp[:, None, None], axis=1)[:, 0]
    top_v, top_i = lax.top_k(le_sel, TOP_K_IN_GROUP)
    pe = jax.nn.softmax(top_v, axis=-1)
    gate_w = pg_sel[:, None] * pe
    expert = grp[:, None] * EXPERTS_PER_GROUP + top_i

    N = T * TOP_K_IN_GROUP
    e_flat = expert.reshape(N).astype(jnp.int32)
    w_flat = gate_w.reshape(N)
    tok_flat = jnp.repeat(jnp.arange(T, dtype=jnp.int32), TOP_K_IN_GROUP)
    counts = jnp.bincount(e_flat, length=N_EXPERTS)
    starts = jnp.cumsum(counts) - counts
    padded = ((counts + MOE_BLOCK - 1) // MOE_BLOCK) * MOE_BLOCK
    pends = jnp.cumsum(padded)
    pstarts = pends - padded
    order = jnp.argsort(e_flat, stable=True)
    se = e_flat[order]
    dest = pstarts[se] + (jnp.arange(N) - starts[se])
    n_rows = ((N + MOE_BLOCK - 1) // MOE_BLOCK + N_EXPERTS) * MOE_BLOCK
    row_tok = jnp.zeros((n_rows,), jnp.int32).at[dest].set(tok_flat[order])
    row_w = jnp.zeros((n_rows,), jnp.float32).at[dest].set(w_flat[order])
    nblk = n_rows // MOE_BLOCK
    block_start = jnp.arange(nblk) * MOE_BLOCK
    block_expert = jnp.minimum(jnp.sum(block_start[:, None] >= pends[None, :], axis=1),
                               N_EXPERTS - 1)

    def expert_block(args):
        tok, wrow, e = args
        xb = xt[tok]
        hdn = jax.nn.silu(xb @ w_gate[e]) * (xb @ w_up[e])
        return (hdn @ w_down[e]) * wrow[:, None].astype(xb.dtype)

    y = lax.map(expert_block, (row_tok.reshape(nblk, MOE_BLOCK),
                               row_w.reshape(nblk, MOE_BLOCK), block_expert))
    out = jnp.zeros((T, D), xt.dtype).at[row_tok].add(y.reshape(n_rows, D))
    return out.reshape(B, S, D)


def setup_inputs(seed: int = 0) -> dict:
    key = jax.random.key(seed)
    ks = jax.random.split(key, 20)
    def nrm(k, shape, scale):
        return jax.random.normal(k, shape, jnp.float32) * scale
    return {
        "x": nrm(ks[0], (BATCH, SEQ, D_MODEL), 1.0),
        "g_mix": 1.0 + nrm(ks[1], (DEPTH, D_MODEL), 0.01),
        "w_in": nrm(ks[2], (DEPTH, D_MODEL, PROJ_WIDTH), D_MODEL ** -0.5),
        "ln_v_g": 1.0 + nrm(ks[3], (DEPTH, A_WIDTH), 0.01),
        "ln_v_b": nrm(ks[4], (DEPTH, A_WIDTH), 0.01),
        "w_spatial": jnp.tril(nrm(ks[5], (DEPTH, A_GROUPS, CHUNK, CHUNK), 0.5 * CHUNK ** -0.5)),
        "b_spatial": 1.0 + nrm(ks[6], (DEPTH, A_GROUPS, CHUNK), 0.01),
        "w_branch_a": nrm(ks[7], (DEPTH, A_WIDTH, D_MODEL), A_WIDTH ** -0.5),
        "w_branch_b": nrm(ks[8], (DEPTH, SB_WIDTH, D_MODEL), SB_WIDTH ** -0.5),
        "w_out": nrm(ks[9], (DEPTH, D_MODEL, D_MODEL), D_MODEL ** -0.5),
        "g_ffn": 1.0 + nrm(ks[10], (DEPTH, D_MODEL), 0.01),
        "w_router_group": nrm(ks[11], (DEPTH, D_MODEL, N_GROUPS), D_MODEL ** -0.5),
        "b_router_group": nrm(ks[12], (DEPTH, N_GROUPS), 0.01),
        "w_router_expert": nrm(ks[13], (DEPTH, D_MODEL, N_EXPERTS), D_MODEL ** -0.5),
        "b_router_expert": nrm(ks[14], (DEPTH, N_EXPERTS), 0.01),
        "w_gate": nrm(ks[15], (DEPTH, N_EXPERTS, D_MODEL, D_EXPERT), D_MODEL ** -0.5),
        "w_up": nrm(ks[16], (DEPTH, N_EXPERTS, D_MODEL, D_EXPERT), D_MODEL ** -0.5),
        "w_down": nrm(ks[17], (DEPTH, N_EXPERTS, D_EXPERT, D_MODEL), D_EXPERT ** -0.5),
        "g_final": 1.0 + nrm(ks[18], (D_MODEL,), 0.01),
    }


def reference(x, g_mix, w_in, ln_v_g, ln_v_b, w_spatial, b_spatial, w_branch_a, w_branch_b,
              w_out, g_ffn, w_router_group, b_router_group, w_router_expert, b_router_expert,
              w_gate, w_up, w_down, g_final):
    for i in range(DEPTH):
        h = rms_norm(x, g_mix[i])
        proj = jnp.einsum('bsd,dp->bsp', h, w_in[i])
        u_a, v_a, q, k, v, gate_a, gate_b = jnp.split(proj, SPLITS, axis=-1)
        y_a = chunked_spatial_gating(jax.nn.gelu(u_a), jax.nn.gelu(v_a),
                                     ln_v_g[i], ln_v_b[i], w_spatial[i], b_spatial[i])
        y_b = stick_breaking_attention(q, k, v)
        merged = (jax.nn.sigmoid(gate_a) * jnp.einsum('bsc,cd->bsd', y_a, w_branch_a[i])
                  + jax.nn.sigmoid(gate_b) * jnp.einsum('bsc,cd->bsd', y_b, w_branch_b[i]))
        x = x + jnp.einsum('bsd,de->bse', merged, w_out[i])
        h2 = rms_norm(x, g_ffn[i])
        x = x + hierarchical_moe(h2, w_router_group[i], b_router_group[i],
                                 w_router_expert[i], b_router_expert[i],
                                 w_gate[i], w_up[i], w_down[i])
    return rms_norm(x, g_final)
```

```python
import functools

import jax
import jax.numpy as jnp
from jax import lax
from jax.experimental import pallas as pl
from jax.experimental.pallas import tpu as pltpu

F32 = jnp.float32
BF16 = jnp.bfloat16
I32 = jnp.int32

D_MODEL = 2048
CHUNK = 128
A_GROUPS = 8
A_WIDTH = 1024
SB_HEADS = 8
SB_HEAD_DIM = 128
SB_WIDTH = 1024
N_GROUPS = 4
EXPERTS_PER_GROUP = 8
N_EXPERTS = 32
D_EXPERT = 1024
EPS = 1e-6
PROJ_WIDTH = 2 * A_WIDTH + 3 * SB_WIDTH + 2 * D_MODEL
LANES = 128

OFF_U, OFF_V = 0, A_WIDTH
OFF_Q = 2 * A_WIDTH
OFF_K = OFF_Q + SB_WIDTH
OFF_VV = OFF_K + SB_WIDTH
OFF_GA = OFF_VV + SB_WIDTH
OFF_GB = OFF_GA + D_MODEL

VMEM_LIMIT = 56 * 1024 * 1024

PROJ_TM, PROJ_TN = 1024, 512
ATT_TQ, ATT_TK = 128, 128
MERGE_TM = 256
ROUTE_TB = 512
DISP_TM = 256
MOE_SB = 256
MOE_R = 1024
MOE_TH = 256
COMB_TM = 256


def _dot(a, b):
    return jnp.dot(a, b, preferred_element_type=F32)


def _proj_body(x_ref, g_ref, w_ref, o_ref, h_scr, *, n_gelu, n_plain):
    j = pl.program_id(1)

    @pl.when(j == 0)
    def _():
        x = x_ref[...]
        ms = jnp.mean(x * x, axis=-1, keepdims=True)
        h_scr[...] = (x * lax.rsqrt(ms + EPS) * g_ref[...]).astype(BF16)

    acc = _dot(h_scr[...], w_ref[...])

    @pl.when(j < n_gelu)
    def _():
        o_ref[...] = jax.nn.gelu(acc).astype(BF16)

    @pl.when((j >= n_gelu) & (j < n_gelu + n_plain))
    def _():
        o_ref[...] = acc.astype(BF16)

    @pl.when(j >= n_gelu + n_plain)
    def _():
        o_ref[...] = jax.nn.sigmoid(acc).astype(BF16)


def _proj(x2, g_mix, w_in_bf):
    T = x2.shape[0]
    tm = min(PROJ_TM, T)
    tn = PROJ_TN
    body = functools.partial(_proj_body, n_gelu=OFF_Q // tn, n_plain=(OFF_GA - OFF_Q) // tn)
    return pl.pallas_call(
        body,
        grid=(T // tm, PROJ_WIDTH // tn),
        in_specs=[
            pl.BlockSpec((tm, D_MODEL), lambda i, j: (i, 0)),
            pl.BlockSpec((1, D_MODEL), lambda i, j: (0, 0)),
            pl.BlockSpec((D_MODEL, tn), lambda i, j: (0, j)),
        ],
        out_specs=pl.BlockSpec((tm, tn), lambda i, j: (i, j)),
        out_shape=jax.ShapeDtypeStruct((T, PROJ_WIDTH), BF16),
        scratch_shapes=[pltpu.VMEM((tm, D_MODEL), BF16)],
        compiler_params=pltpu.CompilerParams(
            dimension_semantics=("arbitrary", "arbitrary"), vmem_limit_bytes=VMEM_LIMIT),
    )(x2, g_mix.reshape(1, D_MODEL), w_in_bf)


def _attn_tile(q, k, v, upper, carry, mask):
    z = lax.dot_general(q, k, (((1,), (1,)), ((), ())), preferred_element_type=F32)
    z = z * (SB_HEAD_DIM ** -0.5)
    sp = jnp.maximum(z, 0.0) + jnp.log(1.0 + jnp.exp(-jnp.abs(z)))
    log_rem = -sp
    if mask is not None:
        log_rem = jnp.where(mask, log_rem, 0.0)
    hi = log_rem.astype(BF16)
    lo = (log_rem - hi.astype(F32)).astype(BF16)
    suffix = _dot(hi, upper) + _dot(lo, upper)
    w = jnp.exp((z - sp) + suffix + carry)
    if mask is not None:
        w = jnp.where(mask, w, 0.0)
    contrib = _dot(w.astype(BF16), v)
    carry = carry + jnp.sum(log_rem, axis=1, keepdims=True)
    return contrib, carry


def _attn_body(q_ref, k_ref, v_ref, o_ref):
    i = pl.program_id(1)
    tq, tk = ATT_TQ, ATT_TK
    q = q_ref[...]
    row = lax.broadcasted_iota(I32, (tq, tk), 0)
    col = lax.broadcasted_iota(I32, (tq, tk), 1)
    upper = (row > col).astype(BF16)
    causal = col < row

    kd = k_ref[pl.ds(pl.multiple_of(i * tk, tk), tk), :]
    vd = v_ref[pl.ds(pl.multiple_of(i * tk, tk), tk), :]
    acc, carry = _attn_tile(q, kd, vd, upper, jnp.zeros((tq, 1), F32), causal)

    def step(jj, state):
        acc, carry = state
        j = i - 1 - jj
        off = pl.multiple_of(j * tk, tk)
        contrib, carry = _attn_tile(q, k_ref[pl.ds(off, tk), :], v_ref[pl.ds(off, tk), :],
                                    upper, carry, None)
        return acc + contrib, carry

    acc, _ = lax.fori_loop(0, i, step, (acc, carry))
    o_ref[...] = acc.astype(BF16)


def _attn(proj):
    T = proj.shape[0]
    qb, kb, vb = OFF_Q // SB_HEAD_DIM, OFF_K // SB_HEAD_DIM, OFF_VV // SB_HEAD_DIM
    return pl.pallas_call(
        _attn_body,
        grid=(SB_HEADS, T // ATT_TQ),
        in_specs=[
            pl.BlockSpec((ATT_TQ, SB_HEAD_DIM), lambda h, i: (i, qb + h)),
            pl.BlockSpec((T, SB_HEAD_DIM), lambda h, i: (0, kb + h)),
            pl.BlockSpec((T, SB_HEAD_DIM), lambda h, i: (0, vb + h)),
        ],
        out_specs=pl.BlockSpec((ATT_TQ, SB_HEAD_DIM), lambda h, i: (i, h)),
        out_shape=jax.ShapeDtypeStruct((T, SB_WIDTH), BF16),
        compiler_params=pltpu.CompilerParams(
            dimension_semantics=("arbitrary", "arbitrary"), vmem_limit_bytes=VMEM_LIMIT),
    )(proj, proj, proj)


def _merge_body(u_ref, v_ref, ga0_ref, ga1_ref, gb0_ref, gb1_ref, yb_ref, x_ref, lng_ref, lnb_ref,
                ws_ref, bst_ref, wa_ref, wb_ref, wo_ref, gffn_ref, wr_ref, br_ref,
                x1_ref, h2_ref, lg_ref, ya_scr):
    tm = u_ref.shape[0]
    vf = v_ref[...].astype(F32)
    mu = jnp.mean(vf, axis=-1, keepdims=True)
    xc = vf - mu
    var = jnp.mean(xc * xc, axis=-1, keepdims=True)
    vln = (xc * lax.rsqrt(var + EPS) * lng_ref[...] + lnb_ref[...]).astype(BF16)

    row = lax.broadcasted_iota(I32, (CHUNK, CHUNK), 0)
    col = lax.broadcasted_iota(I32, (CHUNK, CHUNK), 1)
    tril = col <= row
    for g in range(A_GROUPS):
        wg = jnp.where(tril, ws_ref[g], 0.0).astype(BF16)
        bias = bst_ref[:, g:g + 1]
        cs = slice(g * LANES, (g + 1) * LANES)
        for c in range(tm // CHUNK):
            rs = slice(c * CHUNK, (c + 1) * CHUNK)
            mixed = _dot(wg, vln[rs, cs]) + bias
            ya_scr[rs, cs] = (u_ref[rs, cs].astype(F32) * mixed).astype(BF16)

    a = _dot(ya_scr[...], wa_ref[...])
    b = _dot(yb_ref[...], wb_ref[...])
    ga = jnp.concatenate([ga0_ref[...], ga1_ref[...]], axis=1).astype(F32)
    gb = jnp.concatenate([gb0_ref[...], gb1_ref[...]], axis=1).astype(F32)
    merged = (ga * a + gb * b).astype(BF16)
    x1 = x_ref[...] + _dot(merged, wo_ref[...])
    x1_ref[...] = x1
    ms = jnp.mean(x1 * x1, axis=-1, keepdims=True)
    h2 = x1 * lax.rsqrt(ms + EPS) * gffn_ref[...]
    h2_ref[...] = h2
    hi = h2.astype(BF16)
    lo = (h2 - hi.astype(F32)).astype(BF16)
    wr = wr_ref[...]
    whi = wr.astype(BF16)
    wlo = (wr - whi.astype(F32)).astype(BF16)
    lg_ref[...] = _dot(hi, whi) + _dot(hi, wlo) + _dot(lo, whi) + br_ref[...]


def _merge(proj, yb, x2, ln_g, ln_b, w_s, b_s_t, wa, wb, wo, g_ffn, w_r, b_r):
    T = x2.shape[0]
    tm = MERGE_TM
    const2 = lambda i: (0, 0)
    one = pl.Buffered(1)

    def colblk(width, off):
        return pl.BlockSpec((tm, width), lambda i: (i, off // width))

    return pl.pallas_call(
        _merge_body,
        grid=(T // tm,),
        in_specs=[
            colblk(A_WIDTH, OFF_U), colblk(A_WIDTH, OFF_V),
            colblk(A_WIDTH, OFF_GA), colblk(A_WIDTH, OFF_GA + A_WIDTH),
            colblk(A_WIDTH, OFF_GB), colblk(A_WIDTH, OFF_GB + A_WIDTH),
            pl.BlockSpec((tm, SB_WIDTH), lambda i: (i, 0)),
            pl.BlockSpec((tm, D_MODEL), lambda i: (i, 0)),
            pl.BlockSpec((1, A_WIDTH), const2), pl.BlockSpec((1, A_WIDTH), const2),
            pl.BlockSpec((A_GROUPS, CHUNK, CHUNK), lambda i: (0, 0, 0)),
            pl.BlockSpec((CHUNK, A_GROUPS), const2),
            pl.BlockSpec((A_WIDTH, D_MODEL), const2, pipeline_mode=one),
            pl.BlockSpec((SB_WIDTH, D_MODEL), const2, pipeline_mode=one),
            pl.BlockSpec((D_MODEL, D_MODEL), const2, pipeline_mode=one),
            pl.BlockSpec((1, D_MODEL), const2),
            pl.BlockSpec((D_MODEL, LANES), const2, pipeline_mode=one),
            pl.BlockSpec((1, LANES), const2),
        ],
        out_specs=[
            pl.BlockSpec((tm, D_MODEL), lambda i: (i, 0)),
            pl.BlockSpec((tm, D_MODEL), lambda i: (i, 0)),
            pl.BlockSpec((tm, LANES), lambda i: (i, 0)),
        ],
        out_shape=[
            jax.ShapeDtypeStruct((T, D_MODEL), F32),
            jax.ShapeDtypeStruct((T, D_MODEL), F32),
            jax.ShapeDtypeStruct((T, LANES), F32),
        ],
        scratch_shapes=[pltpu.VMEM((tm, A_WIDTH), BF16)],
        compiler_params=pltpu.CompilerParams(
            dimension_semantics=("arbitrary",), vmem_limit_bytes=VMEM_LIMIT),
    )(proj, proj, proj, proj, proj, proj, yb, x2, ln_g, ln_b, w_s, b_s_t, wa, wb, wo, g_ffn, w_r, b_r)


def _route_body(lg_ref, oi_ref, ow_ref, cnt_ref, carry_scr):
    i = pl.program_id(0)
    tb = lg_ref.shape[0]

    @pl.when(i == 0)
    def _():
        carry_scr[...] = jnp.zeros_like(carry_scr)

    lg = lg_ref[...]
    lane = lax.broadcasted_iota(I32, (tb, LANES), 1)
    neg = jnp.float32(-jnp.inf)
    big = jnp.int32(LANES)

    def first_max(mask):
        m = jnp.max(jnp.where(mask, lg, neg), axis=1, keepdims=True)
        idx = jnp.min(jnp.where(mask & (lg == m), lane, big), axis=1, keepdims=True)
        return m, idx

    gmask = lane < N_GROUPS
    gm, grp = first_max(gmask)
    pg_sel = 1.0 / jnp.sum(jnp.where(gmask, jnp.exp(lg - gm), 0.0), axis=1, keepdims=True)

    eid = lane - N_GROUPS
    emask = (eid >= 0) & (eid < N_EXPERTS) & ((eid // EXPERTS_PER_GROUP) == grp)
    m1, i1 = first_max(emask)
    emask2 = emask & (lane != i1)
    m2, i2 = first_max(emask2)
    t = jnp.exp(m2 - m1)
    p1 = 1.0 / (1.0 + t)
    p2 = t / (1.0 + t)
    e1 = i1 - N_GROUPS
    e2 = i2 - N_GROUPS

    sel1 = lane == e1
    sel2 = lane == e2
    onehot = (sel1 | sel2).astype(BF16)
    r = lax.broadcasted_iota(I32, (tb, tb), 0)
    c = lax.broadcasted_iota(I32, (tb, tb), 1)
    earlier = (c < r).astype(BF16)
    prefix = _dot(earlier, onehot) + carry_scr[...]
    rank1 = jnp.sum(jnp.where(sel1, prefix, 0.0), axis=1, keepdims=True).astype(I32)
    rank2 = jnp.sum(jnp.where(sel2, prefix, 0.0), axis=1, keepdims=True).astype(I32)
    carry_scr[...] = carry_scr[...] + jnp.sum(onehot.astype(F32), axis=0, keepdims=True)

    zero = jnp.zeros((tb, LANES), I32)
    oi_ref[...] = jnp.where(lane == 0, e1, jnp.where(lane == 1, e2,
                            jnp.where(lane == 2, rank1, jnp.where(lane == 3, rank2, zero))))
    ow_ref[...] = jnp.where(lane == 0, pg_sel * p1, jnp.where(lane == 1, pg_sel * p2, 0.0))
    cnt_ref[...] = carry_scr[...]


def _route(logits):
    T = logits.shape[0]
    tb = min(ROUTE_TB, T)
    return pl.pallas_call(
        _route_body,
        grid=(T // tb,),
        in_specs=[pl.BlockSpec((tb, LANES), lambda i: (i, 0))],
        out_specs=[
            pl.BlockSpec((tb, LANES), lambda i: (i, 0)),
            pl.BlockSpec((tb, LANES), lambda i: (i, 0)),
            pl.BlockSpec((1, LANES), lambda i: (0, 0)),
        ],
        out_shape=[
            jax.ShapeDtypeStruct((T, LANES), I32),
            jax.ShapeDtypeStruct((T, LANES), F32),
            jax.ShapeDtypeStruct((1, LANES), F32),
        ],
        scratch_shapes=[pltpu.VMEM((1, LANES), F32)],
        compiler_params=pltpu.CompilerParams(dimension_semantics=("arbitrary",)),
    )(logits)


def _dispatch_body(pos_ref, pend_ref, h2_ref, xs_ref, zbuf, sem, zsem):
    i = pl.program_id(0)
    tm = h2_ref.shape[0]

    def tail_copy(e):
        start = pl.multiple_of(pend_ref[e] - MOE_SB, MOE_SB)
        return pltpu.make_async_copy(zbuf, xs_ref.at[pl.ds(start, MOE_SB), :], zsem)

    @pl.when(i == 0)
    def _():
        zbuf[...] = jnp.zeros_like(zbuf)
        for e in range(N_EXPERTS):
            prev = pend_ref[e - 1] if e > 0 else 0

            @pl.when(pend_ref[e] > prev)
            def _():
                tail_copy(e).start()
        for e in range(N_EXPERTS):
            prev = pend_ref[e - 1] if e > 0 else 0

            @pl.when(pend_ref[e] > prev)
            def _():
                tail_copy(e).wait()

        used = pend_ref[N_EXPERTS - 1]
        n_slack = (xs_ref.shape[0] - used) // MOE_SB

        def slack_copy(b):
            start = pl.multiple_of(used + b * MOE_SB, MOE_SB)
            return pltpu.make_async_copy(zbuf, xs_ref.at[pl.ds(start, MOE_SB), :], zsem)

        def slack_start(b, _):
            slack_copy(b).start()
            return 0

        def slack_wait(b, _):
            slack_copy(b).wait()
            return 0

        lax.fori_loop(0, n_slack, slack_start, 0)
        lax.fori_loop(0, n_slack, slack_wait, 0)

    def row_copy(r, k):
        p = pos_ref[2 * (i * tm + r) + k]
        return pltpu.make_async_copy(h2_ref.at[pl.ds(r, 1), :], xs_ref.at[pl.ds(p, 1), :], sem)

    def issue(r, _):
        row_copy(r, 0).start()
        row_copy(r, 1).start()
        return 0

    def drain(r, _):
        row_copy(r, 0).wait()
        row_copy(r, 1).wait()
        return 0

    lax.fori_loop(0, tm, issue, 0)
    lax.fori_loop(0, tm, drain, 0)


def _dispatch(pos_flat, pends, h2, n_rows):
    T = h2.shape[0]
    tm = DISP_TM
    return pl.pallas_call(
        _dispatch_body,
        grid_spec=pltpu.PrefetchScalarGridSpec(
            num_scalar_prefetch=2,
            grid=(T // tm,),
            in_specs=[pl.BlockSpec((tm, D_MODEL), lambda i, pos, pe: (i, 0))],
            out_specs=pl.BlockSpec(memory_space=pl.ANY),
            scratch_shapes=[
                pltpu.VMEM((MOE_SB, D_MODEL), F32),
                pltpu.SemaphoreType.DMA(()),
                pltpu.SemaphoreType.DMA(()),
            ],
        ),
        out_shape=jax.ShapeDtypeStruct((n_rows, D_MODEL), F32),
        compiler_params=pltpu.CompilerParams(
            dimension_semantics=("arbitrary",), has_side_effects=True),
    )(pos_flat, pends, h2)


def _moe_body(ie_ref, is_ref, in_ref, used_ref, xs_ref, wg_ref, wu_ref, wd_ref, ys_ref,
              xbuf, acc, sem_in, sem_out):
    s = pl.program_id(0)
    h = pl.program_id(1)
    n_h = pl.num_programs(1)
    n = in_ref[s]
    start = is_ref[s]
    nsub = (n + MOE_SB - 1) // MOE_SB

    def in_copy(b):
        off = pl.multiple_of(b * MOE_SB, MOE_SB)
        src = pl.multiple_of(start + off, MOE_SB)
        return pltpu.make_async_copy(xs_ref.at[pl.ds(src, MOE_SB), :],
                                     acc.at[pl.ds(off, MOE_SB), :], sem_in)

    def out_copy(b):
        off = pl.multiple_of(b * MOE_SB, MOE_SB)
        dst = pl.multiple_of(start + off, MOE_SB)
        return pltpu.make_async_copy(acc.at[pl.ds(off, MOE_SB), :],
                                     ys_ref.at[pl.ds(dst, MOE_SB), :], sem_out)

    def loop(fn):
        def body(b, _):
            fn(b)
            return 0
        lax.fori_loop(0, nsub, body, 0)

    @pl.when((s == 0) & (h == 0))
    def _():
        used = used_ref[0]
        n_slack = (ys_ref.shape[0] - used) // MOE_SB
        acc[pl.ds(0, MOE_SB), :] = jnp.zeros((MOE_SB, D_MODEL), F32)

        def slack_copy(b):
            dst = pl.multiple_of(used + b * MOE_SB, MOE_SB)
            return pltpu.make_async_copy(acc.at[pl.ds(0, MOE_SB), :],
                                         ys_ref.at[pl.ds(dst, MOE_SB), :], sem_out)

        def slack_start(b, _):
            slack_copy(b).start()
            return 0

        def slack_wait(b, _):
            slack_copy(b).wait()
            return 0

        lax.fori_loop(0, n_slack, slack_start, 0)
        lax.fori_loop(0, n_slack, slack_wait, 0)

    @pl.when(h == 0)
    def _():
        loop(lambda b: in_copy(b).start())
        loop(lambda b: in_copy(b).wait())

        def cast(b):
            off = pl.multiple_of(b * MOE_SB, MOE_SB)
            xbuf[pl.ds(off, MOE_SB), :] = acc[pl.ds(off, MOE_SB), :].astype(BF16)
        loop(cast)

    def compute(b, wg, wu, wd):
        off = pl.multiple_of(b * MOE_SB, MOE_SB)
        xb = xbuf[pl.ds(off, MOE_SB), :]
        gate = _dot(xb, wg)
        up = _dot(xb, wu)
        hidden = (gate * jax.nn.sigmoid(gate) * up).astype(BF16)
        contrib = _dot(hidden, wd)

        @pl.when(h == 0)
        def _():
            acc[pl.ds(off, MOE_SB), :] = contrib

        @pl.when(h > 0)
        def _():
            acc[pl.ds(off, MOE_SB), :] = acc[pl.ds(off, MOE_SB), :] + contrib

    @pl.when(n > 0)
    def _():
        wg = wg_ref[...].astype(BF16)
        wu = wu_ref[...].astype(BF16)
        wd = wd_ref[...].astype(BF16)
        loop(lambda b: compute(b, wg, wu, wd))

    @pl.when(h == n_h - 1)
    def _():
        loop(lambda b: out_copy(b).start())
        loop(lambda b: out_copy(b).wait())


def _moe(item_e, item_start, item_n, used, xs, w_gate, w_up, w_down):
    n_rows = xs.shape[0]
    n_items = item_e.shape[0]
    th = MOE_TH
    return pl.pallas_call(
        _moe_body,
        grid_spec=pltpu.PrefetchScalarGridSpec(
            num_scalar_prefetch=4,
            grid=(n_items, D_EXPERT // th),
            in_specs=[
                pl.BlockSpec(memory_space=pl.ANY),
                pl.BlockSpec((None, D_MODEL, th), lambda s, h, ie, st, nn, us: (ie[s], 0, h)),
                pl.BlockSpec((None, D_MODEL, th), lambda s, h, ie, st, nn, us: (ie[s], 0, h)),
                pl.BlockSpec((None, th, D_MODEL), lambda s, h, ie, st, nn, us: (ie[s], h, 0)),
            ],
            out_specs=pl.BlockSpec(memory_space=pl.ANY),
            scratch_shapes=[
                pltpu.VMEM((MOE_R, D_MODEL), BF16),
                pltpu.VMEM((MOE_R, D_MODEL), F32),
                pltpu.SemaphoreType.DMA(()),
                pltpu.SemaphoreType.DMA(()),
            ],
        ),
        out_shape=jax.ShapeDtypeStruct((n_rows, D_MODEL), F32),
        compiler_params=pltpu.CompilerParams(
            dimension_semantics=("arbitrary", "arbitrary"), vmem_limit_bytes=VMEM_LIMIT,
            has_side_effects=True),
    )(item_e, item_start, item_n, used, xs, w_gate, w_up, w_down)


def _combine_body(pos_ref, x1_ref, gw_ref, gf_ref, ys_ref, o_ref, ybuf, sem):
    i = pl.program_id(0)
    tm = x1_ref.shape[0]

    def row_copy(r, k):
        p = pos_ref[2 * (i * tm + r) + k]
        return pltpu.make_async_copy(ys_ref.at[pl.ds(p, 1), :], ybuf.at[k, pl.ds(r, 1), :], sem)

    def issue(r, _):
        row_copy(r, 0).start()
        row_copy(r, 1).start()
        return 0

    def drain(r, _):
        row_copy(r, 0).wait()
        row_copy(r, 1).wait()
        return 0

    lax.fori_loop(0, tm, issue, 0)
    lax.fori_loop(0, tm, drain, 0)

    gw = gw_ref[...]
    x = x1_ref[...] + gw[:, 0:1] * ybuf[0] + gw[:, 1:2] * ybuf[1]
    ms = jnp.mean(x * x, axis=-1, keepdims=True)
    o_ref[...] = x * lax.rsqrt(ms + EPS) * gf_ref[...]


def _combine(pos_flat, x1, gate_w, g_final, ys):
    T = x1.shape[0]
    tm = COMB_TM
    return pl.pallas_call(
        _combine_body,
        grid_spec=pltpu.PrefetchScalarGridSpec(
            num_scalar_prefetch=1,
            grid=(T // tm,),
            in_specs=[
                pl.BlockSpec((tm, D_MODEL), lambda i, pos: (i, 0)),
                pl.BlockSpec((tm, LANES), lambda i, pos: (i, 0)),
                pl.BlockSpec((1, D_MODEL), lambda i, pos: (0, 0)),
                pl.BlockSpec(memory_space=pl.ANY),
            ],
            out_specs=pl.BlockSpec((tm, D_MODEL), lambda i, pos: (i, 0)),
            scratch_shapes=[
                pltpu.VMEM((2, tm, D_MODEL), F32),
                pltpu.SemaphoreType.DMA(()),
            ],
        ),
        out_shape=jax.ShapeDtypeStruct((T, D_MODEL), F32),
        compiler_params=pltpu.CompilerParams(
            dimension_semantics=("arbitrary",), vmem_limit_bytes=VMEM_LIMIT),
    )(pos_flat, x1, gate_w, g_final.reshape(1, D_MODEL), ys)


def _moe_schedule(counts, n_items):
    padded = ((counts + MOE_SB - 1) // MOE_SB) * MOE_SB
    pends = jnp.cumsum(padded)
    pstarts = pends - padded
    chunks = (counts + MOE_R - 1) // MOE_R
    cends = jnp.cumsum(chunks)
    slot = jnp.arange(n_items, dtype=I32)
    total = cends[-1]
    live = slot < total
    owner = jnp.sum((slot[:, None] >= cends[None, :]).astype(I32), axis=1)
    last_owner = jnp.sum((total - 1 >= cends).astype(I32))
    owner = jnp.where(live, owner, last_owner)
    owner = jnp.minimum(owner, N_EXPERTS - 1)
    cidx = slot - (cends - chunks)[owner]
    item_start = jnp.where(live, pstarts[owner] + cidx * MOE_R, 0)
    item_n = jnp.where(live, jnp.clip(counts[owner] - cidx * MOE_R, 0, MOE_R), 0)
    return pstarts.astype(I32), pends.astype(I32), owner.astype(I32), item_start.astype(I32), \
        item_n.astype(I32)


def kernel(x, g_mix, w_in, ln_v_g, ln_v_b, w_spatial, b_spatial, w_branch_a, w_branch_b, w_out,
           g_ffn, w_router_group, b_router_group, w_router_expert, b_router_expert, w_gate, w_up,
           w_down, g_final):
    B, S, D = x.shape
    T = B * S
    cur = x.reshape(T, D)
    depth = g_mix.shape[0]
    for l in range(depth):
        proj = _proj(cur, g_mix[l], w_in[l].astype(BF16))
        yb = _attn(proj)
        w_r = jnp.zeros((D, LANES), F32)
        w_r = w_r.at[:, :N_GROUPS].set(w_router_group[l])
        w_r = w_r.at[:, N_GROUPS:N_GROUPS + N_EXPERTS].set(w_router_expert[l])
        b_r = jnp.zeros((1, LANES), F32)
        b_r = b_r.at[0, :N_GROUPS].set(b_router_group[l])
        b_r = b_r.at[0, N_GROUPS:N_GROUPS + N_EXPERTS].set(b_router_expert[l])
        x1, h2, logits = _merge(
            proj, yb, cur, ln_v_g[l].reshape(1, A_WIDTH), ln_v_b[l].reshape(1, A_WIDTH),
            w_spatial[l], b_spatial[l].T, w_branch_a[l].astype(BF16), w_branch_b[l].astype(BF16),
            w_out[l].astype(BF16), g_ffn[l].reshape(1, D), w_r, b_r)
        oi, gate_w, cnt = _route(logits)

        counts = cnt[0, :N_EXPERTS].astype(I32)
        n_rows = 2 * T + N_EXPERTS * MOE_SB
        n_items = N_EXPERTS + (2 * T) // MOE_R
        pstarts, pends, item_e, item_start, item_n = _moe_schedule(counts, n_items)
        pos = (pstarts[oi[:, 0:2]] + oi[:, 2:4]).reshape(2 * T)

        xs = _dispatch(pos, pends, h2, n_rows)
        ys = _moe(item_e, item_start, item_n, pends[N_EXPERTS - 1:], xs, w_gate[l], w_up[l], w_down[l])
        if l + 1 < depth:
            raise NotImplementedError("only the final layer fuses the closing rms_norm")
        cur = _combine(pos, x1, gate_w, g_final, ys)
    return cur.reshape(B, S, D)
```

```python
import functools

import jax
import jax.numpy as jnp
from jax import lax
from jax.experimental import pallas as pl
from jax.experimental.pallas import tpu as pltpu

F32 = jnp.float32
BF16 = jnp.bfloat16
I32 = jnp.int32

D_MODEL = 2048
CHUNK = 128
A_GROUPS = 8
A_WIDTH = 1024
SB_HEADS = 8
SB_HEAD_DIM = 128
SB_WIDTH = 1024
N_GROUPS = 4
EXPERTS_PER_GROUP = 8
N_EXPERTS = 32
D_EXPERT = 1024
EPS = 1e-6
PROJ_WIDTH = 2 * A_WIDTH + 3 * SB_WIDTH + 2 * D_MODEL
LANES = 128

OFF_U, OFF_V = 0, A_WIDTH
OFF_Q = 2 * A_WIDTH
OFF_K = OFF_Q + SB_WIDTH
OFF_VV = OFF_K + SB_WIDTH
OFF_GA = OFF_VV + SB_WIDTH
OFF_GB = OFF_GA + D_MODEL

VMEM_LIMIT = 56 * 1024 * 1024

PROJ_TM, PROJ_TN = 1024, 512
ATT_TQ, ATT_TK = 256, 256
MERGE_TM = 256
ROUTE_TB = 512
DISP_TM = 256
MOE_SB = 256
MOE_R = 1024
MOE_TH = 256
COMB_TM = 256


def _dot(a, b):
    return jnp.dot(a, b, preferred_element_type=F32)


def _proj_body(x_ref, g_ref, w_ref, o_ref, h_scr, *, n_gelu, n_plain):
    j = pl.program_id(1)

    @pl.when(j == 0)
    def _():
        x = x_ref[...]
        ms = jnp.mean(x * x, axis=-1, keepdims=True)
        h_scr[...] = (x * lax.rsqrt(ms + EPS) * g_ref[...]).astype(BF16)

    acc = _dot(h_scr[...], w_ref[...])

    @pl.when(j < n_gelu)
    def _():
        o_ref[...] = jax.nn.gelu(acc).astype(BF16)

    @pl.when((j >= n_gelu) & (j < n_gelu + n_plain))
    def _():
        o_ref[...] = acc.astype(BF16)

    @pl.when(j >= n_gelu + n_plain)
    def _():
        o_ref[...] = jax.nn.sigmoid(acc).astype(BF16)


def _proj(x2, g_mix, w_in_bf):
    T = x2.shape[0]
    tm = min(PROJ_TM, T)
    tn = PROJ_TN
    body = functools.partial(_proj_body, n_gelu=OFF_Q // tn, n_plain=(OFF_GA - OFF_Q) // tn)
    return pl.pallas_call(
        body,
        grid=(T // tm, PROJ_WIDTH // tn),
        in_specs=[
            pl.BlockSpec((tm, D_MODEL), lambda i, j: (i, 0)),
            pl.BlockSpec((1, D_MODEL), lambda i, j: (0, 0)),
            pl.BlockSpec((D_MODEL, tn), lambda i, j: (0, j)),
        ],
        out_specs=pl.BlockSpec((tm, tn), lambda i, j: (i, j)),
        out_shape=jax.ShapeDtypeStruct((T, PROJ_WIDTH), BF16),
        scratch_shapes=[pltpu.VMEM((tm, D_MODEL), BF16)],
        compiler_params=pltpu.CompilerParams(
            dimension_semantics=("arbitrary", "arbitrary"), vmem_limit_bytes=VMEM_LIMIT),
    )(x2, g_mix.reshape(1, D_MODEL), w_in_bf)


def _attn_tile(q, k, v, upper, carry, qpos, kpos):
    z = lax.dot_general(q, k, (((1,), (1,)), ((), ())), preferred_element_type=F32)
    z = z * (SB_HEAD_DIM ** -0.5)
    mask = kpos < qpos
    sp = jnp.maximum(z, 0.0) + jnp.log(1.0 + jnp.exp(-jnp.abs(z)))
    log_rem = jnp.where(mask, -sp, 0.0)
    hi = log_rem.astype(BF16)
    lo = (log_rem - hi.astype(F32)).astype(BF16)
    suffix = _dot(hi, upper) + _dot(lo, upper)
    w = jnp.where(mask, jnp.exp((z - sp) + suffix + carry), 0.0)
    contrib = _dot(w.astype(BF16), v)
    carry = carry + jnp.sum(log_rem, axis=1, keepdims=True)
    return contrib, carry


def _attn_body(q_ref, k_ref, v_ref, o_ref):
    i = pl.program_id(1)
    tq, tk = ATT_TQ, ATT_TK
    n_keys = k_ref.shape[0]
    q = q_ref[...]
    row = lax.broadcasted_iota(I32, (tk, tk), 0)
    col = lax.broadcasted_iota(I32, (tk, tk), 1)
    upper = (row > col).astype(BF16)
    qpos = i * tq + lax.broadcasted_iota(I32, (tq, tk), 0)
    kcol = lax.broadcasted_iota(I32, (tq, tk), 1)

    def tile(j, carry):
        off = pl.multiple_of(jnp.maximum(j, 0) * tk, tk)
        kpos = kcol + jnp.where(j >= 0, j * tk, n_keys)
        return _attn_tile(q, k_ref[pl.ds(off, tk), :], v_ref[pl.ds(off, tk), :],
                          upper, carry, qpos, kpos)

    def cond(state):
        j, _, _ = state
        return j >= 0

    def step(state):
        j, acc, carry = state
        c1, carry = tile(j, carry)
        c2, carry = tile(j - 1, carry)
        return j - 2, acc + c1 + c2, carry

    j0 = ((i + 1) * tq - 1) // tk
    init = (j0, jnp.zeros((tq, SB_HEAD_DIM), F32), jnp.zeros((tq, 1), F32))
    _, acc, _ = lax.while_loop(cond, step, init)
    o_ref[...] = acc.astype(BF16)


def _attn(proj):
    T = proj.shape[0]
    qb, kb, vb = OFF_Q // SB_HEAD_DIM, OFF_K // SB_HEAD_DIM, OFF_VV // SB_HEAD_DIM
    return pl.pallas_call(
        _attn_body,
        grid=(SB_HEADS, T // ATT_TQ),
        in_specs=[
            pl.BlockSpec((ATT_TQ, SB_HEAD_DIM), lambda h, i: (i, qb + h)),
            pl.BlockSpec((T, SB_HEAD_DIM), lambda h, i: (0, kb + h)),
            pl.BlockSpec((T, SB_HEAD_DIM), lambda h, i: (0, vb + h)),
        ],
        out_specs=pl.BlockSpec((ATT_TQ, SB_HEAD_DIM), lambda h, i: (i, h)),
        out_shape=jax.ShapeDtypeStruct((T, SB_WIDTH), BF16),
        compiler_params=pltpu.CompilerParams(
            dimension_semantics=("arbitrary", "arbitrary"), vmem_limit_bytes=VMEM_LIMIT),
    )(proj, proj, proj)


def _merge_body(u_ref, v_ref, ga0_ref, ga1_ref, gb0_ref, gb1_ref, yb_ref, x_ref, lng_ref, lnb_ref,
                ws_ref, bst_ref, wa_ref, wb_ref, wo_ref, gffn_ref, wr_ref, br_ref,
                x1_ref, h2_ref, lg_ref, ya_scr):
    tm = u_ref.shape[0]
    vf = v_ref[...].astype(F32)
    mu = jnp.mean(vf, axis=-1, keepdims=True)
    xc = vf - mu
    var = jnp.mean(xc * xc, axis=-1, keepdims=True)
    vln = (xc * lax.rsqrt(var + EPS) * lng_ref[...] + lnb_ref[...]).astype(BF16)

    row = lax.broadcasted_iota(I32, (CHUNK, CHUNK), 0)
    col = lax.broadcasted_iota(I32, (CHUNK, CHUNK), 1)
    tril = col <= row
    for g in range(A_GROUPS):
        wg = jnp.where(tril, ws_ref[g], 0.0).astype(BF16)
        bias = bst_ref[:, g:g + 1]
        cs = slice(g * LANES, (g + 1) * LANES)
        for c in range(tm // CHUNK):
            rs = slice(c * CHUNK, (c + 1) * CHUNK)
            mixed = _dot(wg, vln[rs, cs]) + bias
            ya_scr[rs, cs] = (u_ref[rs, cs].astype(F32) * mixed).astype(BF16)

    a = _dot(ya_scr[...], wa_ref[...])
    b = _dot(yb_ref[...], wb_ref[...])
    ga = jnp.concatenate([ga0_ref[...], ga1_ref[...]], axis=1).astype(F32)
    gb = jnp.concatenate([gb0_ref[...], gb1_ref[...]], axis=1).astype(F32)
    merged = (ga * a + gb * b).astype(BF16)
    x1 = x_ref[...] + _dot(merged, wo_ref[...])
    x1_ref[...] = x1
    ms = jnp.mean(x1 * x1, axis=-1, keepdims=True)
    h2 = x1 * lax.rsqrt(ms + EPS) * gffn_ref[...]
    h2_ref[...] = h2
    hi = h2.astype(BF16)
    lo = (h2 - hi.astype(F32)).astype(BF16)
    wr = wr_ref[...]
    whi = wr.astype(BF16)
    wlo = (wr - whi.astype(F32)).astype(BF16)
    lg_ref[...] = _dot(hi, whi) + _dot(hi, wlo) + _dot(lo, whi) + br_ref[...]


def _merge(proj, yb, x2, ln_g, ln_b, w_s, b_s_t, wa, wb, wo, g_ffn, w_r, b_r):
    T = x2.shape[0]
    tm = MERGE_TM
    const2 = lambda i: (0, 0)
    one = pl.Buffered(1)

    def colblk(width, off):
        return pl.BlockSpec((tm, width), lambda i: (i, off // width))

    return pl.pallas_call(
        _merge_body,
        grid=(T // tm,),
        in_specs=[
            colblk(A_WIDTH, OFF_U), colblk(A_WIDTH, OFF_V),
            colblk(A_WIDTH, OFF_GA), colblk(A_WIDTH, OFF_GA + A_WIDTH),
            colblk(A_WIDTH, OFF_GB), colblk(A_WIDTH, OFF_GB + A_WIDTH),
            pl.BlockSpec((tm, SB_WIDTH), lambda i: (i, 0)),
            pl.BlockSpec((tm, D_MODEL), lambda i: (i, 0)),
            pl.BlockSpec((1, A_WIDTH), const2), pl.BlockSpec((1, A_WIDTH), const2),
            pl.BlockSpec((A_GROUPS, CHUNK, CHUNK), lambda i: (0, 0, 0)),
            pl.BlockSpec((CHUNK, A_GROUPS), const2),
            pl.BlockSpec((A_WIDTH, D_MODEL), const2, pipeline_mode=one),
            pl.BlockSpec((SB_WIDTH, D_MODEL), const2, pipeline_mode=one),
            pl.BlockSpec((D_MODEL, D_MODEL), const2, pipeline_mode=one),
            pl.BlockSpec((1, D_MODEL), const2),
            pl.BlockSpec((D_MODEL, LANES), const2, pipeline_mode=one),
            pl.BlockSpec((1, LANES), const2),
        ],
        out_specs=[
            pl.BlockSpec((tm, D_MODEL), lambda i: (i, 0)),
            pl.BlockSpec((tm, D_MODEL), lambda i: (i, 0)),
            pl.BlockSpec((tm, LANES), lambda i: (i, 0)),
        ],
        out_shape=[
            jax.ShapeDtypeStruct((T, D_MODEL), F32),
            jax.ShapeDtypeStruct((T, D_MODEL), F32),
            jax.ShapeDtypeStruct((T, LANES), F32),
        ],
        scratch_shapes=[pltpu.VMEM((tm, A_WIDTH), BF16)],
        compiler_params=pltpu.CompilerParams(
            dimension_semantics=("arbitrary",), vmem_limit_bytes=VMEM_LIMIT),
    )(proj, proj, proj, proj, proj, proj, yb, x2, ln_g, ln_b, w_s, b_s_t, wa, wb, wo, g_ffn, w_r, b_r)


def _route_body(lg_ref, oi_ref, ow_ref, cnt_ref, carry_scr):
    i = pl.program_id(0)
    tb = lg_ref.shape[0]

    @pl.when(i == 0)
    def _():
        carry_scr[...] = jnp.zeros_like(carry_scr)

    lg = lg_ref[...]
    lane = lax.broadcasted_iota(I32, (tb, LANES), 1)
    neg = jnp.float32(-jnp.inf)
    big = jnp.int32(LANES)

    def first_max(mask):
        m = jnp.max(jnp.where(mask, lg, neg), axis=1, keepdims=True)
        idx = jnp.min(jnp.where(mask & (lg == m), lane, big), axis=1, keepdims=True)
        return m, idx

    gmask = lane < N_GROUPS
    gm, grp = first_max(gmask)
    pg_sel = 1.0 / jnp.sum(jnp.where(gmask, jnp.exp(lg - gm), 0.0), axis=1, keepdims=True)

    eid = lane - N_GROUPS
    emask = (eid >= 0) & (eid < N_EXPERTS) & ((eid // EXPERTS_PER_GROUP) == grp)
    m1, i1 = first_max(emask)
    emask2 = emask & (lane != i1)
    m2, i2 = first_max(emask2)
    t = jnp.exp(m2 - m1)
    p1 = 1.0 / (1.0 + t)
    p2 = t / (1.0 + t)
    e1 = i1 - N_GROUPS
    e2 = i2 - N_GROUPS

    sel1 = lane == e1
    sel2 = lane == e2
    onehot = (sel1 | sel2).astype(BF16)
    r = lax.broadcasted_iota(I32, (tb, tb), 0)
    c = lax.broadcasted_iota(I32, (tb, tb), 1)
    earlier = (c < r).astype(BF16)
    prefix = _dot(earlier, onehot) + carry_scr[...]
    rank1 = jnp.sum(jnp.where(sel1, prefix, 0.0), axis=1, keepdims=True).astype(I32)
    rank2 = jnp.sum(jnp.where(sel2, prefix, 0.0), axis=1, keepdims=True).astype(I32)
    carry_scr[...] = carry_scr[...] + jnp.sum(onehot.astype(F32), axis=0, keepdims=True)

    zero = jnp.zeros((tb, LANES), I32)
    oi_ref[...] = jnp.where(lane == 0, e1, jnp.where(lane == 1, e2,
                            jnp.where(lane == 2, rank1, jnp.where(lane == 3, rank2, zero))))
    ow_ref[...] = jnp.where(lane == 0, pg_sel * p1, jnp.where(lane == 1, pg_sel * p2, 0.0))
    cnt_ref[...] = carry_scr[...]


def _route(logits):
    T = logits.shape[0]
    tb = min(ROUTE_TB, T)
    return pl.pallas_call(
        _route_body,
        grid=(T // tb,),
        in_specs=[pl.BlockSpec((tb, LANES), lambda i: (i, 0))],
        out_specs=[
            pl.BlockSpec((tb, LANES), lambda i: (i, 0)),
            pl.BlockSpec((tb, LANES), lambda i: (i, 0)),
            pl.BlockSpec((1, LANES), lambda i: (0, 0)),
        ],
        out_shape=[
            jax.ShapeDtypeStruct((T, LANES), I32),
            jax.ShapeDtypeStruct((T, LANES), F32),
            jax.ShapeDtypeStruct((1, LANES), F32),
        ],
        scratch_shapes=[pltpu.VMEM((1, LANES), F32)],
        compiler_params=pltpu.CompilerParams(dimension_semantics=("arbitrary",)),
    )(logits)


def _dispatch_body(pos_ref, pend_ref, h2_ref, xs_ref, zbuf, sem, zsem):
    i = pl.program_id(0)
    tm = h2_ref.shape[0]

    def tail_copy(e):
        start = pl.multiple_of(pend_ref[e] - MOE_SB, MOE_SB)
        return pltpu.make_async_copy(zbuf, xs_ref.at[pl.ds(start, MOE_SB), :], zsem)

    @pl.when(i == 0)
    def _():
        zbuf[...] = jnp.zeros_like(zbuf)
        for e in range(N_EXPERTS):
            prev = pend_ref[e - 1] if e > 0 else 0

            @pl.when(pend_ref[e] > prev)
            def _():
                tail_copy(e).start()
        for e in range(N_EXPERTS):
            prev = pend_ref[e - 1] if e > 0 else 0

            @pl.when(pend_ref[e] > prev)
            def _():
                tail_copy(e).wait()

        used = pend_ref[N_EXPERTS - 1]
        n_slack = (xs_ref.shape[0] - used) // MOE_SB

        def slack_copy(b):
            start = pl.multiple_of(used + b * MOE_SB, MOE_SB)
            return pltpu.make_async_copy(zbuf, xs_ref.at[pl.ds(start, MOE_SB), :], zsem)

        def slack_start(b, _):
            slack_copy(b).start()
            return 0

        def slack_wait(b, _):
            slack_copy(b).wait()
            return 0

        lax.fori_loop(0, n_slack, slack_start, 0)
        lax.fori_loop(0, n_slack, slack_wait, 0)

    def row_copy(r, k):
        p = pos_ref[2 * (i * tm + r) + k]
        return pltpu.make_async_copy(h2_ref.at[pl.ds(r, 1), :], xs_ref.at[pl.ds(p, 1), :], sem)

    def issue(r, _):
        row_copy(r, 0).start()
        row_copy(r, 1).start()
        return 0

    def drain(r, _):
        row_copy(r, 0).wait()
        row_copy(r, 1).wait()
        return 0

    lax.fori_loop(0, tm, issue, 0)
    lax.fori_loop(0, tm, drain, 0)


def _dispatch(pos_flat, pends, h2, n_rows):
    T = h2.shape[0]
    tm = DISP_TM
    return pl.pallas_call(
        _dispatch_body,
        grid_spec=pltpu.PrefetchScalarGridSpec(
            num_scalar_prefetch=2,
            grid=(T // tm,),
            in_specs=[pl.BlockSpec((tm, D_MODEL), lambda i, pos, pe: (i, 0))],
            out_specs=pl.BlockSpec(memory_space=pl.ANY),
            scratch_shapes=[
                pltpu.VMEM((MOE_SB, D_MODEL), F32),
                pltpu.SemaphoreType.DMA(()),
                pltpu.SemaphoreType.DMA(()),
            ],
        ),
        out_shape=jax.ShapeDtypeStruct((n_rows, D_MODEL), F32),
        compiler_params=pltpu.CompilerParams(
            dimension_semantics=("arbitrary",), has_side_effects=True),
    )(pos_flat, pends, h2)


def _moe_body(ie_ref, is_ref, in_ref, used_ref, xs_ref, wg_ref, wu_ref, wd_ref, ys_ref,
              xbuf, acc, sem_in, sem_out):
    s = pl.program_id(0)
    h = pl.program_id(1)
    n_h = pl.num_programs(1)
    n = in_ref[s]
    start = is_ref[s]
    nsub = (n + MOE_SB - 1) // MOE_SB

    def in_copy(b):
        off = pl.multiple_of(b * MOE_SB, MOE_SB)
        src = pl.multiple_of(start + off, MOE_SB)
        return pltpu.make_async_copy(xs_ref.at[pl.ds(src, MOE_SB), :],
                                     acc.at[pl.ds(off, MOE_SB), :], sem_in)

    def out_copy(b):
        off = pl.multiple_of(b * MOE_SB, MOE_SB)
        dst = pl.multiple_of(start + off, MOE_SB)
        return pltpu.make_async_copy(acc.at[pl.ds(off, MOE_SB), :],
                                     ys_ref.at[pl.ds(dst, MOE_SB), :], sem_out)

    def loop(fn):
        def body(b, _):
            fn(b)
            return 0
        lax.fori_loop(0, nsub, body, 0)

    @pl.when((s == 0) & (h == 0))
    def _():
        used = used_ref[0]
        n_slack = (ys_ref.shape[0] - used) // MOE_SB
        acc[pl.ds(0, MOE_SB), :] = jnp.zeros((MOE_SB, D_MODEL), F32)

        def slack_copy(b):
            dst = pl.multiple_of(used + b * MOE_SB, MOE_SB)
            return pltpu.make_async_copy(acc.at[pl.ds(0, MOE_SB), :],
                                         ys_ref.at[pl.ds(dst, MOE_SB), :], sem_out)

        def slack_start(b, _):
            slack_copy(b).start()
            return 0

        def slack_wait(b, _):
            slack_copy(b).wait()
            return 0

        lax.fori_loop(0, n_slack, slack_start, 0)
        lax.fori_loop(0, n_slack, slack_wait, 0)

    @pl.when(h == 0)
    def _():
        loop(lambda b: in_copy(b).start())
        loop(lambda b: in_copy(b).wait())

        def cast(b):
            off = pl.multiple_of(b * MOE_SB, MOE_SB)
            xbuf[pl.ds(off, MOE_SB), :] = acc[pl.ds(off, MOE_SB), :].astype(BF16)
        loop(cast)

    def compute(b, wg, wu, wd):
        off = pl.multiple_of(b * MOE_SB, MOE_SB)
        xb = xbuf[pl.ds(off, MOE_SB), :]
        gate = _dot(xb, wg)
        up = _dot(xb, wu)
        hidden = (gate * jax.nn.sigmoid(gate) * up).astype(BF16)
        contrib = _dot(hidden, wd)

        @pl.when(h == 0)
        def _():
            acc[pl.ds(off, MOE_SB), :] = contrib

        @pl.when(h > 0)
        def _():
            acc[pl.ds(off, MOE_SB), :] = acc[pl.ds(off, MOE_SB), :] + contrib

    @pl.when(n > 0)
    def _():
        wg = wg_ref[...].astype(BF16)
        wu = wu_ref[...].astype(BF16)
        wd = wd_ref[...].astype(BF16)
        loop(lambda b: compute(b, wg, wu, wd))

    @pl.when(h == n_h - 1)
    def _():
        loop(lambda b: out_copy(b).start())
        loop(lambda b: out_copy(b).wait())


def _moe(item_e, item_start, item_n, used, xs, w_gate, w_up, w_down):
    n_rows = xs.shape[0]
    n_items = item_e.shape[0]
    th = MOE_TH
    return pl.pallas_call(
        _moe_body,
        grid_spec=pltpu.PrefetchScalarGridSpec(
            num_scalar_prefetch=4,
            grid=(n_items, D_EXPERT // th),
            in_specs=[
                pl.BlockSpec(memory_space=pl.ANY),
                pl.BlockSpec((None, D_MODEL, th), lambda s, h, ie, st, nn, us: (ie[s], 0, h)),
                pl.BlockSpec((None, D_MODEL, th), lambda s, h, ie, st, nn, us: (ie[s], 0, h)),
                pl.BlockSpec((None, th, D_MODEL), lambda s, h, ie, st, nn, us: (ie[s], h, 0)),
            ],
            out_specs=pl.BlockSpec(memory_space=pl.ANY),
            scratch_shapes=[
                pltpu.VMEM((MOE_R, D_MODEL), BF16),
                pltpu.VMEM((MOE_R, D_MODEL), F32),
                pltpu.SemaphoreType.DMA(()),
                pltpu.SemaphoreType.DMA(()),
            ],
        ),
        out_shape=jax.ShapeDtypeStruct((n_rows, D_MODEL), F32),
        compiler_params=pltpu.CompilerParams(
            dimension_semantics=("arbitrary", "arbitrary"), vmem_limit_bytes=VMEM_LIMIT,
            has_side_effects=True),
    )(item_e, item_start, item_n, used, xs, w_gate, w_up, w_down)


def _combine_body(pos_ref, x1_ref, gw_ref, gf_ref, ys_ref, o_ref, ybuf, sem):
    i = pl.program_id(0)
    tm = x1_ref.shape[0]

    def row_copy(r, k):
        p = pos_ref[2 * (i * tm + r) + k]
        return pltpu.make_async_copy(ys_ref.at[pl.ds(p, 1), :], ybuf.at[k, pl.ds(r, 1), :], sem)

    def issue(r, _):
        row_copy(r, 0).start()
        row_copy(r, 1).start()
        return 0

    def drain(r, _):
        row_copy(r, 0).wait()
        row_copy(r, 1).wait()
        return 0

    lax.fori_loop(0, tm, issue, 0)
    lax.fori_loop(0, tm, drain, 0)

    gw = gw_ref[...]
    x = x1_ref[...] + gw[:, 0:1] * ybuf[0] + gw[:, 1:2] * ybuf[1]
    ms = jnp.mean(x * x, axis=-1, keepdims=True)
    o_ref[...] = x * lax.rsqrt(ms + EPS) * gf_ref[...]


def _combine(pos_flat, x1, gate_w, g_final, ys):
    T = x1.shape[0]
    tm = COMB_TM
    return pl.pallas_call(
        _combine_body,
        grid_spec=pltpu.PrefetchScalarGridSpec(
            num_scalar_prefetch=1,
            grid=(T // tm,),
            in_specs=[
                pl.BlockSpec((tm, D_MODEL), lambda i, pos: (i, 0)),
                pl.BlockSpec((tm, LANES), lambda i, pos: (i, 0)),
                pl.BlockSpec((1, D_MODEL), lambda i, pos: (0, 0)),
                pl.BlockSpec(memory_space=pl.ANY),
            ],
            out_specs=pl.BlockSpec((tm, D_MODEL), lambda i, pos: (i, 0)),
            scratch_shapes=[
                pltpu.VMEM((2, tm, D_MODEL), F32),
                pltpu.SemaphoreType.DMA(()),
            ],
        ),
        out_shape=jax.ShapeDtypeStruct((T, D_MODEL), F32),
        compiler_params=pltpu.CompilerParams(
            dimension_semantics=("arbitrary",), vmem_limit_bytes=VMEM_LIMIT),
    )(pos_flat, x1, gate_w, g_final.reshape(1, D_MODEL), ys)


def _moe_schedule(counts, n_items):
    padded = ((counts + MOE_SB - 1) // MOE_SB) * MOE_SB
    pends = jnp.cumsum(padded)
    pstarts = pends - padded
    chunks = (counts + MOE_R - 1) // MOE_R
    cends = jnp.cumsum(chunks)
    slot = jnp.arange(n_items, dtype=I32)
    total = cends[-1]
    live = slot < total
    owner = jnp.sum((slot[:, None] >= cends[None, :]).astype(I32), axis=1)
    last_owner = jnp.sum((total - 1 >= cends).astype(I32))
    owner = jnp.where(live, owner, last_owner)
    owner = jnp.minimum(owner, N_EXPERTS - 1)
    cidx = slot - (cends - chunks)[owner]
    item_start = jnp.where(live, pstarts[owner] + cidx * MOE_R, 0)
    item_n = jnp.where(live, jnp.clip(counts[owner] - cidx * MOE_R, 0, MOE_R), 0)
    return pstarts.astype(I32), pends.astype(I32), owner.astype(I32), item_start.astype(I32), \
        item_n.astype(I32)


def kernel(x, g_mix, w_in, ln_v_g, ln_v_b, w_spatial, b_spatial, w_branch_a, w_branch_b, w_out,
           g_ffn, w_router_group, b_router_group, w_router_expert, b_router_expert, w_gate, w_up,
           w_down, g_final):
    B, S, D = x.shape
    T = B * S
    cur = x.reshape(T, D)
    depth = g_mix.shape[0]
    for l in range(depth):
        proj = _proj(cur, g_mix[l], w_in[l].astype(BF16))
        yb = _attn(proj)
        w_r = jnp.zeros((D, LANES), F32)
        w_r = w_r.at[:, :N_GROUPS].set(w_router_group[l])
        w_r = w_r.at[:, N_GROUPS:N_GROUPS + N_EXPERTS].set(w_router_expert[l])
        b_r = jnp.zeros((1, LANES), F32)
        b_r = b_r.at[0, :N_GROUPS].set(b_router_group[l])
        b_r = b_r.at[0, N_GROUPS:N_GROUPS + N_EXPERTS].set(b_router_expert[l])
        x1, h2, logits = _merge(
            proj, yb, cur, ln_v_g[l].reshape(1, A_WIDTH), ln_v_b[l].reshape(1, A_WIDTH),
            w_spatial[l], b_spatial[l].T, w_branch_a[l].astype(BF16), w_branch_b[l].astype(BF16),
            w_out[l].astype(BF16), g_ffn[l].reshape(1, D), w_r, b_r)
        oi, gate_w, cnt = _route(logits)

        counts = cnt[0, :N_EXPERTS].astype(I32)
        n_rows = 2 * T + N_EXPERTS * MOE_SB
        n_items = N_EXPERTS + (2 * T) // MOE_R
        pstarts, pends, item_e, item_start, item_n = _moe_schedule(counts, n_items)
        pos = (pstarts[oi[:, 0:2]] + oi[:, 2:4]).reshape(2 * T)

        xs = _dispatch(pos, pends, h2, n_rows)
        ys = _moe(item_e, item_start, item_n, pends[N_EXPERTS - 1:], xs, w_gate[l], w_up[l], w_down[l])
        if l + 1 < depth:
            raise NotImplementedError("only the final layer fuses the closing rms_norm")
        cur = _combine(pos, x1, gate_w, g_final, ys)
    return cur.reshape(B, S, D)
```

```python
import functools

import jax
import jax.numpy as jnp
from jax import lax
from jax.experimental import pallas as pl
from jax.experimental.pallas import tpu as pltpu

F32 = jnp.float32
BF16 = jnp.bfloat16
I32 = jnp.int32

D_MODEL = 2048
CHUNK = 128
A_GROUPS = 8
A_WIDTH = 1024
SB_HEADS = 8
SB_HEAD_DIM = 128
SB_WIDTH = 1024
N_GROUPS = 4
EXPERTS_PER_GROUP = 8
N_EXPERTS = 32
D_EXPERT = 1024
EPS = 1e-6
PROJ_WIDTH = 2 * A_WIDTH + 3 * SB_WIDTH + 2 * D_MODEL
LANES = 128
EXP_ZERO_BELOW = -104.0

OFF_U, OFF_V = 0, A_WIDTH
OFF_Q = 2 * A_WIDTH
OFF_K = OFF_Q + SB_WIDTH
OFF_VV = OFF_K + SB_WIDTH
OFF_GA = OFF_VV + SB_WIDTH
OFF_GB = OFF_GA + D_MODEL

VMEM_LIMIT = 56 * 1024 * 1024

PROJ_TM, PROJ_TN = 1024, 512
ATT_TQ, ATT_TK = 256, 256
MERGE_TM = 256
ROUTE_TB = 512
DISP_TM = 256
MOE_SB = 256
MOE_R = 1024
MOE_TH = 256
COMB_TM = 256


def _dot(a, b):
    return jnp.dot(a, b, preferred_element_type=F32)


def _proj_body(x_ref, g_ref, w_ref, o_ref, h_scr, *, n_gelu, n_plain):
    j = pl.program_id(1)

    @pl.when(j == 0)
    def _():
        x = x_ref[...]
        ms = jnp.mean(x * x, axis=-1, keepdims=True)
        h_scr[...] = (x * lax.rsqrt(ms + EPS) * g_ref[...]).astype(BF16)

    acc = _dot(h_scr[...], w_ref[...])

    @pl.when(j < n_gelu)
    def _():
        o_ref[...] = jax.nn.gelu(acc).astype(BF16)

    @pl.when((j >= n_gelu) & (j < n_gelu + n_plain))
    def _():
        o_ref[...] = acc.astype(BF16)

    @pl.when(j >= n_gelu + n_plain)
    def _():
        o_ref[...] = jax.nn.sigmoid(acc).astype(BF16)


def _proj(x2, g_mix, w_in_bf):
    T = x2.shape[0]
    tm = min(PROJ_TM, T)
    tn = PROJ_TN
    body = functools.partial(_proj_body, n_gelu=OFF_Q // tn, n_plain=(OFF_GA - OFF_Q) // tn)
    return pl.pallas_call(
        body,
        grid=(T // tm, PROJ_WIDTH // tn),
        in_specs=[
            pl.BlockSpec((tm, D_MODEL), lambda i, j: (i, 0)),
            pl.BlockSpec((1, D_MODEL), lambda i, j: (0, 0)),
            pl.BlockSpec((D_MODEL, tn), lambda i, j: (0, j)),
        ],
        out_specs=pl.BlockSpec((tm, tn), lambda i, j: (i, j)),
        out_shape=jax.ShapeDtypeStruct((T, PROJ_WIDTH), BF16),
        scratch_shapes=[pltpu.VMEM((tm, D_MODEL), BF16)],
        compiler_params=pltpu.CompilerParams(
            dimension_semantics=("arbitrary", "arbitrary"), vmem_limit_bytes=VMEM_LIMIT),
    )(x2, g_mix.reshape(1, D_MODEL), w_in_bf)


def _attn_tile(q, k, v, upper, carry, qpos, kpos):
    z = lax.dot_general(q, k, (((1,), (1,)), ((), ())), preferred_element_type=F32)
    z = z * (SB_HEAD_DIM ** -0.5)
    mask = kpos < qpos
    sp = jnp.maximum(z, 0.0) + jnp.log(1.0 + jnp.exp(-jnp.abs(z)))
    log_rem = jnp.where(mask, -sp, 0.0)
    hi = log_rem.astype(BF16)
    lo = (log_rem - hi.astype(F32)).astype(BF16)
    suffix = _dot(hi, upper) + _dot(lo, upper)
    w = jnp.where(mask, jnp.exp((z - sp) + suffix + carry), 0.0)
    contrib = _dot(w.astype(BF16), v)
    carry = carry + jnp.sum(log_rem, axis=1, keepdims=True)
    return contrib, carry


def _attn_body(q_ref, k_ref, v_ref, o_ref):
    i = pl.program_id(1)
    tq, tk = ATT_TQ, ATT_TK
    n_keys = k_ref.shape[0]
    q = q_ref[...]
    row = lax.broadcasted_iota(I32, (tk, tk), 0)
    col = lax.broadcasted_iota(I32, (tk, tk), 1)
    upper = (row > col).astype(BF16)
    qpos = i * tq + lax.broadcasted_iota(I32, (tq, tk), 0)
    kcol = lax.broadcasted_iota(I32, (tq, tk), 1)

    def tile(j, carry):
        off = pl.multiple_of(jnp.maximum(j, 0) * tk, tk)
        kpos = kcol + jnp.where(j >= 0, j * tk, n_keys)
        return _attn_tile(q, k_ref[pl.ds(off, tk), :], v_ref[pl.ds(off, tk), :],
                          upper, carry, qpos, kpos)

    def cond(state):
        j, _, carry = state
        return (j >= 0) & (jnp.max(carry) > EXP_ZERO_BELOW)

    def step(state):
        j, acc, carry = state
        c1, carry = tile(j, carry)
        c2, carry = tile(j - 1, carry)
        return j - 2, acc + c1 + c2, carry

    j0 = ((i + 1) * tq - 1) // tk
    init = (j0, jnp.zeros((tq, SB_HEAD_DIM), F32), jnp.zeros((tq, 1), F32))
    _, acc, _ = lax.while_loop(cond, step, init)
    o_ref[...] = acc.astype(BF16)


def _attn(proj):
    T = proj.shape[0]
    qb, kb, vb = OFF_Q // SB_HEAD_DIM, OFF_K // SB_HEAD_DIM, OFF_VV // SB_HEAD_DIM
    return pl.pallas_call(
        _attn_body,
        grid=(SB_HEADS, T // ATT_TQ),
        in_specs=[
            pl.BlockSpec((ATT_TQ, SB_HEAD_DIM), lambda h, i: (i, qb + h)),
            pl.BlockSpec((T, SB_HEAD_DIM), lambda h, i: (0, kb + h)),
            pl.BlockSpec((T, SB_HEAD_DIM), lambda h, i: (0, vb + h)),
        ],
        out_specs=pl.BlockSpec((ATT_TQ, SB_HEAD_DIM), lambda h, i: (i, h)),
        out_shape=jax.ShapeDtypeStruct((T, SB_WIDTH), BF16),
        compiler_params=pltpu.CompilerParams(
            dimension_semantics=("arbitrary", "arbitrary"), vmem_limit_bytes=VMEM_LIMIT),
    )(proj, proj, proj)


def _merge_body(u_ref, v_ref, ga0_ref, ga1_ref, gb0_ref, gb1_ref, yb_ref, x_ref, lng_ref, lnb_ref,
                ws_ref, bst_ref, wa_ref, wb_ref, wo_ref, gffn_ref, wr_ref, br_ref,
                x1_ref, h2_ref, lg_ref, ya_scr):
    tm = u_ref.shape[0]
    vf = v_ref[...].astype(F32)
    mu = jnp.mean(vf, axis=-1, keepdims=True)
    xc = vf - mu
    var = jnp.mean(xc * xc, axis=-1, keepdims=True)
    vln = (xc * lax.rsqrt(var + EPS) * lng_ref[...] + lnb_ref[...]).astype(BF16)

    row = lax.broadcasted_iota(I32, (CHUNK, CHUNK), 0)
    col = lax.broadcasted_iota(I32, (CHUNK, CHUNK), 1)
    tril = col <= row
    for g in range(A_GROUPS):
        wg = jnp.where(tril, ws_ref[g], 0.0).astype(BF16)
        bias = bst_ref[:, g:g + 1]
        cs = slice(g * LANES, (g + 1) * LANES)
        for c in range(tm // CHUNK):
            rs = slice(c * CHUNK, (c + 1) * CHUNK)
            mixed = _dot(wg, vln[rs, cs]) + bias
            ya_scr[rs, cs] = (u_ref[rs, cs].astype(F32) * mixed).astype(BF16)

    a = _dot(ya_scr[...], wa_ref[...])
    b = _dot(yb_ref[...], wb_ref[...])
    ga = jnp.concatenate([ga0_ref[...], ga1_ref[...]], axis=1).astype(F32)
    gb = jnp.concatenate([gb0_ref[...], gb1_ref[...]], axis=1).astype(F32)
    merged = (ga * a + gb * b).astype(BF16)
    x1 = x_ref[...] + _dot(merged, wo_ref[...])
    x1_ref[...] = x1
    ms = jnp.mean(x1 * x1, axis=-1, keepdims=True)
    h2 = x1 * lax.rsqrt(ms + EPS) * gffn_ref[...]
    h2_ref[...] = h2
    hi = h2.astype(BF16)
    lo = (h2 - hi.astype(F32)).astype(BF16)
    wr = wr_ref[...]
    whi = wr.astype(BF16)
    wlo = (wr - whi.astype(F32)).astype(BF16)
    lg_ref[...] = _dot(hi, whi) + _dot(hi, wlo) + _dot(lo, whi) + br_ref[...]


def _merge(proj, yb, x2, ln_g, ln_b, w_s, b_s_t, wa, wb, wo, g_ffn, w_r, b_r):
    T = x2.shape[0]
    tm = MERGE_TM
    const2 = lambda i: (0, 0)
    one = pl.Buffered(1)

    def colblk(width, off):
        return pl.BlockSpec((tm, width), lambda i: (i, off // width))

    return pl.pallas_call(
        _merge_body,
        grid=(T // tm,),
        in_specs=[
            colblk(A_WIDTH, OFF_U), colblk(A_WIDTH, OFF_V),
            colblk(A_WIDTH, OFF_GA), colblk(A_WIDTH, OFF_GA + A_WIDTH),
            colblk(A_WIDTH, OFF_GB), colblk(A_WIDTH, OFF_GB + A_WIDTH),
            pl.BlockSpec((tm, SB_WIDTH), lambda i: (i, 0)),
            pl.BlockSpec((tm, D_MODEL), lambda i: (i, 0)),
            pl.BlockSpec((1, A_WIDTH), const2), pl.BlockSpec((1, A_WIDTH), const2),
            pl.BlockSpec((A_GROUPS, CHUNK, CHUNK), lambda i: (0, 0, 0)),
            pl.BlockSpec((CHUNK, A_GROUPS), const2),
            pl.BlockSpec((A_WIDTH, D_MODEL), const2, pipeline_mode=one),
            pl.BlockSpec((SB_WIDTH, D_MODEL), const2, pipeline_mode=one),
            pl.BlockSpec((D_MODEL, D_MODEL), const2, pipeline_mode=one),
            pl.BlockSpec((1, D_MODEL), const2),
            pl.BlockSpec((D_MODEL, LANES), const2, pipeline_mode=one),
            pl.BlockSpec((1, LANES), const2),
        ],
        out_specs=[
            pl.BlockSpec((tm, D_MODEL), lambda i: (i, 0)),
            pl.BlockSpec((tm, D_MODEL), lambda i: (i, 0)),
            pl.BlockSpec((tm, LANES), lambda i: (i, 0)),
        ],
        out_shape=[
            jax.ShapeDtypeStruct((T, D_MODEL), F32),
            jax.ShapeDtypeStruct((T, D_MODEL), F32),
            jax.ShapeDtypeStruct((T, LANES), F32),
        ],
        scratch_shapes=[pltpu.VMEM((tm, A_WIDTH), BF16)],
        compiler_params=pltpu.CompilerParams(
            dimension_semantics=("arbitrary",), vmem_limit_bytes=VMEM_LIMIT),
    )(proj, proj, proj, proj, proj, proj, yb, x2, ln_g, ln_b, w_s, b_s_t, wa, wb, wo, g_ffn, w_r, b_r)


def _route_body(lg_ref, oi_ref, ow_ref, cnt_ref, carry_scr):
    i = pl.program_id(0)
    tb = lg_ref.shape[0]

    @pl.when(i == 0)
    def _():
        carry_scr[...] = jnp.zeros_like(carry_scr)

    lg = lg_ref[...]
    lane = lax.broadcasted_iota(I32, (tb, LANES), 1)
    neg = jnp.float32(-jnp.inf)
    big = jnp.int32(LANES)

    def first_max(mask):
        m = jnp.max(jnp.where(mask, lg, neg), axis=1, keepdims=True)
        idx = jnp.min(jnp.where(mask & (lg == m), lane, big), axis=1, keepdims=True)
        return m, idx

    gmask = lane < N_GROUPS
    gm, grp = first_max(gmask)
    pg_sel = 1.0 / jnp.sum(jnp.where(gmask, jnp.exp(lg - gm), 0.0), axis=1, keepdims=True)

    eid = lane - N_GROUPS
    emask = (eid >= 0) & (eid < N_EXPERTS) & ((eid // EXPERTS_PER_GROUP) == grp)
    m1, i1 = first_max(emask)
    emask2 = emask & (lane != i1)
    m2, i2 = first_max(emask2)
    t = jnp.exp(m2 - m1)
    p1 = 1.0 / (1.0 + t)
    p2 = t / (1.0 + t)
    e1 = i1 - N_GROUPS
    e2 = i2 - N_GROUPS

    sel1 = lane == e1
    sel2 = lane == e2
    onehot = (sel1 | sel2).astype(BF16)
    r = lax.broadcasted_iota(I32, (tb, tb), 0)
    c = lax.broadcasted_iota(I32, (tb, tb), 1)
    earlier = (c < r).astype(BF16)
    prefix = _dot(earlier, onehot) + carry_scr[...]
    rank1 = jnp.sum(jnp.where(sel1, prefix, 0.0), axis=1, keepdims=True).astype(I32)
    rank2 = jnp.sum(jnp.where(sel2, prefix, 0.0), axis=1, keepdims=True).astype(I32)
    carry_scr[...] = carry_scr[...] + jnp.sum(onehot.astype(F32), axis=0, keepdims=True)

    zero = jnp.zeros((tb, LANES), I32)
    oi_ref[...] = jnp.where(lane == 0, e1, jnp.where(lane == 1, e2,
                            jnp.where(lane == 2, rank1, jnp.where(lane == 3, rank2, zero))))
    ow_ref[...] = jnp.where(lane == 0, pg_sel * p1, jnp.where(lane == 1, pg_sel * p2, 0.0))
    cnt_ref[...] = carry_scr[...]


def _route(logits):
    T = logits.shape[0]
    tb = min(ROUTE_TB, T)
    return pl.pallas_call(
        _route_body,
        grid=(T // tb,),
        in_specs=[pl.BlockSpec((tb, LANES), lambda i: (i, 0))],
        out_specs=[
            pl.BlockSpec((tb, LANES), lambda i: (i, 0)),
            pl.BlockSpec((tb, LANES), lambda i: (i, 0)),
            pl.BlockSpec((1, LANES), lambda i: (0, 0)),
        ],
        out_shape=[
            jax.ShapeDtypeStruct((T, LANES), I32),
            jax.ShapeDtypeStruct((T, LANES), F32),
            jax.ShapeDtypeStruct((1, LANES), F32),
        ],
        scratch_shapes=[pltpu.VMEM((1, LANES), F32)],
        compiler_params=pltpu.CompilerParams(dimension_semantics=("arbitrary",)),
    )(logits)


def _dispatch_body(pos_ref, pend_ref, h2_ref, xs_ref, zbuf, sem, zsem):
    i = pl.program_id(0)
    tm = h2_ref.shape[0]

    def tail_copy(e):
        start = pl.multiple_of(pend_ref[e] - MOE_SB, MOE_SB)
        return pltpu.make_async_copy(zbuf, xs_ref.at[pl.ds(start, MOE_SB), :], zsem)

    @pl.when(i == 0)
    def _():
        zbuf[...] = jnp.zeros_like(zbuf)
        for e in range(N_EXPERTS):
            prev = pend_ref[e - 1] if e > 0 else 0

            @pl.when(pend_ref[e] > prev)
            def _():
                tail_copy(e).start()
        for e in range(N_EXPERTS):
            prev = pend_ref[e - 1] if e > 0 else 0

            @pl.when(pend_ref[e] > prev)
            def _():
                tail_copy(e).wait()

        used = pend_ref[N_EXPERTS - 1]
        n_slack = (xs_ref.shape[0] - used) // MOE_SB

        def slack_copy(b):
            start = pl.multiple_of(used + b * MOE_SB, MOE_SB)
            return pltpu.make_async_copy(zbuf, xs_ref.at[pl.ds(start, MOE_SB), :], zsem)

        def slack_start(b, _):
            slack_copy(b).start()
            return 0

        def slack_wait(b, _):
            slack_copy(b).wait()
            return 0

        lax.fori_loop(0, n_slack, slack_start, 0)
        lax.fori_loop(0, n_slack, slack_wait, 0)

    def row_copy(r, k):
        p = pos_ref[2 * (i * tm + r) + k]
        return pltpu.make_async_copy(h2_ref.at[pl.ds(r, 1), :], xs_ref.at[pl.ds(p, 1), :], sem)

    def issue(r, _):
        row_copy(r, 0).start()
        row_copy(r, 1).start()
        return 0

    def drain(r, _):
        row_copy(r, 0).wait()
        row_copy(r, 1).wait()
        return 0

    lax.fori_loop(0, tm, issue, 0)
    lax.fori_loop(0, tm, drain, 0)


def _dispatch(pos_flat, pends, h2, n_rows):
    T = h2.shape[0]
    tm = DISP_TM
    return pl.pallas_call(
        _dispatch_body,
        grid_spec=pltpu.PrefetchScalarGridSpec(
            num_scalar_prefetch=2,
            grid=(T // tm,),
            in_specs=[pl.BlockSpec((tm, D_MODEL), lambda i, pos, pe: (i, 0))],
            out_specs=pl.BlockSpec(memory_space=pl.ANY),
            scratch_shapes=[
                pltpu.VMEM((MOE_SB, D_MODEL), F32),
                pltpu.SemaphoreType.DMA(()),
                pltpu.SemaphoreType.DMA(()),
            ],
        ),
        out_shape=jax.ShapeDtypeStruct((n_rows, D_MODEL), F32),
        compiler_params=pltpu.CompilerParams(
            dimension_semantics=("arbitrary",), has_side_effects=True),
    )(pos_flat, pends, h2)


def _moe_body(ie_ref, is_ref, in_ref, used_ref, xs_ref, wg_ref, wu_ref, wd_ref, ys_ref,
              xbuf, acc, sem_in, sem_out):
    s = pl.program_id(0)
    h = pl.program_id(1)
    n_h = pl.num_programs(1)
    n = in_ref[s]
    start = is_ref[s]
    nsub = (n + MOE_SB - 1) // MOE_SB

    def in_copy(b):
        off = pl.multiple_of(b * MOE_SB, MOE_SB)
        src = pl.multiple_of(start + off, MOE_SB)
        return pltpu.make_async_copy(xs_ref.at[pl.ds(src, MOE_SB), :],
                                     acc.at[pl.ds(off, MOE_SB), :], sem_in)

    def out_copy(b):
        off = pl.multiple_of(b * MOE_SB, MOE_SB)
        dst = pl.multiple_of(start + off, MOE_SB)
        return pltpu.make_async_copy(acc.at[pl.ds(off, MOE_SB), :],
                                     ys_ref.at[pl.ds(dst, MOE_SB), :], sem_out)

    def loop(fn):
        def body(b, _):
            fn(b)
            return 0
        lax.fori_loop(0, nsub, body, 0)

    @pl.when((s == 0) & (h == 0))
    def _():
        used = used_ref[0]
        n_slack = (ys_ref.shape[0] - used) // MOE_SB
        acc[pl.ds(0, MOE_SB), :] = jnp.zeros((MOE_SB, D_MODEL), F32)

        def slack_copy(b):
            dst = pl.multiple_of(used + b * MOE_SB, MOE_SB)
            return pltpu.make_async_copy(acc.at[pl.ds(0, MOE_SB), :],
                                         ys_ref.at[pl.ds(dst, MOE_SB), :], sem_out)

        def slack_start(b, _):
            slack_copy(b).start()
            return 0

        def slack_wait(b, _):
            slack_copy(b).wait()
            return 0

        lax.fori_loop(0, n_slack, slack_start, 0)
        lax.fori_loop(0, n_slack, slack_wait, 0)

    @pl.when(h == 0)
    def _():
        loop(lambda b: in_copy(b).start())
        loop(lambda b: in_copy(b).wait())

        def cast(b):
            off = pl.multiple_of(b * MOE_SB, MOE_SB)
            xbuf[pl.ds(off, MOE_SB), :] = acc[pl.ds(off, MOE_SB), :].astype(BF16)
        loop(cast)

    def compute(b, wg, wu, wd):
        off = pl.multiple_of(b * MOE_SB, MOE_SB)
        xb = xbuf[pl.ds(off, MOE_SB), :]
        gate = _dot(xb, wg)
        up = _dot(xb, wu)
        hidden = (gate * jax.nn.sigmoid(gate) * up).astype(BF16)
        contrib = _dot(hidden, wd)

        @pl.when(h == 0)
        def _():
            acc[pl.ds(off, MOE_SB), :] = contrib

        @pl.when(h > 0)
        def _():
            acc[pl.ds(off, MOE_SB), :] = acc[pl.ds(off, MOE_SB), :] + contrib

    @pl.when(n > 0)
    def _():
        wg = wg_ref[...].astype(BF16)
        wu = wu_ref[...].astype(BF16)
        wd = wd_ref[...].astype(BF16)
        loop(lambda b: compute(b, wg, wu, wd))

    @pl.when(h == n_h - 1)
    def _():
        loop(lambda b: out_copy(b).start())
        loop(lambda b: out_copy(b).wait())


def _moe(item_e, item_start, item_n, used, xs, w_gate, w_up, w_down):
    n_rows = xs.shape[0]
    n_items = item_e.shape[0]
    th = MOE_TH
    return pl.pallas_call(
        _moe_body,
        grid_spec=pltpu.PrefetchScalarGridSpec(
            num_scalar_prefetch=4,
            grid=(n_items, D_EXPERT // th),
            in_specs=[
                pl.BlockSpec(memory_space=pl.ANY),
                pl.BlockSpec((None, D_MODEL, th), lambda s, h, ie, st, nn, us: (ie[s], 0, h)),
                pl.BlockSpec((None, D_MODEL, th), lambda s, h, ie, st, nn, us: (ie[s], 0, h)),
                pl.BlockSpec((None, th, D_MODEL), lambda s, h, ie, st, nn, us: (ie[s], h, 0)),
            ],
            out_specs=pl.BlockSpec(memory_space=pl.ANY),
            scratch_shapes=[
                pltpu.VMEM((MOE_R, D_MODEL), BF16),
                pltpu.VMEM((MOE_R, D_MODEL), F32),
                pltpu.SemaphoreType.DMA(()),
                pltpu.SemaphoreType.DMA(()),
            ],
        ),
        out_shape=jax.ShapeDtypeStruct((n_rows, D_MODEL), F32),
        compiler_params=pltpu.CompilerParams(
            dimension_semantics=("arbitrary", "arbitrary"), vmem_limit_bytes=VMEM_LIMIT,
            has_side_effects=True),
    )(item_e, item_start, item_n, used, xs, w_gate, w_up, w_down)


def _combine_body(pos_ref, x1_ref, gw_ref, gf_ref, ys_ref, o_ref, ybuf, sem):
    i = pl.program_id(0)
    tm = x1_ref.shape[0]

    def row_copy(r, k):
        p = pos_ref[2 * (i * tm + r) + k]
        return pltpu.make_async_copy(ys_ref.at[pl.ds(p, 1), :], ybuf.at[k, pl.ds(r, 1), :], sem)

    def issue(r, _):
        row_copy(r, 0).start()
        row_copy(r, 1).start()
        return 0

    def drain(r, _):
        row_copy(r, 0).wait()
        row_copy(r, 1).wait()
        return 0

    lax.fori_loop(0, tm, issue, 0)
    lax.fori_loop(0, tm, drain, 0)

    gw = gw_ref[...]
    x = x1_ref[...] + gw[:, 0:1] * ybuf[0] + gw[:, 1:2] * ybuf[1]
    ms = jnp.mean(x * x, axis=-1, keepdims=True)
    o_ref[...] = x * lax.rsqrt(ms + EPS) * gf_ref[...]


def _combine(pos_flat, x1, gate_w, g_final, ys):
    T = x1.shape[0]
    tm = COMB_TM
    return pl.pallas_call(
        _combine_body,
        grid_spec=pltpu.PrefetchScalarGridSpec(
            num_scalar_prefetch=1,
            grid=(T // tm,),
            in_specs=[
                pl.BlockSpec((tm, D_MODEL), lambda i, pos: (i, 0)),
                pl.BlockSpec((tm, LANES), lambda i, pos: (i, 0)),
                pl.BlockSpec((1, D_MODEL), lambda i, pos: (0, 0)),
                pl.BlockSpec(memory_space=pl.ANY),
            ],
            out_specs=pl.BlockSpec((tm, D_MODEL), lambda i, pos: (i, 0)),
            scratch_shapes=[
                pltpu.VMEM((2, tm, D_MODEL), F32),
                pltpu.SemaphoreType.DMA(()),
            ],
        ),
        out_shape=jax.ShapeDtypeStruct((T, D_MODEL), F32),
        compiler_params=pltpu.CompilerParams(
            dimension_semantics=("arbitrary",), vmem_limit_bytes=VMEM_LIMIT),
    )(pos_flat, x1, gate_w, g_final.reshape(1, D_MODEL), ys)


def _moe_schedule(counts, n_items):
    padded = ((counts + MOE_SB - 1) // MOE_SB) * MOE_SB
    pends = jnp.cumsum(padded)
    pstarts = pends - padded
    chunks = (counts + MOE_R - 1) // MOE_R
    cends = jnp.cumsum(chunks)
    slot = jnp.arange(n_items, dtype=I32)
    total = cends[-1]
    live = slot < total
    owner = jnp.sum((slot[:, None] >= cends[None, :]).astype(I32), axis=1)
    last_owner = jnp.sum((total - 1 >= cends).astype(I32))
    owner = jnp.where(live, owner, last_owner)
    owner = jnp.minimum(owner, N_EXPERTS - 1)
    cidx = slot - (cends - chunks)[owner]
    item_start = jnp.where(live, pstarts[owner] + cidx * MOE_R, 0)
    item_n = jnp.where(live, jnp.clip(counts[owner] - cidx * MOE_R, 0, MOE_R), 0)
    return pstarts.astype(I32), pends.astype(I32), owner.astype(I32), item_start.astype(I32), \
        item_n.astype(I32)


def kernel(x, g_mix, w_in, ln_v_g, ln_v_b, w_spatial, b_spatial, w_branch_a, w_branch_b, w_out,
           g_ffn, w_router_group, b_router_group, w_router_expert, b_router_expert, w_gate, w_up,
           w_down, g_final):
    B, S, D = x.shape
    T = B * S
    cur = x.reshape(T, D)
    depth = g_mix.shape[0]
    for l in range(depth):
        proj = _proj(cur, g_mix[l], w_in[l].astype(BF16))
        yb = _attn(proj)
        w_r = jnp.zeros((D, LANES), F32)
        w_r = w_r.at[:, :N_GROUPS].set(w_router_group[l])
        w_r = w_r.at[:, N_GROUPS:N_GROUPS + N_EXPERTS].set(w_router_expert[l])
        b_r = jnp.zeros((1, LANES), F32)
        b_r = b_r.at[0, :N_GROUPS].set(b_router_group[l])
        b_r = b_r.at[0, N_GROUPS:N_GROUPS + N_EXPERTS].set(b_router_expert[l])
        x1, h2, logits = _merge(
            proj, yb, cur, ln_v_g[l].reshape(1, A_WIDTH), ln_v_b[l].reshape(1, A_WIDTH),
            w_spatial[l], b_spatial[l].T, w_branch_a[l].astype(BF16), w_branch_b[l].astype(BF16),
            w_out[l].astype(BF16), g_ffn[l].reshape(1, D), w_r, b_r)
        oi, gate_w, cnt = _route(logits)

        counts = cnt[0, :N_EXPERTS].astype(I32)
        n_rows = 2 * T + N_EXPERTS * MOE_SB
        n_items = N_EXPERTS + (2 * T) // MOE_R
        pstarts, pends, item_e, item_start, item_n = _moe_schedule(counts, n_items)
        pos = (pstarts[oi[:, 0:2]] + oi[:, 2:4]).reshape(2 * T)

        xs = _dispatch(pos, pends, h2, n_rows)
        ys = _moe(item_e, item_start, item_n, pends[N_EXPERTS - 1:], xs, w_gate[l], w_up[l], w_down[l])
        if l + 1 < depth:
            raise NotImplementedError("only the final layer fuses the closing rms_norm")
        cur = _combine(pos, x1, gate_w, g_final, ys)
    return cur.reshape(B, S, D)
```

```python
import functools

import jax
import jax.numpy as jnp
from jax import lax
from jax.experimental import pallas as pl
from jax.experimental.pallas import tpu as pltpu

F32 = jnp.float32
BF16 = jnp.bfloat16
I32 = jnp.int32

D_MODEL = 2048
CHUNK = 128
A_GROUPS = 8
A_WIDTH = 1024
SB_HEADS = 8
SB_HEAD_DIM = 128
SB_WIDTH = 1024
N_GROUPS = 4
EXPERTS_PER_GROUP = 8
N_EXPERTS = 32
D_EXPERT = 1024
EPS = 1e-6
PROJ_WIDTH = 2 * A_WIDTH + 3 * SB_WIDTH + 2 * D_MODEL
LANES = 128
EXP_ZERO_BELOW = -104.0

OFF_U, OFF_V = 0, A_WIDTH
OFF_Q = 2 * A_WIDTH
OFF_K = OFF_Q + SB_WIDTH
OFF_VV = OFF_K + SB_WIDTH
OFF_GA = OFF_VV + SB_WIDTH
OFF_GB = OFF_GA + D_MODEL

VMEM_LIMIT = 56 * 1024 * 1024

PROJ_TM, PROJ_TN = 1024, 512
ATT_TQ, ATT_TK = 256, 256
MERGE_TM = 256
ROUTE_TB = 512
DISP_TM = 256
MOE_SB = 128
MOE_CHUNKS = (512, 256, 128)
MOE_R = sum(MOE_CHUNKS)
MOE_TH = 256
COMB_TM = 256


def _dot(a, b):
    return jnp.dot(a, b, preferred_element_type=F32)


def _proj_body(x_ref, g_ref, w_ref, o_ref, h_scr, *, n_gelu, n_plain):
    j = pl.program_id(1)

    @pl.when(j == 0)
    def _():
        x = x_ref[...]
        ms = jnp.mean(x * x, axis=-1, keepdims=True)
        h_scr[...] = (x * lax.rsqrt(ms + EPS) * g_ref[...]).astype(BF16)

    acc = _dot(h_scr[...], w_ref[...])

    @pl.when(j < n_gelu)
    def _():
        o_ref[...] = jax.nn.gelu(acc).astype(BF16)

    @pl.when((j >= n_gelu) & (j < n_gelu + n_plain))
    def _():
        o_ref[...] = acc.astype(BF16)

    @pl.when(j >= n_gelu + n_plain)
    def _():
        o_ref[...] = jax.nn.sigmoid(acc).astype(BF16)


def _proj(x2, g_mix, w_in_bf):
    T = x2.shape[0]
    tm = min(PROJ_TM, T)
    tn = PROJ_TN
    body = functools.partial(_proj_body, n_gelu=OFF_Q // tn, n_plain=(OFF_GA - OFF_Q) // tn)
    return pl.pallas_call(
        body,
        grid=(T // tm, PROJ_WIDTH // tn),
        in_specs=[
            pl.BlockSpec((tm, D_MODEL), lambda i, j: (i, 0)),
            pl.BlockSpec((1, D_MODEL), lambda i, j: (0, 0)),
            pl.BlockSpec((D_MODEL, tn), lambda i, j: (0, j)),
        ],
        out_specs=pl.BlockSpec((tm, tn), lambda i, j: (i, j)),
        out_shape=jax.ShapeDtypeStruct((T, PROJ_WIDTH), BF16),
        scratch_shapes=[pltpu.VMEM((tm, D_MODEL), BF16)],
        compiler_params=pltpu.CompilerParams(
            dimension_semantics=("arbitrary", "arbitrary"), vmem_limit_bytes=VMEM_LIMIT),
    )(x2, g_mix.reshape(1, D_MODEL), w_in_bf)


def _attn_tile(q, k, v, upper, carry, qpos, kpos):
    z = lax.dot_general(q, k, (((1,), (1,)), ((), ())), preferred_element_type=F32)
    z = z * (SB_HEAD_DIM ** -0.5)
    mask = kpos < qpos
    sp = jnp.maximum(z, 0.0) + jnp.log(1.0 + jnp.exp(-jnp.abs(z)))
    log_rem = jnp.where(mask, -sp, 0.0)
    hi = log_rem.astype(BF16)
    lo = (log_rem - hi.astype(F32)).astype(BF16)
    suffix = _dot(hi, upper) + _dot(lo, upper)
    w = jnp.where(mask, jnp.exp((z - sp) + suffix + carry), 0.0)
    contrib = _dot(w.astype(BF16), v)
    carry = carry + jnp.sum(log_rem, axis=1, keepdims=True)
    return contrib, carry


def _attn_body(q_ref, k_ref, v_ref, o_ref):
    i = pl.program_id(1)
    tq, tk = ATT_TQ, ATT_TK
    n_keys = k_ref.shape[0]
    q = q_ref[...]
    row = lax.broadcasted_iota(I32, (tk, tk), 0)
    col = lax.broadcasted_iota(I32, (tk, tk), 1)
    upper = (row > col).astype(BF16)
    qpos = i * tq + lax.broadcasted_iota(I32, (tq, tk), 0)
    kcol = lax.broadcasted_iota(I32, (tq, tk), 1)

    def tile(j, carry):
        off = pl.multiple_of(jnp.maximum(j, 0) * tk, tk)
        kpos = kcol + jnp.where(j >= 0, j * tk, n_keys)
        return _attn_tile(q, k_ref[pl.ds(off, tk), :], v_ref[pl.ds(off, tk), :],
                          upper, carry, qpos, kpos)

    def cond(state):
        j, _, carry = state
        return (j >= 0) & (jnp.max(carry) > EXP_ZERO_BELOW)

    def step(state):
        j, acc, carry = state
        c1, carry = tile(j, carry)
        c2, carry = tile(j - 1, carry)
        return j - 2, acc + c1 + c2, carry

    j0 = ((i + 1) * tq - 1) // tk
    init = (j0, jnp.zeros((tq, SB_HEAD_DIM), F32), jnp.zeros((tq, 1), F32))
    _, acc, _ = lax.while_loop(cond, step, init)
    o_ref[...] = acc.astype(BF16)


def _attn(proj):
    T = proj.shape[0]
    qb, kb, vb = OFF_Q // SB_HEAD_DIM, OFF_K // SB_HEAD_DIM, OFF_VV // SB_HEAD_DIM
    return pl.pallas_call(
        _attn_body,
        grid=(SB_HEADS, T // ATT_TQ),
        in_specs=[
            pl.BlockSpec((ATT_TQ, SB_HEAD_DIM), lambda h, i: (i, qb + h)),
            pl.BlockSpec((T, SB_HEAD_DIM), lambda h, i: (0, kb + h)),
            pl.BlockSpec((T, SB_HEAD_DIM), lambda h, i: (0, vb + h)),
        ],
        out_specs=pl.BlockSpec((ATT_TQ, SB_HEAD_DIM), lambda h, i: (i, h)),
        out_shape=jax.ShapeDtypeStruct((T, SB_WIDTH), BF16),
        compiler_params=pltpu.CompilerParams(
            dimension_semantics=("arbitrary", "arbitrary"), vmem_limit_bytes=VMEM_LIMIT),
    )(proj, proj, proj)


def _merge_body(u_ref, v_ref, ga0_ref, ga1_ref, gb0_ref, gb1_ref, yb_ref, x_ref, lng_ref, lnb_ref,
                ws_ref, bst_ref, wa_ref, wb_ref, wo_ref, gffn_ref, wr_ref, br_ref,
                x1_ref, h2_ref, lg_ref, ya_scr):
    tm = u_ref.shape[0]
    vf = v_ref[...].astype(F32)
    mu = jnp.mean(vf, axis=-1, keepdims=True)
    xc = vf - mu
    var = jnp.mean(xc * xc, axis=-1, keepdims=True)
    vln = (xc * lax.rsqrt(var + EPS) * lng_ref[...] + lnb_ref[...]).astype(BF16)

    row = lax.broadcasted_iota(I32, (CHUNK, CHUNK), 0)
    col = lax.broadcasted_iota(I32, (CHUNK, CHUNK), 1)
    tril = col <= row
    for g in range(A_GROUPS):
        wg = jnp.where(tril, ws_ref[g], 0.0).astype(BF16)
        bias = bst_ref[:, g:g + 1]
        cs = slice(g * LANES, (g + 1) * LANES)
        for c in range(tm // CHUNK):
            rs = slice(c * CHUNK, (c + 1) * CHUNK)
            mixed = _dot(wg, vln[rs, cs]) + bias
            ya_scr[rs, cs] = (u_ref[rs, cs].astype(F32) * mixed).astype(BF16)

    a = _dot(ya_scr[...], wa_ref[...])
    b = _dot(yb_ref[...], wb_ref[...])
    ga = jnp.concatenate([ga0_ref[...], ga1_ref[...]], axis=1).astype(F32)
    gb = jnp.concatenate([gb0_ref[...], gb1_ref[...]], axis=1).astype(F32)
    merged = (ga * a + gb * b).astype(BF16)
    x1 = x_ref[...] + _dot(merged, wo_ref[...])
    x1_ref[...] = x1
    ms = jnp.mean(x1 * x1, axis=-1, keepdims=True)
    h2 = x1 * lax.rsqrt(ms + EPS) * gffn_ref[...]
    h2_ref[...] = h2
    hi = h2.astype(BF16)
    lo = (h2 - hi.astype(F32)).astype(BF16)
    wr = wr_ref[...]
    whi = wr.astype(BF16)
    wlo = (wr - whi.astype(F32)).astype(BF16)
    lg_ref[...] = _dot(hi, whi) + _dot(hi, wlo) + _dot(lo, whi) + br_ref[...]


def _merge(proj, yb, x2, ln_g, ln_b, w_s, b_s_t, wa, wb, wo, g_ffn, w_r, b_r):
    T = x2.shape[0]
    tm = MERGE_TM
    const2 = lambda i: (0, 0)
    one = pl.Buffered(1)

    def colblk(width, off):
        return pl.BlockSpec((tm, width), lambda i: (i, off // width))

    return pl.pallas_call(
        _merge_body,
        grid=(T // tm,),
        in_specs=[
            colblk(A_WIDTH, OFF_U), colblk(A_WIDTH, OFF_V),
            colblk(A_WIDTH, OFF_GA), colblk(A_WIDTH, OFF_GA + A_WIDTH),
            colblk(A_WIDTH, OFF_GB), colblk(A_WIDTH, OFF_GB + A_WIDTH),
            pl.BlockSpec((tm, SB_WIDTH), lambda i: (i, 0)),
            pl.BlockSpec((tm, D_MODEL), lambda i: (i, 0)),
            pl.BlockSpec((1, A_WIDTH), const2), pl.BlockSpec((1, A_WIDTH), const2),
            pl.BlockSpec((A_GROUPS, CHUNK, CHUNK), lambda i: (0, 0, 0)),
            pl.BlockSpec((CHUNK, A_GROUPS), const2),
            pl.BlockSpec((A_WIDTH, D_MODEL), const2, pipeline_mode=one),
            pl.BlockSpec((SB_WIDTH, D_MODEL), const2, pipeline_mode=one),
            pl.BlockSpec((D_MODEL, D_MODEL), const2, pipeline_mode=one),
            pl.BlockSpec((1, D_MODEL), const2),
            pl.BlockSpec((D_MODEL, LANES), const2, pipeline_mode=one),
            pl.BlockSpec((1, LANES), const2),
        ],
        out_specs=[
            pl.BlockSpec((tm, D_MODEL), lambda i: (i, 0)),
            pl.BlockSpec((tm, D_MODEL), lambda i: (i, 0)),
            pl.BlockSpec((tm, LANES), lambda i: (i, 0)),
        ],
        out_shape=[
            jax.ShapeDtypeStruct((T, D_MODEL), F32),
            jax.ShapeDtypeStruct((T, D_MODEL), F32),
            jax.ShapeDtypeStruct((T, LANES), F32),
        ],
        scratch_shapes=[pltpu.VMEM((tm, A_WIDTH), BF16)],
        compiler_params=pltpu.CompilerParams(
            dimension_semantics=("arbitrary",), vmem_limit_bytes=VMEM_LIMIT),
    )(proj, proj, proj, proj, proj, proj, yb, x2, ln_g, ln_b, w_s, b_s_t, wa, wb, wo, g_ffn, w_r, b_r)


def _route_body(lg_ref, oi_ref, ow_ref, cnt_ref, carry_scr):
    i = pl.program_id(0)
    tb = lg_ref.shape[0]

    @pl.when(i == 0)
    def _():
        carry_scr[...] = jnp.zeros_like(carry_scr)

    lg = lg_ref[...]
    lane = lax.broadcasted_iota(I32, (tb, LANES), 1)
    neg = jnp.float32(-jnp.inf)
    big = jnp.int32(LANES)

    def first_max(mask):
        m = jnp.max(jnp.where(mask, lg, neg), axis=1, keepdims=True)
        idx = jnp.min(jnp.where(mask & (lg == m), lane, big), axis=1, keepdims=True)
        return m, idx

    gmask = lane < N_GROUPS
    gm, grp = first_max(gmask)
    pg_sel = 1.0 / jnp.sum(jnp.where(gmask, jnp.exp(lg - gm), 0.0), axis=1, keepdims=True)

    eid = lane - N_GROUPS
    emask = (eid >= 0) & (eid < N_EXPERTS) & ((eid // EXPERTS_PER_GROUP) == grp)
    m1, i1 = first_max(emask)
    emask2 = emask & (lane != i1)
    m2, i2 = first_max(emask2)
    t = jnp.exp(m2 - m1)
    p1 = 1.0 / (1.0 + t)
    p2 = t / (1.0 + t)
    e1 = i1 - N_GROUPS
    e2 = i2 - N_GROUPS

    sel1 = lane == e1
    sel2 = lane == e2
    onehot = (sel1 | sel2).astype(BF16)
    r = lax.broadcasted_iota(I32, (tb, tb), 0)
    c = lax.broadcasted_iota(I32, (tb, tb), 1)
    earlier = (c < r).astype(BF16)
    prefix = _dot(earlier, onehot) + carry_scr[...]
    rank1 = jnp.sum(jnp.where(sel1, prefix, 0.0), axis=1, keepdims=True).astype(I32)
    rank2 = jnp.sum(jnp.where(sel2, prefix, 0.0), axis=1, keepdims=True).astype(I32)
    carry_scr[...] = carry_scr[...] + jnp.sum(onehot.astype(F32), axis=0, keepdims=True)

    zero = jnp.zeros((tb, LANES), I32)
    oi_ref[...] = jnp.where(lane == 0, e1, jnp.where(lane == 1, e2,
                            jnp.where(lane == 2, rank1, jnp.where(lane == 3, rank2, zero))))
    ow_ref[...] = jnp.where(lane == 0, pg_sel * p1, jnp.where(lane == 1, pg_sel * p2, 0.0))
    cnt_ref[...] = carry_scr[...]


def _route(logits):
    T = logits.shape[0]
    tb = min(ROUTE_TB, T)
    return pl.pallas_call(
        _route_body,
        grid=(T // tb,),
        in_specs=[pl.BlockSpec((tb, LANES), lambda i: (i, 0))],
        out_specs=[
            pl.BlockSpec((tb, LANES), lambda i: (i, 0)),
            pl.BlockSpec((tb, LANES), lambda i: (i, 0)),
            pl.BlockSpec((1, LANES), lambda i: (0, 0)),
        ],
        out_shape=[
            jax.ShapeDtypeStruct((T, LANES), I32),
            jax.ShapeDtypeStruct((T, LANES), F32),
            jax.ShapeDtypeStruct((1, LANES), F32),
        ],
        scratch_shapes=[pltpu.VMEM((1, LANES), F32)],
        compiler_params=pltpu.CompilerParams(dimension_semantics=("arbitrary",)),
    )(logits)


def _dispatch_body(pos_ref, pend_ref, h2_ref, xs_ref, zbuf, sem, zsem):
    i = pl.program_id(0)
    tm = h2_ref.shape[0]

    def tail_copy(e):
        start = pl.multiple_of(pend_ref[e] - MOE_SB, MOE_SB)
        return pltpu.make_async_copy(zbuf, xs_ref.at[pl.ds(start, MOE_SB), :], zsem)

    @pl.when(i == 0)
    def _():
        zbuf[...] = jnp.zeros_like(zbuf)
        for e in range(N_EXPERTS):
            prev = pend_ref[e - 1] if e > 0 else 0

            @pl.when(pend_ref[e] > prev)
            def _():
                tail_copy(e).start()
        for e in range(N_EXPERTS):
            prev = pend_ref[e - 1] if e > 0 else 0

            @pl.when(pend_ref[e] > prev)
            def _():
                tail_copy(e).wait()

        used = pend_ref[N_EXPERTS - 1]
        n_slack = (xs_ref.shape[0] - used) // MOE_SB

        def slack_copy(b):
            start = pl.multiple_of(used + b * MOE_SB, MOE_SB)
            return pltpu.make_async_copy(zbuf, xs_ref.at[pl.ds(start, MOE_SB), :], zsem)

        def slack_start(b, _):
            slack_copy(b).start()
            return 0

        def slack_wait(b, _):
            slack_copy(b).wait()
            return 0

        lax.fori_loop(0, n_slack, slack_start, 0)
        lax.fori_loop(0, n_slack, slack_wait, 0)

    def row_copy(r, k):
        p = pos_ref[2 * (i * tm + r) + k]
        return pltpu.make_async_copy(h2_ref.at[pl.ds(r, 1), :], xs_ref.at[pl.ds(p, 1), :], sem)

    def issue(r, _):
        row_copy(r, 0).start()
        row_copy(r, 1).start()
        return 0

    def drain(r, _):
        row_copy(r, 0).wait()
        row_copy(r, 1).wait()
        return 0

    lax.fori_loop(0, tm, issue, 0)
    lax.fori_loop(0, tm, drain, 0)


def _dispatch(pos_flat, pends, h2, n_rows):
    T = h2.shape[0]
    tm = DISP_TM
    return pl.pallas_call(
        _dispatch_body,
        grid_spec=pltpu.PrefetchScalarGridSpec(
            num_scalar_prefetch=2,
            grid=(T // tm,),
            in_specs=[pl.BlockSpec((tm, D_MODEL), lambda i, pos, pe: (i, 0))],
            out_specs=pl.BlockSpec(memory_space=pl.ANY),
            scratch_shapes=[
                pltpu.VMEM((MOE_SB, D_MODEL), F32),
                pltpu.SemaphoreType.DMA(()),
                pltpu.SemaphoreType.DMA(()),
            ],
        ),
        out_shape=jax.ShapeDtypeStruct((n_rows, D_MODEL), F32),
        compiler_params=pltpu.CompilerParams(
            dimension_semantics=("arbitrary",), has_side_effects=True),
    )(pos_flat, pends, h2)


def _moe_body(ie_ref, is_ref, in_ref, used_ref, xs_ref, wg_ref, wu_ref, wd_ref, ys_ref,
              xbuf, acc, sem_in, sem_out):
    s = pl.program_id(0)
    h = pl.program_id(1)
    n_items = pl.num_programs(0)
    n_h = pl.num_programs(1)
    slot = s % 2

    def n_blocks(item):
        return (in_ref[item] + MOE_SB - 1) // MOE_SB

    def in_copy(item, b):
        sl = item % 2
        off = pl.multiple_of(b * MOE_SB, MOE_SB)
        src = pl.multiple_of(is_ref[item] + off, MOE_SB)
        return pltpu.make_async_copy(xs_ref.at[pl.ds(src, MOE_SB), :],
                                     xbuf.at[sl, pl.ds(off, MOE_SB), :], sem_in.at[sl])

    def out_copy(item, b):
        sl = item % 2
        off = pl.multiple_of(b * MOE_SB, MOE_SB)
        dst = pl.multiple_of(is_ref[item] + off, MOE_SB)
        return pltpu.make_async_copy(acc.at[sl, pl.ds(off, MOE_SB), :],
                                     ys_ref.at[pl.ds(dst, MOE_SB), :], sem_out.at[sl])

    def for_blocks(item, fn):
        def body(b, carry):
            fn(item, b)
            return carry
        lax.fori_loop(0, n_blocks(item), body, 0)

    def start_in(item, b):
        in_copy(item, b).start()

    def wait_in(item, b):
        in_copy(item, b).wait()

    def start_out(item, b):
        out_copy(item, b).start()

    def wait_out(item, b):
        out_copy(item, b).wait()

    @pl.when((s == 0) & (h == 0))
    def _():
        used = used_ref[0]
        n_slack = (ys_ref.shape[0] - used) // MOE_SB
        acc[1, pl.ds(0, MOE_SB), :] = jnp.zeros((MOE_SB, D_MODEL), F32)

        def slack_copy(b):
            dst = pl.multiple_of(used + b * MOE_SB, MOE_SB)
            return pltpu.make_async_copy(acc.at[1, pl.ds(0, MOE_SB), :],
                                         ys_ref.at[pl.ds(dst, MOE_SB), :], sem_out.at[1])

        def slack_start(b, _):
            slack_copy(b).start()
            return 0

        def slack_wait(b, _):
            slack_copy(b).wait()
            return 0

        lax.fori_loop(0, n_slack, slack_start, 0)
        lax.fori_loop(0, n_slack, slack_wait, 0)
        for_blocks(s, start_in)

    @pl.when(h == 0)
    def _():
        for_blocks(s, wait_in)

    @pl.when((h == 1) & (s + 1 < n_items))
    def _():
        for_blocks(s + 1, start_in)

    def chunk(off, size, first):
        off = pl.multiple_of(off, MOE_SB)
        xb = xbuf[slot, pl.ds(off, size), :]
        gate = _dot(xb, wg_ref[...])
        up = _dot(xb, wu_ref[...])
        hidden = gate * jax.nn.sigmoid(gate) * up
        contrib = _dot(hidden, wd_ref[...])
        if first:
            acc[slot, pl.ds(off, size), :] = contrib
        else:
            acc[slot, pl.ds(off, size), :] += contrib

    n_pad = n_blocks(s) * MOE_SB
    off = 0
    for size in MOE_CHUNKS:
        take = (n_pad & size) != 0
        for first in (True, False):
            @pl.when(take & ((h == 0) if first else (h > 0)))
            def _(off=off, size=size, first=first):
                chunk(off, size, first)

        off = off + jnp.where(take, size, 0)

    @pl.when(h == n_h - 1)
    def _():
        @pl.when(s > 0)
        def _():
            for_blocks(s - 1, wait_out)

        for_blocks(s, start_out)

        @pl.when(s == n_items - 1)
        def _():
            for_blocks(s, wait_out)


def _moe(item_e, item_start, item_n, used, xs, w_gate, w_up, w_down):
    n_rows = xs.shape[0]
    n_items = item_e.shape[0]
    th = MOE_TH
    return pl.pallas_call(
        _moe_body,
        grid_spec=pltpu.PrefetchScalarGridSpec(
            num_scalar_prefetch=4,
            grid=(n_items, D_EXPERT // th),
            in_specs=[
                pl.BlockSpec(memory_space=pl.ANY),
                pl.BlockSpec((None, D_MODEL, th), lambda s, h, ie, st, nn, us: (ie[s], 0, h)),
                pl.BlockSpec((None, D_MODEL, th), lambda s, h, ie, st, nn, us: (ie[s], 0, h)),
                pl.BlockSpec((None, th, D_MODEL), lambda s, h, ie, st, nn, us: (ie[s], h, 0)),
            ],
            out_specs=pl.BlockSpec(memory_space=pl.ANY),
            scratch_shapes=[
                pltpu.VMEM((2, MOE_R, D_MODEL), F32),
                pltpu.VMEM((2, MOE_R, D_MODEL), F32),
                pltpu.SemaphoreType.DMA((2,)),
                pltpu.SemaphoreType.DMA((2,)),
            ],
        ),
        out_shape=jax.ShapeDtypeStruct((n_rows, D_MODEL), F32),
        compiler_params=pltpu.CompilerParams(
            dimension_semantics=("arbitrary", "arbitrary"), vmem_limit_bytes=VMEM_LIMIT,
            has_side_effects=True),
    )(item_e, item_start, item_n, used, xs, w_gate, w_up, w_down)


def _combine_body(pos_ref, x1_ref, gw_ref, gf_ref, ys_ref, o_ref, ybuf, sem):
    i = pl.program_id(0)
    tm = x1_ref.shape[0]

    def row_copy(r, k):
        p = pos_ref[2 * (i * tm + r) + k]
        return pltpu.make_async_copy(ys_ref.at[pl.ds(p, 1), :], ybuf.at[k, pl.ds(r, 1), :], sem)

    def issue(r, _):
        row_copy(r, 0).start()
        row_copy(r, 1).start()
        return 0

    def drain(r, _):
        row_copy(r, 0).wait()
        row_copy(r, 1).wait()
        return 0

    lax.fori_loop(0, tm, issue, 0)
    lax.fori_loop(0, tm, drain, 0)

    gw = gw_ref[...]
    x = x1_ref[...] + gw[:, 0:1] * ybuf[0] + gw[:, 1:2] * ybuf[1]
    ms = jnp.mean(x * x, axis=-1, keepdims=True)
    o_ref[...] = x * lax.rsqrt(ms + EPS) * gf_ref[...]


def _combine(pos_flat, x1, gate_w, g_final, ys):
    T = x1.shape[0]
    tm = COMB_TM
    return pl.pallas_call(
        _combine_body,
        grid_spec=pltpu.PrefetchScalarGridSpec(
            num_scalar_prefetch=1,
            grid=(T // tm,),
            in_specs=[
                pl.BlockSpec((tm, D_MODEL), lambda i, pos: (i, 0)),
                pl.BlockSpec((tm, LANES), lambda i, pos: (i, 0)),
                pl.BlockSpec((1, D_MODEL), lambda i, pos: (0, 0)),
                pl.BlockSpec(memory_space=pl.ANY),
            ],
            out_specs=pl.BlockSpec((tm, D_MODEL), lambda i, pos: (i, 0)),
            scratch_shapes=[
                pltpu.VMEM((2, tm, D_MODEL), F32),
                pltpu.SemaphoreType.DMA(()),
            ],
        ),
        out_shape=jax.ShapeDtypeStruct((T, D_MODEL), F32),
        compiler_params=pltpu.CompilerParams(
            dimension_semantics=("arbitrary",), vmem_limit_bytes=VMEM_LIMIT),
    )(pos_flat, x1, gate_w, g_final.reshape(1, D_MODEL), ys)


def _moe_schedule(counts, n_items):
    padded = ((counts + MOE_SB - 1) // MOE_SB) * MOE_SB
    pends = jnp.cumsum(padded)
    pstarts = pends - padded
    chunks = (counts + MOE_R - 1) // MOE_R
    cends = jnp.cumsum(chunks)
    slot = jnp.arange(n_items, dtype=I32)
    total = cends[-1]
    live = slot < total
    owner = jnp.sum((slot[:, None] >= cends[None, :]).astype(I32), axis=1)
    last_owner = jnp.sum((total - 1 >= cends).astype(I32))
    owner = jnp.where(live, owner, last_owner)
    owner = jnp.minimum(owner, N_EXPERTS - 1)
    cidx = slot - (cends - chunks)[owner]
    item_start = jnp.where(live, pstarts[owner] + cidx * MOE_R, 0)
    item_n = jnp.where(live, jnp.clip(counts[owner] - cidx * MOE_R, 0, MOE_R), 0)
    return pstarts.astype(I32), pends.astype(I32), owner.astype(I32), item_start.astype(I32), \
        item_n.astype(I32)


def kernel(x, g_mix, w_in, ln_v_g, ln_v_b, w_spatial, b_spatial, w_branch_a, w_branch_b, w_out,
           g_ffn, w_router_group, b_router_group, w_router_expert, b_router_expert, w_gate, w_up,
           w_down, g_final):
    B, S, D = x.shape
    T = B * S
    cur = x.reshape(T, D)
    depth = g_mix.shape[0]
    for l in range(depth):
        proj = _proj(cur, g_mix[l], w_in[l].astype(BF16))
        yb = _attn(proj)
        w_r = jnp.zeros((D, LANES), F32)
        w_r = w_r.at[:, :N_GROUPS].set(w_router_group[l])
        w_r = w_r.at[:, N_GROUPS:N_GROUPS + N_EXPERTS].set(w_router_expert[l])
        b_r = jnp.zeros((1, LANES), F32)
        b_r = b_r.at[0, :N_GROUPS].set(b_router_group[l])
        b_r = b_r.at[0, N_GROUPS:N_GROUPS + N_EXPERTS].set(b_router_expert[l])
        x1, h2, logits = _merge(
            proj, yb, cur, ln_v_g[l].reshape(1, A_WIDTH), ln_v_b[l].reshape(1, A_WIDTH),
            w_spatial[l], b_spatial[l].T, w_branch_a[l].astype(BF16), w_branch_b[l].astype(BF16),
            w_out[l].astype(BF16), g_ffn[l].reshape(1, D), w_r, b_r)
        oi, gate_w, cnt = _route(logits)

        counts = cnt[0, :N_EXPERTS].astype(I32)
        n_rows = 2 * T + N_EXPERTS * MOE_SB
        n_items = N_EXPERTS + -(-(2 * T) // MOE_R)
        pstarts, pends, item_e, item_start, item_n = _moe_schedule(counts, n_items)
        pos = (pstarts[oi[:, 0:2]] + oi[:, 2:4]).reshape(2 * T)

        xs = _dispatch(pos, pends, h2, n_rows)
        ys = _moe(item_e, item_start, item_n, pends[N_EXPERTS - 1:], xs, w_gate[l], w_up[l], w_down[l])
        if l + 1 < depth:
            raise NotImplementedError("only the final layer fuses the closing rms_norm")
        cur = _combine(pos, x1, gate_w, g_final, ys)
    return cur.reshape(B, S, D)
```

```python
import functools

import jax
import jax.numpy as jnp
from jax import lax
from jax.experimental import pallas as pl
from jax.experimental.pallas import tpu as pltpu

F32 = jnp.float32
BF16 = jnp.bfloat16
I32 = jnp.int32

D_MODEL = 2048
CHUNK = 128
A_GROUPS = 8
A_WIDTH = 1024
SB_HEADS = 8
SB_HEAD_DIM = 128
SB_WIDTH = 1024
N_GROUPS = 4
EXPERTS_PER_GROUP = 8
N_EXPERTS = 32
D_EXPERT = 1024
EPS = 1e-6
PROJ_WIDTH = 2 * A_WIDTH + 3 * SB_WIDTH + 2 * D_MODEL
LANES = 128
EXP_ZERO_BELOW = -104.0

OFF_U, OFF_V = 0, A_WIDTH
OFF_Q = 2 * A_WIDTH
OFF_K = OFF_Q + SB_WIDTH
OFF_VV = OFF_K + SB_WIDTH
OFF_GA = OFF_VV + SB_WIDTH
OFF_GB = OFF_GA + D_MODEL

VMEM_LIMIT = 56 * 1024 * 1024

PROJ_TM, PROJ_TN = 1024, 512
ATT_TQ, ATT_TK = 256, 256
MERGE_TM = 256
ROUTE_TB = 512
DISP_TM = 256
MOE_SB = 128
MOE_CHUNKS = (512, 256, 128)
MOE_R = 768
MOE_TH = 256
MOE_WBUF = 3
COMB_TM = 256


def _dot(a, b):
    return jnp.dot(a, b, preferred_element_type=F32)


def _proj_body(x_ref, g_ref, w_ref, o_ref, h_scr, *, n_gelu, n_plain):
    j = pl.program_id(1)

    @pl.when(j == 0)
    def _():
        x = x_ref[...]
        ms = jnp.mean(x * x, axis=-1, keepdims=True)
        h_scr[...] = (x * lax.rsqrt(ms + EPS) * g_ref[...]).astype(BF16)

    acc = _dot(h_scr[...], w_ref[...])

    @pl.when(j < n_gelu)
    def _():
        o_ref[...] = jax.nn.gelu(acc).astype(BF16)

    @pl.when((j >= n_gelu) & (j < n_gelu + n_plain))
    def _():
        o_ref[...] = acc.astype(BF16)

    @pl.when(j >= n_gelu + n_plain)
    def _():
        o_ref[...] = jax.nn.sigmoid(acc).astype(BF16)


def _proj(x2, g_mix, w_in_bf):
    T = x2.shape[0]
    tm = min(PROJ_TM, T)
    tn = PROJ_TN
    body = functools.partial(_proj_body, n_gelu=OFF_Q // tn, n_plain=(OFF_GA - OFF_Q) // tn)
    return pl.pallas_call(
        body,
        grid=(T // tm, PROJ_WIDTH // tn),
        in_specs=[
            pl.BlockSpec((tm, D_MODEL), lambda i, j: (i, 0)),
            pl.BlockSpec((1, D_MODEL), lambda i, j: (0, 0)),
            pl.BlockSpec((D_MODEL, tn), lambda i, j: (0, j)),
        ],
        out_specs=pl.BlockSpec((tm, tn), lambda i, j: (i, j)),
        out_shape=jax.ShapeDtypeStruct((T, PROJ_WIDTH), BF16),
        scratch_shapes=[pltpu.VMEM((tm, D_MODEL), BF16)],
        compiler_params=pltpu.CompilerParams(
            dimension_semantics=("arbitrary", "arbitrary"), vmem_limit_bytes=VMEM_LIMIT),
    )(x2, g_mix.reshape(1, D_MODEL), w_in_bf)


def _attn_tile(q, k, v, upper, carry, qpos, kpos):
    z = lax.dot_general(q, k, (((1,), (1,)), ((), ())), preferred_element_type=F32)
    z = z * (SB_HEAD_DIM ** -0.5)
    mask = kpos < qpos
    sp = jnp.maximum(z, 0.0) + jnp.log(1.0 + jnp.exp(-jnp.abs(z)))
    log_rem = jnp.where(mask, -sp, 0.0)
    hi = log_rem.astype(BF16)
    lo = (log_rem - hi.astype(F32)).astype(BF16)
    suffix = _dot(hi, upper) + _dot(lo, upper)
    w = jnp.where(mask, jnp.exp((z - sp) + suffix + carry), 0.0)
    contrib = _dot(w.astype(BF16), v)
    carry = carry + jnp.sum(log_rem, axis=1, keepdims=True)
    return contrib, carry


def _attn_body(q_ref, k_ref, v_ref, o_ref):
    i = pl.program_id(1)
    tq, tk = ATT_TQ, ATT_TK
    n_keys = k_ref.shape[0]
    q = q_ref[...]
    row = lax.broadcasted_iota(I32, (tk, tk), 0)
    col = lax.broadcasted_iota(I32, (tk, tk), 1)
    upper = (row > col).astype(BF16)
    qpos = i * tq + lax.broadcasted_iota(I32, (tq, tk), 0)
    kcol = lax.broadcasted_iota(I32, (tq, tk), 1)

    def tile(j, carry):
        off = pl.multiple_of(jnp.maximum(j, 0) * tk, tk)
        kpos = kcol + jnp.where(j >= 0, j * tk, n_keys)
        return _attn_tile(q, k_ref[pl.ds(off, tk), :], v_ref[pl.ds(off, tk), :],
                          upper, carry, qpos, kpos)

    def cond(state):
        j, _, carry = state
        return (j >= 0) & (jnp.max(carry) > EXP_ZERO_BELOW)

    def step(state):
        j, acc, carry = state
        c1, carry = tile(j, carry)
        c2, carry = tile(j - 1, carry)
        return j - 2, acc + c1 + c2, carry

    j0 = ((i + 1) * tq - 1) // tk
    init = (j0, jnp.zeros((tq, SB_HEAD_DIM), F32), jnp.zeros((tq, 1), F32))
    _, acc, _ = lax.while_loop(cond, step, init)
    o_ref[...] = acc.astype(BF16)


def _attn(proj):
    T = proj.shape[0]
    qb, kb, vb = OFF_Q // SB_HEAD_DIM, OFF_K // SB_HEAD_DIM, OFF_VV // SB_HEAD_DIM
    return pl.pallas_call(
        _attn_body,
        grid=(SB_HEADS, T // ATT_TQ),
        in_specs=[
            pl.BlockSpec((ATT_TQ, SB_HEAD_DIM), lambda h, i: (i, qb + h)),
            pl.BlockSpec((T, SB_HEAD_DIM), lambda h, i: (0, kb + h)),
            pl.BlockSpec((T, SB_HEAD_DIM), lambda h, i: (0, vb + h)),
        ],
        out_specs=pl.BlockSpec((ATT_TQ, SB_HEAD_DIM), lambda h, i: (i, h)),
        out_shape=jax.ShapeDtypeStruct((T, SB_WIDTH), BF16),
        compiler_params=pltpu.CompilerParams(
            dimension_semantics=("arbitrary", "arbitrary"), vmem_limit_bytes=VMEM_LIMIT),
    )(proj, proj, proj)


def _merge_body(u_ref, v_ref, ga0_ref, ga1_ref, gb0_ref, gb1_ref, yb_ref, x_ref, lng_ref, lnb_ref,
                ws_ref, bst_ref, wa_ref, wb_ref, wo_ref, gffn_ref, wr_ref, br_ref,
                x1_ref, h2_ref, lg_ref, ya_scr):
    tm = u_ref.shape[0]
    vf = v_ref[...].astype(F32)
    mu = jnp.mean(vf, axis=-1, keepdims=True)
    xc = vf - mu
    var = jnp.mean(xc * xc, axis=-1, keepdims=True)
    vln = (xc * lax.rsqrt(var + EPS) * lng_ref[...] + lnb_ref[...]).astype(BF16)

    row = lax.broadcasted_iota(I32, (CHUNK, CHUNK), 0)
    col = lax.broadcasted_iota(I32, (CHUNK, CHUNK), 1)
    tril = col <= row
    for g in range(A_GROUPS):
        wg = jnp.where(tril, ws_ref[g], 0.0).astype(BF16)
        bias = bst_ref[:, g:g + 1]
        cs = slice(g * LANES, (g + 1) * LANES)
        for c in range(tm // CHUNK):
            rs = slice(c * CHUNK, (c + 1) * CHUNK)
            mixed = _dot(wg, vln[rs, cs]) + bias
            ya_scr[rs, cs] = (u_ref[rs, cs].astype(F32) * mixed).astype(BF16)

    a = _dot(ya_scr[...], wa_ref[...])
    b = _dot(yb_ref[...], wb_ref[...])
    ga = jnp.concatenate([ga0_ref[...], ga1_ref[...]], axis=1).astype(F32)
    gb = jnp.concatenate([gb0_ref[...], gb1_ref[...]], axis=1).astype(F32)
    merged = (ga * a + gb * b).astype(BF16)
    x1 = x_ref[...] + _dot(merged, wo_ref[...])
    x1_ref[...] = x1
    ms = jnp.mean(x1 * x1, axis=-1, keepdims=True)
    h2 = x1 * lax.rsqrt(ms + EPS) * gffn_ref[...]
    h2_ref[...] = h2
    hi = h2.astype(BF16)
    lo = (h2 - hi.astype(F32)).astype(BF16)
    wr = wr_ref[...]
    whi = wr.astype(BF16)
    wlo = (wr - whi.astype(F32)).astype(BF16)
    lg_ref[...] = _dot(hi, whi) + _dot(hi, wlo) + _dot(lo, whi) + br_ref[...]


def _merge(proj, yb, x2, ln_g, ln_b, w_s, b_s_t, wa, wb, wo, g_ffn, w_r, b_r):
    T = x2.shape[0]
    tm = MERGE_TM
    const2 = lambda i: (0, 0)
    one = pl.Buffered(1)

    def colblk(width, off):
        return pl.BlockSpec((tm, width), lambda i: (i, off // width))

    return pl.pallas_call(
        _merge_body,
        grid=(T // tm,),
        in_specs=[
            colblk(A_WIDTH, OFF_U), colblk(A_WIDTH, OFF_V),
            colblk(A_WIDTH, OFF_GA), colblk(A_WIDTH, OFF_GA + A_WIDTH),
            colblk(A_WIDTH, OFF_GB), colblk(A_WIDTH, OFF_GB + A_WIDTH),
            pl.BlockSpec((tm, SB_WIDTH), lambda i: (i, 0)),
            pl.BlockSpec((tm, D_MODEL), lambda i: (i, 0)),
            pl.BlockSpec((1, A_WIDTH), const2), pl.BlockSpec((1, A_WIDTH), const2),
            pl.BlockSpec((A_GROUPS, CHUNK, CHUNK), lambda i: (0, 0, 0)),
            pl.BlockSpec((CHUNK, A_GROUPS), const2),
            pl.BlockSpec((A_WIDTH, D_MODEL), const2, pipeline_mode=one),
            pl.BlockSpec((SB_WIDTH, D_MODEL), const2, pipeline_mode=one),
            pl.BlockSpec((D_MODEL, D_MODEL), const2, pipeline_mode=one),
            pl.BlockSpec((1, D_MODEL), const2),
            pl.BlockSpec((D_MODEL, LANES), const2, pipeline_mode=one),
            pl.BlockSpec((1, LANES), const2),
        ],
        out_specs=[
            pl.BlockSpec((tm, D_MODEL), lambda i: (i, 0)),
            pl.BlockSpec((tm, D_MODEL), lambda i: (i, 0)),
            pl.BlockSpec((tm, LANES), lambda i: (i, 0)),
        ],
        out_shape=[
            jax.ShapeDtypeStruct((T, D_MODEL), F32),
            jax.ShapeDtypeStruct((T, D_MODEL), F32),
            jax.ShapeDtypeStruct((T, LANES), F32),
        ],
        scratch_shapes=[pltpu.VMEM((tm, A_WIDTH), BF16)],
        compiler_params=pltpu.CompilerParams(
            dimension_semantics=("arbitrary",), vmem_limit_bytes=VMEM_LIMIT),
    )(proj, proj, proj, proj, proj, proj, yb, x2, ln_g, ln_b, w_s, b_s_t, wa, wb, wo, g_ffn, w_r, b_r)


def _route_body(lg_ref, oi_ref, ow_ref, cnt_ref, carry_scr):
    i = pl.program_id(0)
    tb = lg_ref.shape[0]

    @pl.when(i == 0)
    def _():
        carry_scr[...] = jnp.zeros_like(carry_scr)

    lg = lg_ref[...]
    lane = lax.broadcasted_iota(I32, (tb, LANES), 1)
    neg = jnp.float32(-jnp.inf)
    big = jnp.int32(LANES)

    def first_max(mask):
        m = jnp.max(jnp.where(mask, lg, neg), axis=1, keepdims=True)
        idx = jnp.min(jnp.where(mask & (lg == m), lane, big), axis=1, keepdims=True)
        return m, idx

    gmask = lane < N_GROUPS
    gm, grp = first_max(gmask)
    pg_sel = 1.0 / jnp.sum(jnp.where(gmask, jnp.exp(lg - gm), 0.0), axis=1, keepdims=True)

    eid = lane - N_GROUPS
    emask = (eid >= 0) & (eid < N_EXPERTS) & ((eid // EXPERTS_PER_GROUP) == grp)
    m1, i1 = first_max(emask)
    emask2 = emask & (lane != i1)
    m2, i2 = first_max(emask2)
    t = jnp.exp(m2 - m1)
    p1 = 1.0 / (1.0 + t)
    p2 = t / (1.0 + t)
    e1 = i1 - N_GROUPS
    e2 = i2 - N_GROUPS

    sel1 = lane == e1
    sel2 = lane == e2
    onehot = (sel1 | sel2).astype(BF16)
    r = lax.broadcasted_iota(I32, (tb, tb), 0)
    c = lax.broadcasted_iota(I32, (tb, tb), 1)
    earlier = (c < r).astype(BF16)
    prefix = _dot(earlier, onehot) + carry_scr[...]
    rank1 = jnp.sum(jnp.where(sel1, prefix, 0.0), axis=1, keepdims=True).astype(I32)
    rank2 = jnp.sum(jnp.where(sel2, prefix, 0.0), axis=1, keepdims=True).astype(I32)
    carry_scr[...] = carry_scr[...] + jnp.sum(onehot.astype(F32), axis=0, keepdims=True)

    zero = jnp.zeros((tb, LANES), I32)
    oi_ref[...] = jnp.where(lane == 0, e1, jnp.where(lane == 1, e2,
                            jnp.where(lane == 2, rank1, jnp.where(lane == 3, rank2, zero))))
    ow_ref[...] = jnp.where(lane == 0, pg_sel * p1, jnp.where(lane == 1, pg_sel * p2, 0.0))
    cnt_ref[...] = carry_scr[...]


def _route(logits):
    T = logits.shape[0]
    tb = min(ROUTE_TB, T)
    return pl.pallas_call(
        _route_body,
        grid=(T // tb,),
        in_specs=[pl.BlockSpec((tb, LANES), lambda i: (i, 0))],
        out_specs=[
            pl.BlockSpec((tb, LANES), lambda i: (i, 0)),
            pl.BlockSpec((tb, LANES), lambda i: (i, 0)),
            pl.BlockSpec((1, LANES), lambda i: (0, 0)),
        ],
        out_shape=[
            jax.ShapeDtypeStruct((T, LANES), I32),
            jax.ShapeDtypeStruct((T, LANES), F32),
            jax.ShapeDtypeStruct((1, LANES), F32),
        ],
        scratch_shapes=[pltpu.VMEM((1, LANES), F32)],
        compiler_params=pltpu.CompilerParams(dimension_semantics=("arbitrary",)),
    )(logits)


def _dispatch_body(pos_ref, pend_ref, h2_ref, xs_ref, zbuf, sem, zsem):
    i = pl.program_id(0)
    tm = h2_ref.shape[0]

    def tail_copy(e):
        start = pl.multiple_of(pend_ref[e] - MOE_SB, MOE_SB)
        return pltpu.make_async_copy(zbuf, xs_ref.at[pl.ds(start, MOE_SB), :], zsem)

    @pl.when(i == 0)
    def _():
        zbuf[...] = jnp.zeros_like(zbuf)
        for e in range(N_EXPERTS):
            prev = pend_ref[e - 1] if e > 0 else 0

            @pl.when(pend_ref[e] > prev)
            def _():
                tail_copy(e).start()
        for e in range(N_EXPERTS):
            prev = pend_ref[e - 1] if e > 0 else 0

            @pl.when(pend_ref[e] > prev)
            def _():
                tail_copy(e).wait()

        used = pend_ref[N_EXPERTS - 1]
        n_slack = (xs_ref.shape[0] - used) // MOE_SB

        def slack_copy(b):
            start = pl.multiple_of(used + b * MOE_SB, MOE_SB)
            return pltpu.make_async_copy(zbuf, xs_ref.at[pl.ds(start, MOE_SB), :], zsem)

        def slack_start(b, _):
            slack_copy(b).start()
            return 0

        def slack_wait(b, _):
            slack_copy(b).wait()
            return 0

        lax.fori_loop(0, n_slack, slack_start, 0)
        lax.fori_loop(0, n_slack, slack_wait, 0)

    def row_copy(r, k):
        p = pos_ref[2 * (i * tm + r) + k]
        return pltpu.make_async_copy(h2_ref.at[pl.ds(r, 1), :], xs_ref.at[pl.ds(p, 1), :], sem)

    def issue(r, _):
        row_copy(r, 0).start(priority=0)
        row_copy(r, 1).start(priority=1)
        return 0

    def drain(r, _):
        row_copy(r, 0).wait()
        row_copy(r, 1).wait()
        return 0

    lax.fori_loop(0, tm, issue, 0)
    lax.fori_loop(0, tm, drain, 0)


def _dispatch(pos_flat, pends, h2, n_rows):
    T = h2.shape[0]
    tm = DISP_TM
    return pl.pallas_call(
        _dispatch_body,
        grid_spec=pltpu.PrefetchScalarGridSpec(
            num_scalar_prefetch=2,
            grid=(T // tm,),
            in_specs=[pl.BlockSpec((tm, D_MODEL), lambda i, pos, pe: (i, 0))],
            out_specs=pl.BlockSpec(memory_space=pl.ANY),
            scratch_shapes=[
                pltpu.VMEM((MOE_SB, D_MODEL), F32),
                pltpu.SemaphoreType.DMA(()),
                pltpu.SemaphoreType.DMA(()),
            ],
        ),
        out_shape=jax.ShapeDtypeStruct((n_rows, D_MODEL), F32),
        compiler_params=pltpu.CompilerParams(
            dimension_semantics=("arbitrary",), has_side_effects=True),
    )(pos_flat, pends, h2)


def _moe_body(ie_ref, is_ref, in_ref, meta_ref, xs_ref, wg_hbm, wu_hbm, wd_hbm, ys_ref,
              xbuf, acc, wgbuf, wubuf, wdbuf, sem_in, sem_out, sem_w):
    s = pl.program_id(0)
    h = pl.program_id(1)
    n_items = pl.num_programs(0)
    n_h = pl.num_programs(1)
    slot = s % 2
    th = MOE_TH
    used = meta_ref[0]
    n_steps = meta_ref[1] * n_h
    t = s * n_h + h

    def n_blocks(item):
        return (in_ref[item] + MOE_SB - 1) // MOE_SB

    def in_copy(item, b):
        sl = item % 2
        off = pl.multiple_of(b * MOE_SB, MOE_SB)
        src = pl.multiple_of(is_ref[item] + off, MOE_SB)
        return pltpu.make_async_copy(xs_ref.at[pl.ds(src, MOE_SB), :],
                                     xbuf.at[sl, pl.ds(off, MOE_SB), :], sem_in.at[sl])

    def out_copy(item, b):
        sl = item % 2
        off = pl.multiple_of(b * MOE_SB, MOE_SB)
        dst = pl.multiple_of(is_ref[item] + off, MOE_SB)
        return pltpu.make_async_copy(acc.at[sl, pl.ds(off, MOE_SB), :],
                                     ys_ref.at[pl.ds(dst, MOE_SB), :], sem_out.at[sl])

    def for_blocks(item, fn):
        def body(b, carry):
            fn(item, b)
            return carry
        lax.fori_loop(0, n_blocks(item), body, 0)

    def start_in(item, b):
        in_copy(item, b).start()

    def wait_in(item, b):
        in_copy(item, b).wait()

    def start_out(item, b):
        out_copy(item, b).start()

    def wait_out(item, b):
        out_copy(item, b).wait()

    def weight_copies(step):
        e = ie_ref[step // n_h]
        col = pl.multiple_of((step % n_h) * th, th)
        ws = step % MOE_WBUF
        half = th // 2
        return (
            pltpu.make_async_copy(wg_hbm.at[e, :, pl.ds(col, th)], wgbuf.at[ws], sem_w.at[ws, 0]),
            pltpu.make_async_copy(wu_hbm.at[e, :, pl.ds(col, th)], wubuf.at[ws], sem_w.at[ws, 1]),
            pltpu.make_async_copy(wd_hbm.at[e, pl.ds(col, half), :],
                                  wdbuf.at[ws, pl.ds(0, half), :], sem_w.at[ws, 2]),
            pltpu.make_async_copy(wd_hbm.at[e, pl.ds(col + half, half), :],
                                  wdbuf.at[ws, pl.ds(half, half), :], sem_w.at[ws, 3]),
        )

    def start_weights(step):
        for k, c in enumerate(weight_copies(step)):
            c.start(priority=k % 2)

    @pl.when(t == 0)
    def _():
        for d in range(MOE_WBUF - 1):
            @pl.when(d < n_steps)
            def _(d=d):
                start_weights(d)

        n_slack = (ys_ref.shape[0] - used) // MOE_SB
        acc[1, pl.ds(0, MOE_SB), :] = jnp.zeros((MOE_SB, D_MODEL), F32)

        def slack_copy(b):
            dst = pl.multiple_of(used + b * MOE_SB, MOE_SB)
            return pltpu.make_async_copy(acc.at[1, pl.ds(0, MOE_SB), :],
                                         ys_ref.at[pl.ds(dst, MOE_SB), :], sem_out.at[1])

        def slack_start(b, _):
            slack_copy(b).start()
            return 0

        def slack_wait(b, _):
            slack_copy(b).wait()
            return 0

        for_blocks(s, start_in)
        lax.fori_loop(0, n_slack, slack_start, 0)
        lax.fori_loop(0, n_slack, slack_wait, 0)

    @pl.when(t + MOE_WBUF - 1 < n_steps)
    def _():
        start_weights(t + MOE_WBUF - 1)

    @pl.when(h == 0)
    def _():
        for_blocks(s, wait_in)

    @pl.when((h == 1) & (s + 1 < n_items))
    def _():
        for_blocks(s + 1, start_in)

    @pl.when(t < n_steps)
    def _():
        for c in weight_copies(t):
            c.wait()

    ws = t % MOE_WBUF

    def chunk(off, size, first):
        off = pl.multiple_of(off, MOE_SB)
        xb = xbuf[slot, pl.ds(off, size), :]
        gate = _dot(xb, wgbuf[ws])
        up = _dot(xb, wubuf[ws])
        hidden = gate * jax.nn.sigmoid(gate) * up
        contrib = _dot(hidden, wdbuf[ws])
        if first:
            acc[slot, pl.ds(off, size), :] = contrib
        else:
            acc[slot, pl.ds(off, size), :] += contrib

    n_pad = n_blocks(s) * MOE_SB
    off = 0
    for size in MOE_CHUNKS:
        take = (n_pad & size) != 0
        for first in (True, False):
            @pl.when(take & ((h == 0) if first else (h > 0)))
            def _(off=off, size=size, first=first):
                chunk(off, size, first)

        off = off + jnp.where(take, size, 0)

    @pl.when(h == n_h - 1)
    def _():
        @pl.when(s > 0)
        def _():
            for_blocks(s - 1, wait_out)

        for_blocks(s, start_out)

        @pl.when(s == n_items - 1)
        def _():
            for_blocks(s, wait_out)


def _moe(item_e, item_start, item_n, meta, xs, w_gate, w_up, w_down):
    n_rows = xs.shape[0]
    n_items = item_e.shape[0]
    th = MOE_TH
    hbm = pl.BlockSpec(memory_space=pl.ANY)
    return pl.pallas_call(
        _moe_body,
        grid_spec=pltpu.PrefetchScalarGridSpec(
            num_scalar_prefetch=4,
            grid=(n_items, D_EXPERT // th),
            in_specs=[hbm, hbm, hbm, hbm],
            out_specs=hbm,
            scratch_shapes=[
                pltpu.VMEM((2, MOE_R, D_MODEL), F32),
                pltpu.VMEM((2, MOE_R, D_MODEL), F32),
                pltpu.VMEM((MOE_WBUF, D_MODEL, th), F32),
                pltpu.VMEM((MOE_WBUF, D_MODEL, th), F32),
                pltpu.VMEM((MOE_WBUF, th, D_MODEL), F32),
                pltpu.SemaphoreType.DMA((2,)),
                pltpu.SemaphoreType.DMA((2,)),
                pltpu.SemaphoreType.DMA((MOE_WBUF, 4)),
            ],
        ),
        out_shape=jax.ShapeDtypeStruct((n_rows, D_MODEL), F32),
        compiler_params=pltpu.CompilerParams(
            dimension_semantics=("arbitrary", "arbitrary"), vmem_limit_bytes=VMEM_LIMIT,
            has_side_effects=True),
    )(item_e, item_start, item_n, meta, xs, w_gate, w_up, w_down)


def _combine_body(pos_ref, x1_ref, gw_ref, gf_ref, ys_ref, o_ref, ybuf, sem):
    i = pl.program_id(0)
    tm = x1_ref.shape[0]

    def row_copy(r, k):
        p = pos_ref[2 * (i * tm + r) + k]
        return pltpu.make_async_copy(ys_ref.at[pl.ds(p, 1), :], ybuf.at[k, pl.ds(r, 1), :], sem)

    def issue(r, _):
        row_copy(r, 0).start(priority=0)
        row_copy(r, 1).start(priority=1)
        return 0

    def drain(r, _):
        row_copy(r, 0).wait()
        row_copy(r, 1).wait()
        return 0

    lax.fori_loop(0, tm, issue, 0)
    lax.fori_loop(0, tm, drain, 0)

    gw = gw_ref[...]
    x = x1_ref[...] + gw[:, 0:1] * ybuf[0] + gw[:, 1:2] * ybuf[1]
    ms = jnp.mean(x * x, axis=-1, keepdims=True)
    o_ref[...] = x * lax.rsqrt(ms + EPS) * gf_ref[...]


def _combine(pos_flat, x1, gate_w, g_final, ys):
    T = x1.shape[0]
    tm = COMB_TM
    return pl.pallas_call(
        _combine_body,
        grid_spec=pltpu.PrefetchScalarGridSpec(
            num_scalar_prefetch=1,
            grid=(T // tm,),
            in_specs=[
                pl.BlockSpec((tm, D_MODEL), lambda i, pos: (i, 0)),
                pl.BlockSpec((tm, LANES), lambda i, pos: (i, 0)),
                pl.BlockSpec((1, D_MODEL), lambda i, pos: (0, 0)),
                pl.BlockSpec(memory_space=pl.ANY),
            ],
            out_specs=pl.BlockSpec((tm, D_MODEL), lambda i, pos: (i, 0)),
            scratch_shapes=[
                pltpu.VMEM((2, tm, D_MODEL), F32),
                pltpu.SemaphoreType.DMA(()),
            ],
        ),
        out_shape=jax.ShapeDtypeStruct((T, D_MODEL), F32),
        compiler_params=pltpu.CompilerParams(
            dimension_semantics=("arbitrary",), vmem_limit_bytes=VMEM_LIMIT),
    )(pos_flat, x1, gate_w, g_final.reshape(1, D_MODEL), ys)


def _moe_schedule(counts, n_items):
    padded = ((counts + MOE_SB - 1) // MOE_SB) * MOE_SB
    pends = jnp.cumsum(padded)
    pstarts = pends - padded
    chunks = (counts + MOE_R - 1) // MOE_R
    cends = jnp.cumsum(chunks)
    slot = jnp.arange(n_items, dtype=I32)
    total = cends[-1]
    live = slot < total
    owner = jnp.sum((slot[:, None] >= cends[None, :]).astype(I32), axis=1)
    last_owner = jnp.sum((total - 1 >= cends).astype(I32))
    owner = jnp.where(live, owner, last_owner)
    owner = jnp.minimum(owner, N_EXPERTS - 1)
    cidx = slot - (cends - chunks)[owner]
    item_start = jnp.where(live, pstarts[owner] + cidx * MOE_R, 0)
    item_n = jnp.where(live, jnp.clip(counts[owner] - cidx * MOE_R, 0, MOE_R), 0)
    meta = jnp.stack([pends[-1], total]).astype(I32)
    return pstarts.astype(I32), pends.astype(I32), owner.astype(I32), item_start.astype(I32), \
        item_n.astype(I32), meta


def kernel(x, g_mix, w_in, ln_v_g, ln_v_b, w_spatial, b_spatial, w_branch_a, w_branch_b, w_out,
           g_ffn, w_router_group, b_router_group, w_router_expert, b_router_expert, w_gate, w_up,
           w_down, g_final):
    B, S, D = x.shape
    T = B * S
    cur = x.reshape(T, D)
    depth = g_mix.shape[0]
    for l in range(depth):
        proj = _proj(cur, g_mix[l], w_in[l].astype(BF16))
        yb = _attn(proj)
        w_r = jnp.zeros((D, LANES), F32)
        w_r = w_r.at[:, :N_GROUPS].set(w_router_group[l])
        w_r = w_r.at[:, N_GROUPS:N_GROUPS + N_EXPERTS].set(w_router_expert[l])
        b_r = jnp.zeros((1, LANES), F32)
        b_r = b_r.at[0, :N_GROUPS].set(b_router_group[l])
        b_r = b_r.at[0, N_GROUPS:N_GROUPS + N_EXPERTS].set(b_router_expert[l])
        x1, h2, logits = _merge(
            proj, yb, cur, ln_v_g[l].reshape(1, A_WIDTH), ln_v_b[l].reshape(1, A_WIDTH),
            w_spatial[l], b_spatial[l].T, w_branch_a[l].astype(BF16), w_branch_b[l].astype(BF16),
            w_out[l].astype(BF16), g_ffn[l].reshape(1, D), w_r, b_r)
        oi, gate_w, cnt = _route(logits)

        counts = cnt[0, :N_EXPERTS].astype(I32)
        n_rows = 2 * T + N_EXPERTS * MOE_SB
        n_items = N_EXPERTS + -(-(2 * T) // MOE_R)
        pstarts, pends, item_e, item_start, item_n, meta = _moe_schedule(counts, n_items)
        pos = (pstarts[oi[:, 0:2]] + oi[:, 2:4]).reshape(2 * T)

        xs = _dispatch(pos, pends, h2, n_rows)
        ys = _moe(item_e, item_start, item_n, meta, xs, w_gate[l], w_up[l], w_down[l])
        if l + 1 < depth:
            raise NotImplementedError("only the final layer fuses the closing rms_norm")
        cur = _combine(pos, x1, gate_w, g_final, ys)
    return cur.reshape(B, S, D)
```

```python
import functools

import jax
import jax.numpy as jnp
from jax import lax
from jax.experimental import pallas as pl
from jax.experimental.pallas import tpu as pltpu

F32 = jnp.float32
BF16 = jnp.bfloat16
I32 = jnp.int32

D_MODEL = 2048
CHUNK = 128
A_GROUPS = 8
A_WIDTH = 1024
SB_HEADS = 8
SB_HEAD_DIM = 128
SB_WIDTH = 1024
N_GROUPS = 4
EXPERTS_PER_GROUP = 8
N_EXPERTS = 32
D_EXPERT = 1024
EPS = 1e-6
PROJ_WIDTH = 2 * A_WIDTH + 3 * SB_WIDTH + 2 * D_MODEL
LANES = 128
LOG2E = 1.4426950408889634
EXP2_ZERO_BELOW = -151.0

OFF_U, OFF_V = 0, A_WIDTH
OFF_Q = 2 * A_WIDTH
OFF_K = OFF_Q + SB_WIDTH
OFF_VV = OFF_K + SB_WIDTH
OFF_GA = OFF_VV + SB_WIDTH
OFF_GB = OFF_GA + D_MODEL

VMEM_LIMIT = 56 * 1024 * 1024

PROJ_TM, PROJ_TN = 1024, 512
ATT_TQ = 256
ATT_HEADS = 2
MERGE_TM = 256
ROUTE_TB = 512
DISP_TM = 512
ROW_DMA_UNROLL = 8
MOE_SB = 128
MOE_CHUNKS = (512, 256, 128)
MOE_R = 768
MOE_TH = 256
MOE_WBUF = 3
COMB_TM = 512


def _dot(a, b):
    return jnp.dot(a, b, preferred_element_type=F32)


def _proj_body(x_ref, g_ref, w_ref, o_ref, h_scr, *, n_gelu, n_plain):
    j = pl.program_id(1)

    @pl.when(j == 0)
    def _():
        x = x_ref[...]
        ms = jnp.mean(x * x, axis=-1, keepdims=True)
        h_scr[...] = (x * lax.rsqrt(ms + EPS) * g_ref[...]).astype(BF16)

    def product():
        return _dot(h_scr[...], w_ref[...].astype(BF16))

    @pl.when(j < n_gelu)
    def _():
        o_ref[...] = jax.nn.gelu(product()).astype(BF16)

    @pl.when((j >= n_gelu) & (j < n_gelu + n_plain))
    def _():
        o_ref[...] = product().astype(BF16)

    @pl.when(j >= n_gelu + n_plain)
    def _():
        o_ref[...] = jax.nn.sigmoid(product()).astype(BF16)


def _proj(x2, g_mix, w_in_bf):
    T = x2.shape[0]
    tm = min(PROJ_TM, T)
    tn = PROJ_TN
    body = functools.partial(_proj_body, n_gelu=OFF_Q // tn, n_plain=(OFF_GA - OFF_Q) // tn)
    return pl.pallas_call(
        body,
        grid=(T // tm, PROJ_WIDTH // tn),
        in_specs=[
            pl.BlockSpec((tm, D_MODEL), lambda i, j: (i, 0)),
            pl.BlockSpec((1, D_MODEL), lambda i, j: (0, 0)),
            pl.BlockSpec((D_MODEL, tn), lambda i, j: (0, j)),
        ],
        out_specs=pl.BlockSpec((tm, tn), lambda i, j: (i, j)),
        out_shape=jax.ShapeDtypeStruct((T, PROJ_WIDTH), BF16),
        scratch_shapes=[pltpu.VMEM((tm, D_MODEL), BF16)],
        compiler_params=pltpu.CompilerParams(
            dimension_semantics=("arbitrary", "arbitrary"), vmem_limit_bytes=VMEM_LIMIT),
    )(x2, g_mix.reshape(1, D_MODEL), w_in_bf)


def _attn_tile(q, k, v, upper2, carry, causal, live):
    z = lax.dot_general(q, k, (((1,), (1,)), ((), ())), preferred_element_type=F32)
    z = z * (SB_HEAD_DIM ** -0.5 * LOG2E)
    nz = -z
    soft = jnp.log2(1.0 + jnp.exp2(jnp.minimum(z, nz)))
    log_rem = jnp.minimum(nz, 0.0) - soft
    log_beta = z + log_rem
    if causal is not None:
        log_rem = jnp.where(causal, log_rem, 0.0)
    hi = log_rem.astype(BF16)
    lo = (log_rem - hi.astype(F32)).astype(BF16)
    suffix = _dot(jnp.concatenate([hi, lo], axis=1), upper2)
    w = jnp.exp2(log_beta + suffix + carry)
    if causal is not None:
        w = jnp.where(causal, w, 0.0)
    contrib = _dot(w.astype(BF16), v)
    total = jnp.sum(log_rem, axis=1, keepdims=True)
    if live is not None:
        contrib = contrib * live
        total = total * live
    return contrib, carry + total


def _attn_body(q_ref, k_ref, v_ref, o_ref):
    i = pl.program_id(1)
    t = ATT_TQ
    dh = SB_HEAD_DIM
    heads = range(ATT_HEADS)
    row = lax.broadcasted_iota(I32, (t, t), 0)
    col = lax.broadcasted_iota(I32, (t, t), 1)
    upper = (row > col).astype(BF16)
    upper2 = jnp.concatenate([upper, upper], axis=0)
    diagonal = col < row

    def tile(a, j, carry, causal=None, live=None):
        off = pl.multiple_of(jnp.maximum(j, 0) * t, t)
        hs = slice(a * dh, (a + 1) * dh)
        return _attn_tile(q_ref[:, hs], k_ref[pl.ds(off, t), hs], v_ref[pl.ds(off, t), hs],
                          upper2, carry, causal, live)

    def pair(j, accs, carries, causal):
        live = (j >= 1).astype(F32)
        new_accs, new_carries = [], []
        for a in heads:
            c1, carry = tile(a, j, carries[a], causal=causal)
            c2, carry = tile(a, j - 1, carry, live=live)
            new_accs.append(accs[a] + c1 + c2)
            new_carries.append(carry)
        return tuple(new_accs), tuple(new_carries)

    def cond(state):
        j, _, carries = state
        top = functools.reduce(jnp.maximum, carries)
        return (j >= 0) & (jnp.max(top) > EXP2_ZERO_BELOW)

    def step(state):
        j, accs, carries = state
        accs, carries = pair(j, accs, carries, None)
        return j - 2, accs, carries

    zero_acc = tuple(jnp.zeros((t, dh), F32) for _ in heads)
    zero_carry = tuple(jnp.zeros((t, 1), F32) for _ in heads)
    accs, carries = pair(i, zero_acc, zero_carry, diagonal)
    _, accs, _ = lax.while_loop(cond, step, (i - 2, accs, carries))
    for a in heads:
        o_ref[:, a * dh:(a + 1) * dh] = accs[a].astype(BF16)


def _attn(proj):
    T = proj.shape[0]
    width = ATT_HEADS * SB_HEAD_DIM
    qb, kb, vb = OFF_Q // width, OFF_K // width, OFF_VV // width
    return pl.pallas_call(
        _attn_body,
        grid=(SB_HEADS // ATT_HEADS, T // ATT_TQ),
        in_specs=[
            pl.BlockSpec((ATT_TQ, width), lambda h, i: (i, qb + h)),
            pl.BlockSpec((T, width), lambda h, i: (0, kb + h)),
            pl.BlockSpec((T, width), lambda h, i: (0, vb + h)),
        ],
        out_specs=pl.BlockSpec((ATT_TQ, width), lambda h, i: (i, h)),
        out_shape=jax.ShapeDtypeStruct((T, SB_WIDTH), BF16),
        compiler_params=pltpu.CompilerParams(
            dimension_semantics=("arbitrary", "arbitrary"), vmem_limit_bytes=VMEM_LIMIT),
    )(proj, proj, proj)


def _merge_body(u_ref, v_ref, ga0_ref, ga1_ref, gb0_ref, gb1_ref, yb_ref, x_ref, lng_ref, lnb_ref,
                ws_ref, bst_ref, wa_ref, wb_ref, wo_ref, gffn_ref, wr_ref, br_ref,
                x1_ref, h2_ref, lg_ref, ya_scr):
    tm = u_ref.shape[0]
    vf = v_ref[...].astype(F32)
    mu = jnp.mean(vf, axis=-1, keepdims=True)
    xc = vf - mu
    var = jnp.mean(xc * xc, axis=-1, keepdims=True)
    vln = (xc * lax.rsqrt(var + EPS) * lng_ref[...] + lnb_ref[...]).astype(BF16)

    row = lax.broadcasted_iota(I32, (CHUNK, CHUNK), 0)
    col = lax.broadcasted_iota(I32, (CHUNK, CHUNK), 1)
    tril = col <= row
    for g in range(A_GROUPS):
        wg = jnp.where(tril, ws_ref[g], 0.0).astype(BF16)
        bias = bst_ref[:, g:g + 1]
        cs = slice(g * LANES, (g + 1) * LANES)
        for c in range(tm // CHUNK):
            rs = slice(c * CHUNK, (c + 1) * CHUNK)
            mixed = _dot(wg, vln[rs, cs]) + bias
            ya_scr[rs, cs] = (u_ref[rs, cs].astype(F32) * mixed).astype(BF16)

    a = _dot(ya_scr[...], wa_ref[...])
    b = _dot(yb_ref[...], wb_ref[...])
    ga = jnp.concatenate([ga0_ref[...], ga1_ref[...]], axis=1).astype(F32)
    gb = jnp.concatenate([gb0_ref[...], gb1_ref[...]], axis=1).astype(F32)
    merged = (ga * a + gb * b).astype(BF16)
    x1 = x_ref[...] + _dot(merged, wo_ref[...])
    x1_ref[...] = x1
    ms = jnp.mean(x1 * x1, axis=-1, keepdims=True)
    h2 = x1 * lax.rsqrt(ms + EPS) * gffn_ref[...]
    h2_ref[...] = h2
    hi = h2.astype(BF16)
    lo = (h2 - hi.astype(F32)).astype(BF16)
    wr = wr_ref[...]
    whi = wr.astype(BF16)
    wlo = (wr - whi.astype(F32)).astype(BF16)
    lg_ref[...] = _dot(hi, whi) + _dot(hi, wlo) + _dot(lo, whi) + br_ref[...]


def _merge(proj, yb, x2, ln_g, ln_b, w_s, b_s_t, wa, wb, wo, g_ffn, w_r, b_r):
    T = x2.shape[0]
    tm = MERGE_TM
    const2 = lambda i: (0, 0)
    one = pl.Buffered(1)

    def colblk(width, off):
        return pl.BlockSpec((tm, width), lambda i: (i, off // width))

    return pl.pallas_call(
        _merge_body,
        grid=(T // tm,),
        in_specs=[
            colblk(A_WIDTH, OFF_U), colblk(A_WIDTH, OFF_V),
            colblk(A_WIDTH, OFF_GA), colblk(A_WIDTH, OFF_GA + A_WIDTH),
            colblk(A_WIDTH, OFF_GB), colblk(A_WIDTH, OFF_GB + A_WIDTH),
            pl.BlockSpec((tm, SB_WIDTH), lambda i: (i, 0)),
            pl.BlockSpec((tm, D_MODEL), lambda i: (i, 0)),
            pl.BlockSpec((1, A_WIDTH), const2), pl.BlockSpec((1, A_WIDTH), const2),
            pl.BlockSpec((A_GROUPS, CHUNK, CHUNK), lambda i: (0, 0, 0)),
            pl.BlockSpec((CHUNK, A_GROUPS), const2),
            pl.BlockSpec((A_WIDTH, D_MODEL), const2, pipeline_mode=one),
            pl.BlockSpec((SB_WIDTH, D_MODEL), const2, pipeline_mode=one),
            pl.BlockSpec((D_MODEL, D_MODEL), const2, pipeline_mode=one),
            pl.BlockSpec((1, D_MODEL), const2),
            pl.BlockSpec((D_MODEL, LANES), const2, pipeline_mode=one),
            pl.BlockSpec((1, LANES), const2),
        ],
        out_specs=[
            pl.BlockSpec((tm, D_MODEL), lambda i: (i, 0)),
            pl.BlockSpec((tm, D_MODEL), lambda i: (i, 0)),
            pl.BlockSpec((tm, LANES), lambda i: (i, 0)),
        ],
        out_shape=[
            jax.ShapeDtypeStruct((T, D_MODEL), F32),
            jax.ShapeDtypeStruct((T, D_MODEL), F32),
            jax.ShapeDtypeStruct((T, LANES), F32),
        ],
        scratch_shapes=[pltpu.VMEM((tm, A_WIDTH), BF16)],
        compiler_params=pltpu.CompilerParams(
            dimension_semantics=("arbitrary",), vmem_limit_bytes=VMEM_LIMIT),
    )(proj, proj, proj, proj, proj, proj, yb, x2, ln_g, ln_b, w_s, b_s_t, wa, wb, wo, g_ffn, w_r, b_r)


def _route_body(lg_ref, oi_ref, ow_ref, cnt_ref, carry_scr):
    i = pl.program_id(0)
    tb = lg_ref.shape[0]

    @pl.when(i == 0)
    def _():
        carry_scr[...] = jnp.zeros_like(carry_scr)

    lg = lg_ref[...]
    lane = lax.broadcasted_iota(I32, (tb, LANES), 1)
    neg = jnp.float32(-jnp.inf)
    big = jnp.int32(LANES)

    def first_max(mask):
        m = jnp.max(jnp.where(mask, lg, neg), axis=1, keepdims=True)
        idx = jnp.min(jnp.where(mask & (lg == m), lane, big), axis=1, keepdims=True)
        return m, idx

    gmask = lane < N_GROUPS
    gm, grp = first_max(gmask)
    pg_sel = 1.0 / jnp.sum(jnp.where(gmask, jnp.exp(lg - gm), 0.0), axis=1, keepdims=True)

    eid = lane - N_GROUPS
    emask = (eid >= 0) & (eid < N_EXPERTS) & ((eid // EXPERTS_PER_GROUP) == grp)
    m1, i1 = first_max(emask)
    emask2 = emask & (lane != i1)
    m2, i2 = first_max(emask2)
    t = jnp.exp(m2 - m1)
    p1 = 1.0 / (1.0 + t)
    p2 = t / (1.0 + t)
    e1 = i1 - N_GROUPS
    e2 = i2 - N_GROUPS

    sel1 = lane == e1
    sel2 = lane == e2
    onehot = (sel1 | sel2).astype(BF16)
    r = lax.broadcasted_iota(I32, (tb, tb), 0)
    c = lax.broadcasted_iota(I32, (tb, tb), 1)
    earlier = (c < r).astype(BF16)
    prefix = _dot(earlier, onehot) + carry_scr[...]
    rank1 = jnp.sum(jnp.where(sel1, prefix, 0.0), axis=1, keepdims=True).astype(I32)
    rank2 = jnp.sum(jnp.where(sel2, prefix, 0.0), axis=1, keepdims=True).astype(I32)
    carry_scr[...] = carry_scr[...] + jnp.sum(onehot.astype(F32), axis=0, keepdims=True)

    zero = jnp.zeros((tb, LANES), I32)
    oi_ref[...] = jnp.where(lane == 0, e1, jnp.where(lane == 1, e2,
                            jnp.where(lane == 2, rank1, jnp.where(lane == 3, rank2, zero))))
    ow_ref[...] = jnp.where(lane == 0, pg_sel * p1, jnp.where(lane == 1, pg_sel * p2, 0.0))
    cnt_ref[...] = carry_scr[...]


def _route(logits):
    T = logits.shape[0]
    tb = min(ROUTE_TB, T)
    return pl.pallas_call(
        _route_body,
        grid=(T // tb,),
        in_specs=[pl.BlockSpec((tb, LANES), lambda i: (i, 0))],
        out_specs=[
            pl.BlockSpec((tb, LANES), lambda i: (i, 0)),
            pl.BlockSpec((tb, LANES), lambda i: (i, 0)),
            pl.BlockSpec((1, LANES), lambda i: (0, 0)),
        ],
        out_shape=[
            jax.ShapeDtypeStruct((T, LANES), I32),
            jax.ShapeDtypeStruct((T, LANES), F32),
            jax.ShapeDtypeStruct((1, LANES), F32),
        ],
        scratch_shapes=[pltpu.VMEM((1, LANES), F32)],
        compiler_params=pltpu.CompilerParams(dimension_semantics=("arbitrary",)),
    )(logits)


def _dispatch_body(pos_ref, pend_ref, h2_ref, xs_ref, zbuf, sem, zsem):
    i = pl.program_id(0)
    tm = h2_ref.shape[0]

    def tail_copy(e):
        start = pl.multiple_of(pend_ref[e] - MOE_SB, MOE_SB)
        return pltpu.make_async_copy(zbuf, xs_ref.at[pl.ds(start, MOE_SB), :], zsem)

    @pl.when(i == 0)
    def _():
        zbuf[...] = jnp.zeros_like(zbuf)
        for e in range(N_EXPERTS):
            prev = pend_ref[e - 1] if e > 0 else 0

            @pl.when(pend_ref[e] > prev)
            def _():
                tail_copy(e).start()
        for e in range(N_EXPERTS):
            prev = pend_ref[e - 1] if e > 0 else 0

            @pl.when(pend_ref[e] > prev)
            def _():
                tail_copy(e).wait()

        used = pend_ref[N_EXPERTS - 1]
        n_slack = (xs_ref.shape[0] - used) // MOE_SB

        def slack_copy(b):
            start = pl.multiple_of(used + b * MOE_SB, MOE_SB)
            return pltpu.make_async_copy(zbuf, xs_ref.at[pl.ds(start, MOE_SB), :], zsem)

        def slack_start(b, _):
            slack_copy(b).start()
            return 0

        def slack_wait(b, _):
            slack_copy(b).wait()
            return 0

        lax.fori_loop(0, n_slack, slack_start, 0)
        lax.fori_loop(0, n_slack, slack_wait, 0)

    def row_copy(r, k):
        p = pos_ref[2 * (i * tm + r) + k]
        return pltpu.make_async_copy(h2_ref.at[pl.ds(r, 1), :], xs_ref.at[pl.ds(p, 1), :], sem)

    def issue(r, _):
        row_copy(r, 0).start(priority=0)
        row_copy(r, 1).start(priority=1)
        return 0

    def drain(r, _):
        row_copy(r, 0).wait()
        row_copy(r, 1).wait()
        return 0

    lax.fori_loop(0, tm, issue, 0, unroll=ROW_DMA_UNROLL)
    lax.fori_loop(0, tm, drain, 0, unroll=ROW_DMA_UNROLL)


def _dispatch(pos_flat, pends, h2, n_rows):
    T = h2.shape[0]
    tm = DISP_TM
    return pl.pallas_call(
        _dispatch_body,
        grid_spec=pltpu.PrefetchScalarGridSpec(
            num_scalar_prefetch=2,
            grid=(T // tm,),
            in_specs=[pl.BlockSpec((tm, D_MODEL), lambda i, pos, pe: (i, 0))],
            out_specs=pl.BlockSpec(memory_space=pl.ANY),
            scratch_shapes=[
                pltpu.VMEM((MOE_SB, D_MODEL), F32),
                pltpu.SemaphoreType.DMA(()),
                pltpu.SemaphoreType.DMA(()),
            ],
        ),
        out_shape=jax.ShapeDtypeStruct((n_rows, D_MODEL), F32),
        compiler_params=pltpu.CompilerParams(
            dimension_semantics=("arbitrary",), has_side_effects=True),
    )(pos_flat, pends, h2)


def _moe_body(ie_ref, is_ref, in_ref, meta_ref, xs_ref, wg_hbm, wu_hbm, wd_hbm, ys_ref,
              xbuf, acc, wgbuf, wubuf, wdbuf, sem_in, sem_out, sem_w):
    s = pl.program_id(0)
    h = pl.program_id(1)
    n_items = pl.num_programs(0)
    n_h = pl.num_programs(1)
    slot = s % 2
    th = MOE_TH
    used = meta_ref[0]
    n_steps = meta_ref[1] * n_h
    t = s * n_h + h

    def n_blocks(item):
        return (in_ref[item] + MOE_SB - 1) // MOE_SB

    def in_copy(item, b):
        sl = item % 2
        off = pl.multiple_of(b * MOE_SB, MOE_SB)
        src = pl.multiple_of(is_ref[item] + off, MOE_SB)
        return pltpu.make_async_copy(xs_ref.at[pl.ds(src, MOE_SB), :],
                                     xbuf.at[sl, pl.ds(off, MOE_SB), :], sem_in.at[sl])

    def out_copy(item, b):
        sl = item % 2
        off = pl.multiple_of(b * MOE_SB, MOE_SB)
        dst = pl.multiple_of(is_ref[item] + off, MOE_SB)
        return pltpu.make_async_copy(acc.at[sl, pl.ds(off, MOE_SB), :],
                                     ys_ref.at[pl.ds(dst, MOE_SB), :], sem_out.at[sl])

    def for_blocks(item, fn):
        def body(b, carry):
            fn(item, b)
            return carry
        lax.fori_loop(0, n_blocks(item), body, 0)

    def start_in(item, b):
        in_copy(item, b).start()

    def wait_in(item, b):
        in_copy(item, b).wait()

    def start_out(item, b):
        out_copy(item, b).start()

    def wait_out(item, b):
        out_copy(item, b).wait()

    def weight_copies(step):
        e = ie_ref[step // n_h]
        col = pl.multiple_of((step % n_h) * th, th)
        ws = step % MOE_WBUF
        half = th // 2
        return (
            pltpu.make_async_copy(wg_hbm.at[e, :, pl.ds(col, th)], wgbuf.at[ws], sem_w.at[ws, 0]),
            pltpu.make_async_copy(wu_hbm.at[e, :, pl.ds(col, th)], wubuf.at[ws], sem_w.at[ws, 1]),
            pltpu.make_async_copy(wd_hbm.at[e, pl.ds(col, half), :],
                                  wdbuf.at[ws, pl.ds(0, half), :], sem_w.at[ws, 2]),
            pltpu.make_async_copy(wd_hbm.at[e, pl.ds(col + half, half), :],
                                  wdbuf.at[ws, pl.ds(half, half), :], sem_w.at[ws, 3]),
        )

    def start_weights(step):
        for k, c in enumerate(weight_copies(step)):
            c.start(priority=k % 2)

    @pl.when(t == 0)
    def _():
        for d in range(MOE_WBUF - 1):
            @pl.when(d < n_steps)
            def _(d=d):
                start_weights(d)

        n_slack = (ys_ref.shape[0] - used) // MOE_SB
        acc[1, pl.ds(0, MOE_SB), :] = jnp.zeros((MOE_SB, D_MODEL), F32)

        def slack_copy(b):
            dst = pl.multiple_of(used + b * MOE_SB, MOE_SB)
            return pltpu.make_async_copy(acc.at[1, pl.ds(0, MOE_SB), :],
                                         ys_ref.at[pl.ds(dst, MOE_SB), :], sem_out.at[1])

        def slack_start(b, _):
            slack_copy(b).start()
            return 0

        def slack_wait(b, _):
            slack_copy(b).wait()
            return 0

        for_blocks(s, start_in)
        lax.fori_loop(0, n_slack, slack_start, 0)
        lax.fori_loop(0, n_slack, slack_wait, 0)

    @pl.when(t + MOE_WBUF - 1 < n_steps)
    def _():
        start_weights(t + MOE_WBUF - 1)

    @pl.when(h == 0)
    def _():
        for_blocks(s, wait_in)

    @pl.when((h == 1) & (s + 1 < n_items))
    def _():
        for_blocks(s + 1, start_in)

    @pl.when(t < n_steps)
    def _():
        for c in weight_copies(t):
            c.wait()

    ws = t % MOE_WBUF

    def chunk(off, size, first):
        off = pl.multiple_of(off, MOE_SB)
        xb = xbuf[slot, pl.ds(off, size), :]
        gate = _dot(xb, wgbuf[ws])
        up = _dot(xb, wubuf[ws])
        hidden = gate * jax.nn.sigmoid(gate) * up
        contrib = _dot(hidden, wdbuf[ws])
        if first:
            acc[slot, pl.ds(off, size), :] = contrib
        else:
            acc[slot, pl.ds(off, size), :] += contrib

    n_pad = n_blocks(s) * MOE_SB
    off = 0
    for size in MOE_CHUNKS:
        take = (n_pad & size) != 0
        for first in (True, False):
            @pl.when(take & ((h == 0) if first else (h > 0)))
            def _(off=off, size=size, first=first):
                chunk(off, size, first)

        off = off + jnp.where(take, size, 0)

    @pl.when(h == n_h - 1)
    def _():
        @pl.when(s > 0)
        def _():
            for_blocks(s - 1, wait_out)

        for_blocks(s, start_out)

        @pl.when(s == n_items - 1)
        def _():
            for_blocks(s, wait_out)


def _moe(item_e, item_start, item_n, meta, xs, w_gate, w_up, w_down):
    n_rows = xs.shape[0]
    n_items = item_e.shape[0]
    th = MOE_TH
    hbm = pl.BlockSpec(memory_space=pl.ANY)
    return pl.pallas_call(
        _moe_body,
        grid_spec=pltpu.PrefetchScalarGridSpec(
            num_scalar_prefetch=4,
            grid=(n_items, D_EXPERT // th),
            in_specs=[hbm, hbm, hbm, hbm],
            out_specs=hbm,
            scratch_shapes=[
                pltpu.VMEM((2, MOE_R, D_MODEL), F32),
                pltpu.VMEM((2, MOE_R, D_MODEL), F32),
                pltpu.VMEM((MOE_WBUF, D_MODEL, th), F32),
                pltpu.VMEM((MOE_WBUF, D_MODEL, th), F32),
                pltpu.VMEM((MOE_WBUF, th, D_MODEL), F32),
                pltpu.SemaphoreType.DMA((2,)),
                pltpu.SemaphoreType.DMA((2,)),
                pltpu.SemaphoreType.DMA((MOE_WBUF, 4)),
            ],
        ),
        out_shape=jax.ShapeDtypeStruct((n_rows, D_MODEL), F32),
        compiler_params=pltpu.CompilerParams(
            dimension_semantics=("arbitrary", "arbitrary"), vmem_limit_bytes=VMEM_LIMIT,
            has_side_effects=True),
    )(item_e, item_start, item_n, meta, xs, w_gate, w_up, w_down)


def _combine_body(pos_ref, x1_ref, gw_ref, gf_ref, ys_ref, o_ref, ybuf, sem):
    i = pl.program_id(0)
    tm = x1_ref.shape[0]

    def row_copy(r, k):
        p = pos_ref[2 * (i * tm + r) + k]
        return pltpu.make_async_copy(ys_ref.at[pl.ds(p, 1), :], ybuf.at[k, pl.ds(r, 1), :], sem)

    def issue(r, _):
        row_copy(r, 0).start(priority=0)
        row_copy(r, 1).start(priority=1)
        return 0

    def drain(r, _):
        row_copy(r, 0).wait()
        row_copy(r, 1).wait()
        return 0

    lax.fori_loop(0, tm, issue, 0, unroll=ROW_DMA_UNROLL)
    lax.fori_loop(0, tm, drain, 0, unroll=ROW_DMA_UNROLL)

    gw = gw_ref[...]
    x = x1_ref[...] + gw[:, 0:1] * ybuf[0] + gw[:, 1:2] * ybuf[1]
    ms = jnp.mean(x * x, axis=-1, keepdims=True)
    o_ref[...] = x * lax.rsqrt(ms + EPS) * gf_ref[...]


def _combine(pos_flat, x1, gate_w, g_final, ys):
    T = x1.shape[0]
    tm = COMB_TM
    return pl.pallas_call(
        _combine_body,
        grid_spec=pltpu.PrefetchScalarGridSpec(
            num_scalar_prefetch=1,
            grid=(T // tm,),
            in_specs=[
                pl.BlockSpec((tm, D_MODEL), lambda i, pos: (i, 0)),
                pl.BlockSpec((tm, LANES), lambda i, pos: (i, 0)),
                pl.BlockSpec((1, D_MODEL), lambda i, pos: (0, 0)),
                pl.BlockSpec(memory_space=pl.ANY),
            ],
            out_specs=pl.BlockSpec((tm, D_MODEL), lambda i, pos: (i, 0)),
            scratch_shapes=[
                pltpu.VMEM((2, tm, D_MODEL), F32),
                pltpu.SemaphoreType.DMA(()),
            ],
        ),
        out_shape=jax.ShapeDtypeStruct((T, D_MODEL), F32),
        compiler_params=pltpu.CompilerParams(
            dimension_semantics=("arbitrary",), vmem_limit_bytes=VMEM_LIMIT),
    )(pos_flat, x1, gate_w, g_final.reshape(1, D_MODEL), ys)


def _moe_schedule(counts, n_items):
    padded = ((counts + MOE_SB - 1) // MOE_SB) * MOE_SB
    pends = jnp.cumsum(padded)
    pstarts = pends - padded
    chunks = (counts + MOE_R - 1) // MOE_R
    cends = jnp.cumsum(chunks)
    slot = jnp.arange(n_items, dtype=I32)
    total = cends[-1]
    live = slot < total
    owner = jnp.sum((slot[:, None] >= cends[None, :]).astype(I32), axis=1)
    last_owner = jnp.sum((total - 1 >= cends).astype(I32))
    owner = jnp.where(live, owner, last_owner)
    owner = jnp.minimum(owner, N_EXPERTS - 1)
    cidx = slot - (cends - chunks)[owner]
    item_start = jnp.where(live, pstarts[owner] + cidx * MOE_R, 0)
    item_n = jnp.where(live, jnp.clip(counts[owner] - cidx * MOE_R, 0, MOE_R), 0)
    meta = jnp.stack([pends[-1], total]).astype(I32)
    return pstarts.astype(I32), pends.astype(I32), owner.astype(I32), item_start.astype(I32), \
        item_n.astype(I32), meta


def kernel(x, g_mix, w_in, ln_v_g, ln_v_b, w_spatial, b_spatial, w_branch_a, w_branch_b, w_out,
           g_ffn, w_router_group, b_router_group, w_router_expert, b_router_expert, w_gate, w_up,
           w_down, g_final):
    B, S, D = x.shape
    T = B * S
    cur = x.reshape(T, D)
    depth = g_mix.shape[0]
    for l in range(depth):
        proj = _proj(cur, g_mix[l], w_in[l])
        yb = _attn(proj)
        w_r = jnp.zeros((D, LANES), F32)
        w_r = w_r.at[:, :N_GROUPS].set(w_router_group[l])
        w_r = w_r.at[:, N_GROUPS:N_GROUPS + N_EXPERTS].set(w_router_expert[l])
        b_r = jnp.zeros((1, LANES), F32)
        b_r = b_r.at[0, :N_GROUPS].set(b_router_group[l])
        b_r = b_r.at[0, N_GROUPS:N_GROUPS + N_EXPERTS].set(b_router_expert[l])
        x1, h2, logits = _merge(
            proj, yb, cur, ln_v_g[l].reshape(1, A_WIDTH), ln_v_b[l].reshape(1, A_WIDTH),
            w_spatial[l], b_spatial[l].T, w_branch_a[l].astype(BF16), w_branch_b[l].astype(BF16),
            w_out[l].astype(BF16), g_ffn[l].reshape(1, D), w_r, b_r)
        oi, gate_w, cnt = _route(logits)

        counts = cnt[0, :N_EXPERTS].astype(I32)
        n_rows = 2 * T + N_EXPERTS * MOE_SB
        n_items = N_EXPERTS + -(-(2 * T) // MOE_R)
        pstarts, pends, item_e, item_start, item_n, meta = _moe_schedule(counts, n_items)
        chosen = oi[:, 0:2, None] == jnp.arange(N_EXPERTS, dtype=I32)
        pos = (jnp.sum(jnp.where(chosen, pstarts, 0), axis=-1) + oi[:, 2:4]).reshape(2 * T)

        xs = _dispatch(pos, pends, h2, n_rows)
        ys = _moe(item_e, item_start, item_n, meta, xs, w_gate[l], w_up[l], w_down[l])
        if l + 1 < depth:
            raise NotImplementedError("only the final layer fuses the closing rms_norm")
        cur = _combine(pos, x1, gate_w, g_final, ys)
    return cur.reshape(B, S, D)
```

```python
import functools

import jax
import jax.numpy as jnp
from jax import lax
from jax.experimental import pallas as pl
from jax.experimental.pallas import tpu as pltpu

F32 = jnp.float32
BF16 = jnp.bfloat16
I32 = jnp.int32

D_MODEL = 2048
CHUNK = 128
A_GROUPS = 8
A_WIDTH = 1024
SB_HEADS = 8
SB_HEAD_DIM = 128
SB_WIDTH = 1024
N_GROUPS = 4
EXPERTS_PER_GROUP = 8
N_EXPERTS = 32
D_EXPERT = 1024
EPS = 1e-6
PROJ_WIDTH = 2 * A_WIDTH + 3 * SB_WIDTH + 2 * D_MODEL
LANES = 128
LOG2E = 1.4426950408889634
EXP2_ZERO_BELOW = -151.0

OFF_U, OFF_V = 0, A_WIDTH
OFF_Q = 2 * A_WIDTH
OFF_K = OFF_Q + SB_WIDTH
OFF_VV = OFF_K + SB_WIDTH
OFF_GA = OFF_VV + SB_WIDTH
OFF_GB = OFF_GA + D_MODEL

VMEM_LIMIT = 56 * 1024 * 1024

NORM_TM = 512
PROJ_TM, PROJ_TN = 1024, 256
ATT_TQ = 256
ATT_HEADS = 4
MERGE_TM = 256
ROUTE_TB = 512
DISP_TM = 512
ROW_DMA_UNROLL = 8
MOE_SB = 128
MOE_CHUNKS = (512, 256, 128)
MOE_R = 768
MOE_TH = 256
MOE_WBUF = 3
COMB_TM = 512


def _dot(a, b):
    return jnp.dot(a, b, preferred_element_type=F32)


def _norm_body(x_ref, g_ref, o_ref):
    x = x_ref[...]
    ms = jnp.mean(x * x, axis=-1, keepdims=True)
    o_ref[...] = (x * lax.rsqrt(ms + EPS) * g_ref[...]).astype(BF16)


def _norm(x2, g):
    T = x2.shape[0]
    tm = min(NORM_TM, T)
    return pl.pallas_call(
        _norm_body,
        grid=(T // tm,),
        in_specs=[pl.BlockSpec((tm, D_MODEL), lambda i: (i, 0)),
                  pl.BlockSpec((1, D_MODEL), lambda i: (0, 0))],
        out_specs=pl.BlockSpec((tm, D_MODEL), lambda i: (i, 0)),
        out_shape=jax.ShapeDtypeStruct((T, D_MODEL), BF16),
        compiler_params=pltpu.CompilerParams(dimension_semantics=("arbitrary",)),
    )(x2, g.reshape(1, D_MODEL))


def _proj_body(h_ref, w_ref, o_ref, *, n_gelu, n_plain, rows):
    j = pl.program_id(0)
    n_chunks = h_ref.shape[0] // rows

    def column_block(act):
        w = w_ref[...].astype(BF16)

        def chunk(c, carry):
            r = pl.ds(pl.multiple_of(c * rows, rows), rows)
            o_ref[r, :] = act(_dot(h_ref[r, :], w)).astype(BF16)
            return carry

        lax.fori_loop(0, n_chunks, chunk, 0)

    @pl.when(j < n_gelu)
    def _():
        column_block(jax.nn.gelu)

    @pl.when((j >= n_gelu) & (j < n_gelu + n_plain))
    def _():
        column_block(lambda v: v)

    @pl.when(j >= n_gelu + n_plain)
    def _():
        column_block(jax.nn.sigmoid)


def _proj(h, w_in):
    T = h.shape[0]
    tn = PROJ_TN
    body = functools.partial(_proj_body, n_gelu=OFF_Q // tn, n_plain=(OFF_GA - OFF_Q) // tn,
                             rows=min(PROJ_TM, T))
    return pl.pallas_call(
        body,
        grid=(PROJ_WIDTH // tn,),
        in_specs=[
            pl.BlockSpec((T, D_MODEL), lambda j: (0, 0), pipeline_mode=pl.Buffered(1)),
            pl.BlockSpec((D_MODEL, tn), lambda j: (0, j)),
        ],
        out_specs=pl.BlockSpec((T, tn), lambda j: (0, j)),
        out_shape=jax.ShapeDtypeStruct((T, PROJ_WIDTH), BF16),
        compiler_params=pltpu.CompilerParams(
            dimension_semantics=("arbitrary",), vmem_limit_bytes=VMEM_LIMIT),
    )(h, w_in)


def _attn_tile(q, k, v, upper2, carry, causal, live):
    z = lax.dot_general(q, k, (((1,), (1,)), ((), ())), preferred_element_type=F32)
    z = z * (SB_HEAD_DIM ** -0.5 * LOG2E)
    nz = -z
    soft = jnp.log2(1.0 + jnp.exp2(jnp.minimum(z, nz)))
    log_rem = jnp.minimum(nz, 0.0) - soft
    log_beta = z + log_rem
    if causal is not None:
        log_rem = jnp.where(causal, log_rem, 0.0)
    hi = log_rem.astype(BF16)
    lo = (log_rem - hi.astype(F32)).astype(BF16)
    suffix = _dot(jnp.concatenate([hi, lo], axis=1), upper2)
    w = jnp.exp2(log_beta + suffix + carry)
    if causal is not None:
        w = jnp.where(causal, w, 0.0)
    contrib = _dot(w.astype(BF16), v)
    total = jnp.sum(log_rem, axis=1, keepdims=True)
    if live is not None:
        contrib = contrib * live
        total = total * live
    return contrib, carry + total


def _attn_body(q_ref, k_ref, v_ref, o_ref):
    i = pl.program_id(1)
    t = ATT_TQ
    dh = SB_HEAD_DIM
    heads = range(ATT_HEADS)
    row = lax.broadcasted_iota(I32, (t, t), 0)
    col = lax.broadcasted_iota(I32, (t, t), 1)
    upper = (row > col).astype(BF16)
    upper2 = jnp.concatenate([upper, upper], axis=0)
    diagonal = col < row

    def tile(a, j, carry, causal=None, live=None):
        off = pl.multiple_of(jnp.maximum(j, 0) * t, t)
        hs = slice(a * dh, (a + 1) * dh)
        return _attn_tile(q_ref[:, hs], k_ref[pl.ds(off, t), hs], v_ref[pl.ds(off, t), hs],
                          upper2, carry, causal, live)

    def pair(j, accs, carries, causal):
        live = (j >= 1).astype(F32)
        new_accs, new_carries = [], []
        for a in heads:
            c1, carry = tile(a, j, carries[a], causal=causal)
            c2, carry = tile(a, j - 1, carry, live=live)
            new_accs.append(accs[a] + c1 + c2)
            new_carries.append(carry)
        return tuple(new_accs), tuple(new_carries)

    def cond(state):
        j, _, carries = state
        top = functools.reduce(jnp.maximum, carries)
        return (j >= 0) & (jnp.max(top) > EXP2_ZERO_BELOW)

    def step(state):
        j, accs, carries = state
        accs, carries = pair(j, accs, carries, None)
        return j - 2, accs, carries

    zero_acc = tuple(jnp.zeros((t, dh), F32) for _ in heads)
    zero_carry = tuple(jnp.zeros((t, 1), F32) for _ in heads)
    accs, carries = pair(i, zero_acc, zero_carry, diagonal)
    _, accs, _ = lax.while_loop(cond, step, (i - 2, accs, carries))
    for a in heads:
        o_ref[:, a * dh:(a + 1) * dh] = accs[a].astype(BF16)


def _attn(proj):
    T = proj.shape[0]
    width = ATT_HEADS * SB_HEAD_DIM
    qb, kb, vb = OFF_Q // width, OFF_K // width, OFF_VV // width
    return pl.pallas_call(
        _attn_body,
        grid=(SB_HEADS // ATT_HEADS, T // ATT_TQ),
        in_specs=[
            pl.BlockSpec((ATT_TQ, width), lambda h, i: (i, qb + h)),
            pl.BlockSpec((T, width), lambda h, i: (0, kb + h)),
            pl.BlockSpec((T, width), lambda h, i: (0, vb + h)),
        ],
        out_specs=pl.BlockSpec((ATT_TQ, width), lambda h, i: (i, h)),
        out_shape=jax.ShapeDtypeStruct((T, SB_WIDTH), BF16),
        compiler_params=pltpu.CompilerParams(
            dimension_semantics=("arbitrary", "arbitrary"), vmem_limit_bytes=VMEM_LIMIT),
    )(proj, proj, proj)


def _merge_body(u_ref, v_ref, ga0_ref, ga1_ref, gb0_ref, gb1_ref, yb_ref, x_ref, lng_ref, lnb_ref,
                ws_ref, bst_ref, wa_ref, wb_ref, wo_ref, gffn_ref, wr_ref, br_ref,
                x1_ref, h2_ref, lg_ref, ya_scr):
    tm = u_ref.shape[0]
    vf = v_ref[...].astype(F32)
    mu = jnp.mean(vf, axis=-1, keepdims=True)
    xc = vf - mu
    var = jnp.mean(xc * xc, axis=-1, keepdims=True)
    vln = (xc * lax.rsqrt(var + EPS) * lng_ref[...] + lnb_ref[...]).astype(BF16)

    row = lax.broadcasted_iota(I32, (CHUNK, CHUNK), 0)
    col = lax.broadcasted_iota(I32, (CHUNK, CHUNK), 1)
    tril = col <= row
    for g in range(A_GROUPS):
        wg = jnp.where(tril, ws_ref[g], 0.0).astype(BF16)
        bias = bst_ref[:, g:g + 1]
        cs = slice(g * LANES, (g + 1) * LANES)
        for c in range(tm // CHUNK):
            rs = slice(c * CHUNK, (c + 1) * CHUNK)
            mixed = _dot(wg, vln[rs, cs]) + bias
            ya_scr[rs, cs] = (u_ref[rs, cs].astype(F32) * mixed).astype(BF16)

    a = _dot(ya_scr[...], wa_ref[...])
    b = _dot(yb_ref[...], wb_ref[...])
    ga = jnp.concatenate([ga0_ref[...], ga1_ref[...]], axis=1).astype(F32)
    gb = jnp.concatenate([gb0_ref[...], gb1_ref[...]], axis=1).astype(F32)
    merged = (ga * a + gb * b).astype(BF16)
    x1 = x_ref[...] + _dot(merged, wo_ref[...])
    x1_ref[...] = x1
    ms = jnp.mean(x1 * x1, axis=-1, keepdims=True)
    h2 = x1 * lax.rsqrt(ms + EPS) * gffn_ref[...]
    h2_ref[...] = h2
    hi = h2.astype(BF16)
    lo = (h2 - hi.astype(F32)).astype(BF16)
    wr = wr_ref[...]
    whi = wr.astype(BF16)
    wlo = (wr - whi.astype(F32)).astype(BF16)
    lg_ref[...] = _dot(hi, whi) + _dot(hi, wlo) + _dot(lo, whi) + br_ref[...]


def _merge(proj, yb, x2, ln_g, ln_b, w_s, b_s_t, wa, wb, wo, g_ffn, w_r, b_r):
    T = x2.shape[0]
    tm = MERGE_TM
    const2 = lambda i: (0, 0)
    one = pl.Buffered(1)

    def colblk(width, off):
        return pl.BlockSpec((tm, width), lambda i: (i, off // width))

    return pl.pallas_call(
        _merge_body,
        grid=(T // tm,),
        in_specs=[
            colblk(A_WIDTH, OFF_U), colblk(A_WIDTH, OFF_V),
            colblk(A_WIDTH, OFF_GA), colblk(A_WIDTH, OFF_GA + A_WIDTH),
            colblk(A_WIDTH, OFF_GB), colblk(A_WIDTH, OFF_GB + A_WIDTH),
            pl.BlockSpec((tm, SB_WIDTH), lambda i: (i, 0)),
            pl.BlockSpec((tm, D_MODEL), lambda i: (i, 0)),
            pl.BlockSpec((1, A_WIDTH), const2), pl.BlockSpec((1, A_WIDTH), const2),
            pl.BlockSpec((A_GROUPS, CHUNK, CHUNK), lambda i: (0, 0, 0)),
            pl.BlockSpec((CHUNK, A_GROUPS), const2),
            pl.BlockSpec((A_WIDTH, D_MODEL), const2, pipeline_mode=one),
            pl.BlockSpec((SB_WIDTH, D_MODEL), const2, pipeline_mode=one),
            pl.BlockSpec((D_MODEL, D_MODEL), const2, pipeline_mode=one),
            pl.BlockSpec((1, D_MODEL), const2),
            pl.BlockSpec((D_MODEL, LANES), const2, pipeline_mode=one),
            pl.BlockSpec((1, LANES), const2),
        ],
        out_specs=[
            pl.BlockSpec((tm, D_MODEL), lambda i: (i, 0)),
            pl.BlockSpec((tm, D_MODEL), lambda i: (i, 0)),
            pl.BlockSpec((tm, LANES), lambda i: (i, 0)),
        ],
        out_shape=[
            jax.ShapeDtypeStruct((T, D_MODEL), F32),
            jax.ShapeDtypeStruct((T, D_MODEL), F32),
            jax.ShapeDtypeStruct((T, LANES), F32),
        ],
        scratch_shapes=[pltpu.VMEM((tm, A_WIDTH), BF16)],
        compiler_params=pltpu.CompilerParams(
            dimension_semantics=("arbitrary",), vmem_limit_bytes=VMEM_LIMIT),
    )(proj, proj, proj, proj, proj, proj, yb, x2, ln_g, ln_b, w_s, b_s_t, wa, wb, wo, g_ffn, w_r, b_r)


def _route_body(lg_ref, oi_ref, ow_ref, cnt_ref, carry_scr):
    i = pl.program_id(0)
    tb = lg_ref.shape[0]

    @pl.when(i == 0)
    def _():
        carry_scr[...] = jnp.zeros_like(carry_scr)

    lg = lg_ref[...]
    lane = lax.broadcasted_iota(I32, (tb, LANES), 1)
    neg = jnp.float32(-jnp.inf)
    big = jnp.int32(LANES)

    def first_max(mask):
        m = jnp.max(jnp.where(mask, lg, neg), axis=1, keepdims=True)
        idx = jnp.min(jnp.where(mask & (lg == m), lane, big), axis=1, keepdims=True)
        return m, idx

    gmask = lane < N_GROUPS
    gm, grp = first_max(gmask)
    pg_sel = 1.0 / jnp.sum(jnp.where(gmask, jnp.exp(lg - gm), 0.0), axis=1, keepdims=True)

    eid = lane - N_GROUPS
    emask = (eid >= 0) & (eid < N_EXPERTS) & ((eid // EXPERTS_PER_GROUP) == grp)
    m1, i1 = first_max(emask)
    emask2 = emask & (lane != i1)
    m2, i2 = first_max(emask2)
    t = jnp.exp(m2 - m1)
    p1 = 1.0 / (1.0 + t)
    p2 = t / (1.0 + t)
    e1 = i1 - N_GROUPS
    e2 = i2 - N_GROUPS

    sel1 = lane == e1
    sel2 = lane == e2
    onehot = (sel1 | sel2).astype(BF16)
    r = lax.broadcasted_iota(I32, (tb, tb), 0)
    c = lax.broadcasted_iota(I32, (tb, tb), 1)
    earlier = (c < r).astype(BF16)
    prefix = _dot(earlier, onehot) + carry_scr[...]
    rank1 = jnp.sum(jnp.where(sel1, prefix, 0.0), axis=1, keepdims=True).astype(I32)
    rank2 = jnp.sum(jnp.where(sel2, prefix, 0.0), axis=1, keepdims=True).astype(I32)
    carry_scr[...] = carry_scr[...] + jnp.sum(onehot.astype(F32), axis=0, keepdims=True)

    zero = jnp.zeros((tb, LANES), I32)
    oi_ref[...] = jnp.where(lane == 0, e1, jnp.where(lane == 1, e2,
                            jnp.where(lane == 2, rank1, jnp.where(lane == 3, rank2, zero))))
    ow_ref[...] = jnp.where(lane == 0, pg_sel * p1, jnp.where(lane == 1, pg_sel * p2, 0.0))
    cnt_ref[...] = carry_scr[...]


def _route(logits):
    T = logits.shape[0]
    tb = min(ROUTE_TB, T)
    return pl.pallas_call(
        _route_body,
        grid=(T // tb,),
        in_specs=[pl.BlockSpec((tb, LANES), lambda i: (i, 0))],
        out_specs=[
            pl.BlockSpec((tb, LANES), lambda i: (i, 0)),
            pl.BlockSpec((tb, LANES), lambda i: (i, 0)),
            pl.BlockSpec((1, LANES), lambda i: (0, 0)),
        ],
        out_shape=[
            jax.ShapeDtypeStruct((T, LANES), I32),
            jax.ShapeDtypeStruct((T, LANES), F32),
            jax.ShapeDtypeStruct((1, LANES), F32),
        ],
        scratch_shapes=[pltpu.VMEM((1, LANES), F32)],
        compiler_params=pltpu.CompilerParams(dimension_semantics=("arbitrary",)),
    )(logits)


def _dispatch_body(pos_ref, pend_ref, h2_ref, xs_ref, zbuf, sem, zsem):
    i = pl.program_id(0)
    tm = h2_ref.shape[0]

    def tail_copy(e):
        start = pl.multiple_of(pend_ref[e] - MOE_SB, MOE_SB)
        return pltpu.make_async_copy(zbuf, xs_ref.at[pl.ds(start, MOE_SB), :], zsem)

    @pl.when(i == 0)
    def _():
        zbuf[...] = jnp.zeros_like(zbuf)
        for e in range(N_EXPERTS):
            prev = pend_ref[e - 1] if e > 0 else 0

            @pl.when(pend_ref[e] > prev)
            def _():
                tail_copy(e).start()
        for e in range(N_EXPERTS):
            prev = pend_ref[e - 1] if e > 0 else 0

            @pl.when(pend_ref[e] > prev)
            def _():
                tail_copy(e).wait()

        used = pend_ref[N_EXPERTS - 1]
        n_slack = (xs_ref.shape[0] - used) // MOE_SB

        def slack_copy(b):
            start = pl.multiple_of(used + b * MOE_SB, MOE_SB)
            return pltpu.make_async_copy(zbuf, xs_ref.at[pl.ds(start, MOE_SB), :], zsem)

        def slack_start(b, _):
            slack_copy(b).start()
            return 0

        def slack_wait(b, _):
            slack_copy(b).wait()
            return 0

        lax.fori_loop(0, n_slack, slack_start, 0)
        lax.fori_loop(0, n_slack, slack_wait, 0)

    def row_copy(r, k):
        p = pos_ref[2 * (i * tm + r) + k]
        return pltpu.make_async_copy(h2_ref.at[pl.ds(r, 1), :], xs_ref.at[pl.ds(p, 1), :], sem)

    def issue(r, _):
        row_copy(r, 0).start(priority=0)
        row_copy(r, 1).start(priority=1)
        return 0

    def drain(r, _):
        row_copy(r, 0).wait()
        row_copy(r, 1).wait()
        return 0

    lax.fori_loop(0, tm, issue, 0, unroll=ROW_DMA_UNROLL)
    lax.fori_loop(0, tm, drain, 0, unroll=ROW_DMA_UNROLL)


def _dispatch(pos_flat, pends, h2, n_rows):
    T = h2.shape[0]
    tm = DISP_TM
    return pl.pallas_call(
        _dispatch_body,
        grid_spec=pltpu.PrefetchScalarGridSpec(
            num_scalar_prefetch=2,
            grid=(T // tm,),
            in_specs=[pl.BlockSpec((tm, D_MODEL), lambda i, pos, pe: (i, 0))],
            out_specs=pl.BlockSpec(memory_space=pl.ANY),
            scratch_shapes=[
                pltpu.VMEM((MOE_SB, D_MODEL), F32),
                pltpu.SemaphoreType.DMA(()),
                pltpu.SemaphoreType.DMA(()),
            ],
        ),
        out_shape=jax.ShapeDtypeStruct((n_rows, D_MODEL), F32),
        compiler_params=pltpu.CompilerParams(
            dimension_semantics=("arbitrary",), has_side_effects=True),
    )(pos_flat, pends, h2)


def _moe_body(ie_ref, is_ref, in_ref, meta_ref, xs_ref, wg_hbm, wu_hbm, wd_hbm, ys_ref,
              xbuf, acc, wgbuf, wubuf, wdbuf, sem_in, sem_out, sem_w):
    s = pl.program_id(0)
    h = pl.program_id(1)
    n_items = pl.num_programs(0)
    n_h = pl.num_programs(1)
    slot = s % 2
    th = MOE_TH
    used = meta_ref[0]
    n_steps = meta_ref[1] * n_h
    t = s * n_h + h

    def n_blocks(item):
        return (in_ref[item] + MOE_SB - 1) // MOE_SB

    def in_copy(item, b):
        sl = item % 2
        off = pl.multiple_of(b * MOE_SB, MOE_SB)
        src = pl.multiple_of(is_ref[item] + off, MOE_SB)
        return pltpu.make_async_copy(xs_ref.at[pl.ds(src, MOE_SB), :],
                                     xbuf.at[sl, pl.ds(off, MOE_SB), :], sem_in.at[sl])

    def out_copy(item, b):
        sl = item % 2
        off = pl.multiple_of(b * MOE_SB, MOE_SB)
        dst = pl.multiple_of(is_ref[item] + off, MOE_SB)
        return pltpu.make_async_copy(acc.at[sl, pl.ds(off, MOE_SB), :],
                                     ys_ref.at[pl.ds(dst, MOE_SB), :], sem_out.at[sl])

    def for_blocks(item, fn):
        def body(b, carry):
            fn(item, b)
            return carry
        lax.fori_loop(0, n_blocks(item), body, 0)

    def start_in(item, b):
        in_copy(item, b).start()

    def wait_in(item, b):
        in_copy(item, b).wait()

    def start_out(item, b):
        out_copy(item, b).start()

    def wait_out(item, b):
        out_copy(item, b).wait()

    def weight_copies(step):
        e = ie_ref[step // n_h]
        col = pl.multiple_of((step % n_h) * th, th)
        ws = step % MOE_WBUF
        half = th // 2
        return (
            pltpu.make_async_copy(wg_hbm.at[e, :, pl.ds(col, th)], wgbuf.at[ws], sem_w.at[ws, 0]),
            pltpu.make_async_copy(wu_hbm.at[e, :, pl.ds(col, th)], wubuf.at[ws], sem_w.at[ws, 1]),
            pltpu.make_async_copy(wd_hbm.at[e, pl.ds(col, half), :],
                                  wdbuf.at[ws, pl.ds(0, half), :], sem_w.at[ws, 2]),
            pltpu.make_async_copy(wd_hbm.at[e, pl.ds(col + half, half), :],
                                  wdbuf.at[ws, pl.ds(half, half), :], sem_w.at[ws, 3]),
        )

    def start_weights(step):
        for k, c in enumerate(weight_copies(step)):
            c.start(priority=k % 2)

    @pl.when(t == 0)
    def _():
        for d in range(MOE_WBUF - 1):
            @pl.when(d < n_steps)
            def _(d=d):
                start_weights(d)

        n_slack = (ys_ref.shape[0] - used) // MOE_SB
        acc[1, pl.ds(0, MOE_SB), :] = jnp.zeros((MOE_SB, D_MODEL), F32)

        def slack_copy(b):
            dst = pl.multiple_of(used + b * MOE_SB, MOE_SB)
            return pltpu.make_async_copy(acc.at[1, pl.ds(0, MOE_SB), :],
                                         ys_ref.at[pl.ds(dst, MOE_SB), :], sem_out.at[1])

        def slack_start(b, _):
            slack_copy(b).start()
            return 0

        def slack_wait(b, _):
            slack_copy(b).wait()
            return 0

        for_blocks(s, start_in)
        lax.fori_loop(0, n_slack, slack_start, 0)
        lax.fori_loop(0, n_slack, slack_wait, 0)

    @pl.when(t + MOE_WBUF - 1 < n_steps)
    def _():
        start_weights(t + MOE_WBUF - 1)

    @pl.when(h == 0)
    def _():
        for_blocks(s, wait_in)

    @pl.when((h == 1) & (s + 1 < n_items))
    def _():
        for_blocks(s + 1, start_in)

    @pl.when(t < n_steps)
    def _():
        for c in weight_copies(t):
            c.wait()

    ws = t % MOE_WBUF

    def chunk(off, size, first):
        off = pl.multiple_of(off, MOE_SB)
        xb = xbuf[slot, pl.ds(off, size), :]
        gate = _dot(xb, wgbuf[ws])
        up = _dot(xb, wubuf[ws])
        hidden = gate * jax.nn.sigmoid(gate) * up
        contrib = _dot(hidden, wdbuf[ws])
        if first:
            acc[slot, pl.ds(off, size), :] = contrib
        else:
            acc[slot, pl.ds(off, size), :] += contrib

    n_pad = n_blocks(s) * MOE_SB
    off = 0
    for size in MOE_CHUNKS:
        take = (n_pad & size) != 0
        for first in (True, False):
            @pl.when(take & ((h == 0) if first else (h > 0)))
            def _(off=off, size=size, first=first):
                chunk(off, size, first)

        off = off + jnp.where(take, size, 0)

    @pl.when(h == n_h - 1)
    def _():
        @pl.when(s > 0)
        def _():
            for_blocks(s - 1, wait_out)

        for_blocks(s, start_out)

        @pl.when(s == n_items - 1)
        def _():
            for_blocks(s, wait_out)


def _moe(item_e, item_start, item_n, meta, xs, w_gate, w_up, w_down):
    n_rows = xs.shape[0]
    n_items = item_e.shape[0]
    th = MOE_TH
    hbm = pl.BlockSpec(memory_space=pl.ANY)
    return pl.pallas_call(
        _moe_body,
        grid_spec=pltpu.PrefetchScalarGridSpec(
            num_scalar_prefetch=4,
            grid=(n_items, D_EXPERT // th),
            in_specs=[hbm, hbm, hbm, hbm],
            out_specs=hbm,
            scratch_shapes=[
                pltpu.VMEM((2, MOE_R, D_MODEL), F32),
                pltpu.VMEM((2, MOE_R, D_MODEL), F32),
                pltpu.VMEM((MOE_WBUF, D_MODEL, th), F32),
                pltpu.VMEM((MOE_WBUF, D_MODEL, th), F32),
                pltpu.VMEM((MOE_WBUF, th, D_MODEL), F32),
                pltpu.SemaphoreType.DMA((2,)),
                pltpu.SemaphoreType.DMA((2,)),
                pltpu.SemaphoreType.DMA((MOE_WBUF, 4)),
            ],
        ),
        out_shape=jax.ShapeDtypeStruct((n_rows, D_MODEL), F32),
        compiler_params=pltpu.CompilerParams(
            dimension_semantics=("arbitrary", "arbitrary"), vmem_limit_bytes=VMEM_LIMIT,
            has_side_effects=True),
    )(item_e, item_start, item_n, meta, xs, w_gate, w_up, w_down)


def _combine_body(pos_ref, x1_ref, gw_ref, gf_ref, ys_ref, o_ref, ybuf, sem):
    i = pl.program_id(0)
    tm = x1_ref.shape[0]
    slot = i % 2

    def row_copy(step, r, k):
        p = pos_ref[2 * (step * tm + r) + k]
        sl = step % 2
        return pltpu.make_async_copy(ys_ref.at[pl.ds(p, 1), :],
                                     ybuf.at[sl, k, pl.ds(r, 1), :], sem.at[sl])

    def issue(step):
        def body(r, _):
            row_copy(step, r, 0).start(priority=0)
            row_copy(step, r, 1).start(priority=1)
            return 0
        lax.fori_loop(0, tm, body, 0, unroll=ROW_DMA_UNROLL)

    def drain(step):
        def body(r, _):
            row_copy(step, r, 0).wait()
            row_copy(step, r, 1).wait()
            return 0
        lax.fori_loop(0, tm, body, 0, unroll=ROW_DMA_UNROLL)

    @pl.when(i == 0)
    def _():
        issue(i)

    @pl.when(i + 1 < pl.num_programs(0))
    def _():
        issue(i + 1)

    drain(i)

    gw = gw_ref[...]
    x = x1_ref[...] + gw[:, 0:1] * ybuf[slot, 0] + gw[:, 1:2] * ybuf[slot, 1]
    ms = jnp.mean(x * x, axis=-1, keepdims=True)
    o_ref[...] = x * lax.rsqrt(ms + EPS) * gf_ref[...]


def _combine(pos_flat, x1, gate_w, g_final, ys):
    T = x1.shape[0]
    tm = COMB_TM
    return pl.pallas_call(
        _combine_body,
        grid_spec=pltpu.PrefetchScalarGridSpec(
            num_scalar_prefetch=1,
            grid=(T // tm,),
            in_specs=[
                pl.BlockSpec((tm, D_MODEL), lambda i, pos: (i, 0)),
                pl.BlockSpec((tm, LANES), lambda i, pos: (i, 0)),
                pl.BlockSpec((1, D_MODEL), lambda i, pos: (0, 0)),
                pl.BlockSpec(memory_space=pl.ANY),
            ],
            out_specs=pl.BlockSpec((tm, D_MODEL), lambda i, pos: (i, 0)),
            scratch_shapes=[
                pltpu.VMEM((2, 2, tm, D_MODEL), F32),
                pltpu.SemaphoreType.DMA((2,)),
            ],
        ),
        out_shape=jax.ShapeDtypeStruct((T, D_MODEL), F32),
        compiler_params=pltpu.CompilerParams(
            dimension_semantics=("arbitrary",), vmem_limit_bytes=VMEM_LIMIT),
    )(pos_flat, x1, gate_w, g_final.reshape(1, D_MODEL), ys)


def _moe_schedule(counts, n_items):
    padded = ((counts + MOE_SB - 1) // MOE_SB) * MOE_SB
    pends = jnp.cumsum(padded)
    pstarts = pends - padded
    chunks = (counts + MOE_R - 1) // MOE_R
    cends = jnp.cumsum(chunks)
    slot = jnp.arange(n_items, dtype=I32)
    total = cends[-1]
    live = slot < total
    owner = jnp.sum((slot[:, None] >= cends[None, :]).astype(I32), axis=1)
    last_owner = jnp.sum((total - 1 >= cends).astype(I32))
    owner = jnp.where(live, owner, last_owner)
    owner = jnp.minimum(owner, N_EXPERTS - 1)
    cidx = slot - (cends - chunks)[owner]
    item_start = jnp.where(live, pstarts[owner] + cidx * MOE_R, 0)
    item_n = jnp.where(live, jnp.clip(counts[owner] - cidx * MOE_R, 0, MOE_R), 0)
    meta = jnp.stack([pends[-1], total]).astype(I32)
    return pstarts.astype(I32), pends.astype(I32), owner.astype(I32), item_start.astype(I32), \
        item_n.astype(I32), meta


def kernel(x, g_mix, w_in, ln_v_g, ln_v_b, w_spatial, b_spatial, w_branch_a, w_branch_b, w_out,
           g_ffn, w_router_group, b_router_group, w_router_expert, b_router_expert, w_gate, w_up,
           w_down, g_final):
    B, S, D = x.shape
    T = B * S
    cur = x.reshape(T, D)
    depth = g_mix.shape[0]
    for l in range(depth):
        proj = _proj(_norm(cur, g_mix[l]), w_in[l])
        yb = _attn(proj)
        w_r = jnp.zeros((D, LANES), F32)
        w_r = w_r.at[:, :N_GROUPS].set(w_router_group[l])
        w_r = w_r.at[:, N_GROUPS:N_GROUPS + N_EXPERTS].set(w_router_expert[l])
        b_r = jnp.zeros((1, LANES), F32)
        b_r = b_r.at[0, :N_GROUPS].set(b_router_group[l])
        b_r = b_r.at[0, N_GROUPS:N_GROUPS + N_EXPERTS].set(b_router_expert[l])
        x1, h2, logits = _merge(
            proj, yb, cur, ln_v_g[l].reshape(1, A_WIDTH), ln_v_b[l].reshape(1, A_WIDTH),
            w_spatial[l], b_spatial[l].T, w_branch_a[l].astype(BF16), w_branch_b[l].astype(BF16),
            w_out[l].astype(BF16), g_ffn[l].reshape(1, D), w_r, b_r)
        oi, gate_w, cnt = _route(logits)

        counts = cnt[0, :N_EXPERTS].astype(I32)
        n_rows = 2 * T + N_EXPERTS * MOE_SB
        n_items = N_EXPERTS + -(-(2 * T) // MOE_R)
        pstarts, pends, item_e, item_start, item_n, meta = _moe_schedule(counts, n_items)
        chosen = oi[:, 0:2, None] == jnp.arange(N_EXPERTS, dtype=I32)
        pos = (jnp.sum(jnp.where(chosen, pstarts, 0), axis=-1) + oi[:, 2:4]).reshape(2 * T)

        xs = _dispatch(pos, pends, h2, n_rows)
        ys = _moe(item_e, item_start, item_n, meta, xs, w_gate[l], w_up[l], w_down[l])
        if l + 1 < depth:
            raise NotImplementedError("only the final layer fuses the closing rms_norm")
        cur = _combine(pos, x1, gate_w, g_final, ys)
    return cur.reshape(B, S, D)
```

```python
import functools

import jax
import jax.numpy as jnp
from jax import lax
from jax.experimental import pallas as pl
from jax.experimental.pallas import tpu as pltpu

F32 = jnp.float32
BF16 = jnp.bfloat16
I32 = jnp.int32

D_MODEL = 2048
CHUNK = 128
A_GROUPS = 8
A_WIDTH = 1024
SB_HEADS = 8
SB_HEAD_DIM = 128
SB_WIDTH = 1024
N_GROUPS = 4
EXPERTS_PER_GROUP = 8
N_EXPERTS = 32
D_EXPERT = 1024
EPS = 1e-6
PROJ_WIDTH = 2 * A_WIDTH + 3 * SB_WIDTH + 2 * D_MODEL
LANES = 128
LOG2E = 1.4426950408889634
EXP2_ZERO_BELOW = -151.0

OFF_U, OFF_V = 0, A_WIDTH
OFF_Q = 2 * A_WIDTH
OFF_K = OFF_Q + SB_WIDTH
OFF_VV = OFF_K + SB_WIDTH
OFF_GA = OFF_VV + SB_WIDTH
OFF_GB = OFF_GA + D_MODEL

VMEM_LIMIT = 56 * 1024 * 1024

NORM_TM = 512
PROJ_TM, PROJ_TN = 1024, 256
ATT_TQ = 256
ATT_HEADS = 4
MERGE_TM = 256
ROUTE_TB = 512
DISP_TM = 512
ROW_DMA_UNROLL = 8
MOE_SB = 128
MOE_CHUNKS = (512, 256, 128)
MOE_R = 768
MOE_TH = 256
MOE_WBUF = 3
COMB_TM = 512


def _dot(a, b):
    return jnp.dot(a, b, preferred_element_type=F32)


def _norm_body(x_ref, g_ref, o_ref):
    x = x_ref[...]
    ms = jnp.mean(x * x, axis=-1, keepdims=True)
    o_ref[...] = (x * lax.rsqrt(ms + EPS) * g_ref[...]).astype(BF16)


def _norm(x2, g):
    T = x2.shape[0]
    tm = min(NORM_TM, T)
    return pl.pallas_call(
        _norm_body,
        grid=(T // tm,),
        in_specs=[pl.BlockSpec((tm, D_MODEL), lambda i: (i, 0)),
                  pl.BlockSpec((1, D_MODEL), lambda i: (0, 0))],
        out_specs=pl.BlockSpec((tm, D_MODEL), lambda i: (i, 0)),
        out_shape=jax.ShapeDtypeStruct((T, D_MODEL), BF16),
        compiler_params=pltpu.CompilerParams(dimension_semantics=("arbitrary",)),
    )(x2, g.reshape(1, D_MODEL))


def _proj_body(h_ref, w_ref, o_ref, *, n_gelu, n_plain, rows):
    j = pl.program_id(0)
    n_chunks = h_ref.shape[0] // rows

    def column_block(act):
        w = w_ref[...].astype(BF16)

        def chunk(c, carry):
            r = pl.ds(pl.multiple_of(c * rows, rows), rows)
            o_ref[r, :] = act(_dot(h_ref[r, :], w)).astype(BF16)
            return carry

        lax.fori_loop(0, n_chunks, chunk, 0)

    @pl.when(j < n_gelu)
    def _():
        column_block(jax.nn.gelu)

    @pl.when((j >= n_gelu) & (j < n_gelu + n_plain))
    def _():
        column_block(lambda v: v)

    @pl.when(j >= n_gelu + n_plain)
    def _():
        column_block(jax.nn.sigmoid)


def _proj(h, w_in):
    T = h.shape[0]
    tn = PROJ_TN
    body = functools.partial(_proj_body, n_gelu=OFF_Q // tn, n_plain=(OFF_GA - OFF_Q) // tn,
                             rows=min(PROJ_TM, T))
    return pl.pallas_call(
        body,
        grid=(PROJ_WIDTH // tn,),
        in_specs=[
            pl.BlockSpec((T, D_MODEL), lambda j: (0, 0), pipeline_mode=pl.Buffered(1)),
            pl.BlockSpec((D_MODEL, tn), lambda j: (0, j)),
        ],
        out_specs=pl.BlockSpec((T, tn), lambda j: (0, j)),
        out_shape=jax.ShapeDtypeStruct((T, PROJ_WIDTH), BF16),
        compiler_params=pltpu.CompilerParams(
            dimension_semantics=("arbitrary",), vmem_limit_bytes=VMEM_LIMIT),
    )(h, w_in)


def _attn_tile(q, k, v, upper2, carry, causal, live):
    z = lax.dot_general(q, k, (((1,), (1,)), ((), ())), preferred_element_type=F32)
    z = z * (SB_HEAD_DIM ** -0.5 * LOG2E)
    nz = -z
    soft = jnp.log2(1.0 + jnp.exp2(jnp.minimum(z, nz)))
    log_rem = jnp.minimum(nz, 0.0) - soft
    log_beta = z + log_rem
    if causal is not None:
        log_rem = jnp.where(causal, log_rem, 0.0)
    hi = log_rem.astype(BF16)
    lo = (log_rem - hi.astype(F32)).astype(BF16)
    suffix = _dot(jnp.concatenate([hi, lo], axis=1), upper2)
    w = jnp.exp2(log_beta + suffix + carry)
    if causal is not None:
        w = jnp.where(causal, w, 0.0)
    contrib = _dot(w.astype(BF16), v)
    total = jnp.sum(log_rem, axis=1, keepdims=True)
    if live is not None:
        contrib = contrib * live
        total = total * live
    return contrib, carry + total


def _attn_body(q_ref, k_ref, v_ref, o_ref):
    i = pl.program_id(1)
    t = ATT_TQ
    dh = SB_HEAD_DIM
    heads = range(ATT_HEADS)
    row = lax.broadcasted_iota(I32, (t, t), 0)
    col = lax.broadcasted_iota(I32, (t, t), 1)
    upper = (row > col).astype(BF16)
    upper2 = jnp.concatenate([upper, upper], axis=0)
    diagonal = col < row

    def tile(a, j, carry, causal=None, live=None):
        off = pl.multiple_of(jnp.maximum(j, 0) * t, t)
        hs = slice(a * dh, (a + 1) * dh)
        return _attn_tile(q_ref[:, hs], k_ref[pl.ds(off, t), hs], v_ref[pl.ds(off, t), hs],
                          upper2, carry, causal, live)

    def pair(j, accs, carries, causal):
        live = (j >= 1).astype(F32)
        new_accs, new_carries = [], []
        for a in heads:
            c1, carry = tile(a, j, carries[a], causal=causal)
            c2, carry = tile(a, j - 1, carry, live=live)
            new_accs.append(accs[a] + c1 + c2)
            new_carries.append(carry)
        return tuple(new_accs), tuple(new_carries)

    def cond(state):
        j, _, carries = state
        top = functools.reduce(jnp.maximum, carries)
        return (j >= 0) & (jnp.max(top) > EXP2_ZERO_BELOW)

    def step(state):
        j, accs, carries = state
        accs, carries = pair(j, accs, carries, None)
        return j - 2, accs, carries

    zero_acc = tuple(jnp.zeros((t, dh), F32) for _ in heads)
    zero_carry = tuple(jnp.zeros((t, 1), F32) for _ in heads)
    accs, carries = pair(i, zero_acc, zero_carry, diagonal)
    _, accs, _ = lax.while_loop(cond, step, (i - 2, accs, carries))
    for a in heads:
        o_ref[:, a * dh:(a + 1) * dh] = accs[a].astype(BF16)


def _attn(proj):
    T = proj.shape[0]
    width = ATT_HEADS * SB_HEAD_DIM
    qb, kb, vb = OFF_Q // width, OFF_K // width, OFF_VV // width
    return pl.pallas_call(
        _attn_body,
        grid=(SB_HEADS // ATT_HEADS, T // ATT_TQ),
        in_specs=[
            pl.BlockSpec((ATT_TQ, width), lambda h, i: (i, qb + h)),
            pl.BlockSpec((T, width), lambda h, i: (0, kb + h)),
            pl.BlockSpec((T, width), lambda h, i: (0, vb + h)),
        ],
        out_specs=pl.BlockSpec((ATT_TQ, width), lambda h, i: (i, h)),
        out_shape=jax.ShapeDtypeStruct((T, SB_WIDTH), BF16),
        compiler_params=pltpu.CompilerParams(
            dimension_semantics=("arbitrary", "arbitrary"), vmem_limit_bytes=VMEM_LIMIT),
    )(proj, proj, proj)


def _merge_body(u_ref, v_ref, ga0_ref, ga1_ref, gb0_ref, gb1_ref, yb_ref, x_ref, lng_ref, lnb_ref,
                ws_ref, bst_ref, wa_ref, wb_ref, wo_ref, gffn_ref, wr_ref, br_ref,
                x1_ref, h2_ref, lg_ref, ya_scr):
    tm = u_ref.shape[0]
    vf = v_ref[...].astype(F32)
    mu = jnp.mean(vf, axis=-1, keepdims=True)
    xc = vf - mu
    var = jnp.mean(xc * xc, axis=-1, keepdims=True)
    vln = (xc * lax.rsqrt(var + EPS) * lng_ref[...] + lnb_ref[...]).astype(BF16)

    row = lax.broadcasted_iota(I32, (CHUNK, CHUNK), 0)
    col = lax.broadcasted_iota(I32, (CHUNK, CHUNK), 1)
    tril = col <= row
    for g in range(A_GROUPS):
        wg = jnp.where(tril, ws_ref[g], 0.0).astype(BF16)
        bias = bst_ref[:, g:g + 1]
        cs = slice(g * LANES, (g + 1) * LANES)
        for c in range(tm // CHUNK):
            rs = slice(c * CHUNK, (c + 1) * CHUNK)
            mixed = _dot(wg, vln[rs, cs]) + bias
            ya_scr[rs, cs] = (u_ref[rs, cs].astype(F32) * mixed).astype(BF16)

    a = _dot(ya_scr[...], wa_ref[...])
    b = _dot(yb_ref[...], wb_ref[...])
    ga = jnp.concatenate([ga0_ref[...], ga1_ref[...]], axis=1).astype(F32)
    gb = jnp.concatenate([gb0_ref[...], gb1_ref[...]], axis=1).astype(F32)
    merged = (ga * a + gb * b).astype(BF16)
    x1 = x_ref[...] + _dot(merged, wo_ref[...])
    x1_ref[...] = x1
    ms = jnp.mean(x1 * x1, axis=-1, keepdims=True)
    h2 = x1 * lax.rsqrt(ms + EPS) * gffn_ref[...]
    h2_ref[...] = h2
    hi = h2.astype(BF16)
    lo = (h2 - hi.astype(F32)).astype(BF16)
    wr = wr_ref[...]
    whi = wr.astype(BF16)
    wlo = (wr - whi.astype(F32)).astype(BF16)
    lg_ref[...] = _dot(hi, whi) + _dot(hi, wlo) + _dot(lo, whi) + br_ref[...]


def _merge(proj, yb, x2, ln_g, ln_b, w_s, b_s_t, wa, wb, wo, g_ffn, w_r, b_r):
    T = x2.shape[0]
    tm = MERGE_TM
    const2 = lambda i: (0, 0)
    one = pl.Buffered(1)

    def colblk(width, off):
        return pl.BlockSpec((tm, width), lambda i: (i, off // width))

    return pl.pallas_call(
        _merge_body,
        grid=(T // tm,),
        in_specs=[
            colblk(A_WIDTH, OFF_U), colblk(A_WIDTH, OFF_V),
            colblk(A_WIDTH, OFF_GA), colblk(A_WIDTH, OFF_GA + A_WIDTH),
            colblk(A_WIDTH, OFF_GB), colblk(A_WIDTH, OFF_GB + A_WIDTH),
            pl.BlockSpec((tm, SB_WIDTH), lambda i: (i, 0)),
            pl.BlockSpec((tm, D_MODEL), lambda i: (i, 0)),
            pl.BlockSpec((1, A_WIDTH), const2), pl.BlockSpec((1, A_WIDTH), const2),
            pl.BlockSpec((A_GROUPS, CHUNK, CHUNK), lambda i: (0, 0, 0)),
            pl.BlockSpec((CHUNK, A_GROUPS), const2),
            pl.BlockSpec((A_WIDTH, D_MODEL), const2, pipeline_mode=one),
            pl.BlockSpec((SB_WIDTH, D_MODEL), const2, pipeline_mode=one),
            pl.BlockSpec((D_MODEL, D_MODEL), const2, pipeline_mode=one),
            pl.BlockSpec((1, D_MODEL), const2),
            pl.BlockSpec((D_MODEL, LANES), const2, pipeline_mode=one),
            pl.BlockSpec((1, LANES), const2),
        ],
        out_specs=[
            pl.BlockSpec((tm, D_MODEL), lambda i: (i, 0)),
            pl.BlockSpec((tm, D_MODEL), lambda i: (i, 0)),
            pl.BlockSpec((tm, LANES), lambda i: (i, 0)),
        ],
        out_shape=[
            jax.ShapeDtypeStruct((T, D_MODEL), F32),
            jax.ShapeDtypeStruct((T, D_MODEL), F32),
            jax.ShapeDtypeStruct((T, LANES), F32),
        ],
        scratch_shapes=[pltpu.VMEM((tm, A_WIDTH), BF16)],
        compiler_params=pltpu.CompilerParams(
            dimension_semantics=("arbitrary",), vmem_limit_bytes=VMEM_LIMIT),
    )(proj, proj, proj, proj, proj, proj, yb, x2, ln_g, ln_b, w_s, b_s_t, wa, wb, wo, g_ffn, w_r, b_r)


def _route_body(lg_ref, oi_ref, ow_ref, cnt_ref, carry_scr):
    i = pl.program_id(0)
    tb = lg_ref.shape[0]

    @pl.when(i == 0)
    def _():
        carry_scr[...] = jnp.zeros_like(carry_scr)

    lg = lg_ref[...]
    lane = lax.broadcasted_iota(I32, (tb, LANES), 1)
    neg = jnp.float32(-jnp.inf)
    big = jnp.int32(LANES)

    def first_max(mask):
        m = jnp.max(jnp.where(mask, lg, neg), axis=1, keepdims=True)
        idx = jnp.min(jnp.where(mask & (lg == m), lane, big), axis=1, keepdims=True)
        return m, idx

    gmask = lane < N_GROUPS
    gm, grp = first_max(gmask)
    pg_sel = 1.0 / jnp.sum(jnp.where(gmask, jnp.exp(lg - gm), 0.0), axis=1, keepdims=True)

    eid = lane - N_GROUPS
    emask = (eid >= 0) & (eid < N_EXPERTS) & ((eid // EXPERTS_PER_GROUP) == grp)
    m1, i1 = first_max(emask)
    emask2 = emask & (lane != i1)
    m2, i2 = first_max(emask2)
    t = jnp.exp(m2 - m1)
    p1 = 1.0 / (1.0 + t)
    p2 = t / (1.0 + t)
    e1 = i1 - N_GROUPS
    e2 = i2 - N_GROUPS

    sel1 = lane == e1
    sel2 = lane == e2
    onehot = (sel1 | sel2).astype(BF16)
    r = lax.broadcasted_iota(I32, (tb, tb), 0)
    c = lax.broadcasted_iota(I32, (tb, tb), 1)
    earlier = (c < r).astype(BF16)
    prefix = _dot(earlier, onehot) + carry_scr[...]
    rank1 = jnp.sum(jnp.where(sel1, prefix, 0.0), axis=1, keepdims=True).astype(I32)
    rank2 = jnp.sum(jnp.where(sel2, prefix, 0.0), axis=1, keepdims=True).astype(I32)
    carry_scr[...] = carry_scr[...] + jnp.sum(onehot.astype(F32), axis=0, keepdims=True)

    zero = jnp.zeros((tb, LANES), I32)
    oi_ref[...] = jnp.where(lane == 0, e1, jnp.where(lane == 1, e2,
                            jnp.where(lane == 2, rank1, jnp.where(lane == 3, rank2, zero))))
    ow_ref[...] = jnp.where(lane == 0, pg_sel * p1, jnp.where(lane == 1, pg_sel * p2, 0.0))
    cnt_ref[...] = carry_scr[...]


def _route(logits):
    T = logits.shape[0]
    tb = min(ROUTE_TB, T)
    return pl.pallas_call(
        _route_body,
        grid=(T // tb,),
        in_specs=[pl.BlockSpec((tb, LANES), lambda i: (i, 0))],
        out_specs=[
            pl.BlockSpec((tb, LANES), lambda i: (i, 0)),
            pl.BlockSpec((tb, LANES), lambda i: (i, 0)),
            pl.BlockSpec((1, LANES), lambda i: (0, 0)),
        ],
        out_shape=[
            jax.ShapeDtypeStruct((T, LANES), I32),
            jax.ShapeDtypeStruct((T, LANES), F32),
            jax.ShapeDtypeStruct((1, LANES), F32),
        ],
        scratch_shapes=[pltpu.VMEM((1, LANES), F32)],
        compiler_params=pltpu.CompilerParams(dimension_semantics=("arbitrary",)),
    )(logits)


def _dispatch_body(pos_ref, pend_ref, h2_ref, xs_ref, zbuf, sem, zsem):
    i = pl.program_id(0)
    tm = h2_ref.shape[0]

    def tail_copy(e):
        start = pl.multiple_of(pend_ref[e] - MOE_SB, MOE_SB)
        return pltpu.make_async_copy(zbuf, xs_ref.at[pl.ds(start, MOE_SB), :], zsem)

    @pl.when(i == 0)
    def _():
        zbuf[...] = jnp.zeros_like(zbuf)
        for e in range(N_EXPERTS):
            prev = pend_ref[e - 1] if e > 0 else 0

            @pl.when(pend_ref[e] > prev)
            def _():
                tail_copy(e).start()
        for e in range(N_EXPERTS):
            prev = pend_ref[e - 1] if e > 0 else 0

            @pl.when(pend_ref[e] > prev)
            def _():
                tail_copy(e).wait()

        used = pend_ref[N_EXPERTS - 1]
        n_slack = (xs_ref.shape[0] - used) // MOE_SB

        def slack_copy(b):
            start = pl.multiple_of(used + b * MOE_SB, MOE_SB)
            return pltpu.make_async_copy(zbuf, xs_ref.at[pl.ds(start, MOE_SB), :], zsem)

        def slack_start(b, _):
            slack_copy(b).start()
            return 0

        def slack_wait(b, _):
            slack_copy(b).wait()
            return 0

        lax.fori_loop(0, n_slack, slack_start, 0)
        lax.fori_loop(0, n_slack, slack_wait, 0)

    def row_copy(r, k):
        p = pos_ref[2 * (i * tm + r) + k]
        return pltpu.make_async_copy(h2_ref.at[pl.ds(r, 1), :], xs_ref.at[pl.ds(p, 1), :], sem)

    def issue(r, _):
        row_copy(r, 0).start(priority=0)
        row_copy(r, 1).start(priority=1)
        return 0

    def drain(r, _):
        row_copy(r, 0).wait()
        row_copy(r, 1).wait()
        return 0

    lax.fori_loop(0, tm, issue, 0, unroll=ROW_DMA_UNROLL)
    lax.fori_loop(0, tm, drain, 0, unroll=ROW_DMA_UNROLL)


def _dispatch(pos_flat, pends, h2, n_rows):
    T = h2.shape[0]
    tm = DISP_TM
    return pl.pallas_call(
        _dispatch_body,
        grid_spec=pltpu.PrefetchScalarGridSpec(
            num_scalar_prefetch=2,
            grid=(T // tm,),
            in_specs=[pl.BlockSpec((tm, D_MODEL), lambda i, pos, pe: (i, 0))],
            out_specs=pl.BlockSpec(memory_space=pl.ANY),
            scratch_shapes=[
                pltpu.VMEM((MOE_SB, D_MODEL), F32),
                pltpu.SemaphoreType.DMA(()),
                pltpu.SemaphoreType.DMA(()),
            ],
        ),
        out_shape=jax.ShapeDtypeStruct((n_rows, D_MODEL), F32),
        compiler_params=pltpu.CompilerParams(
            dimension_semantics=("arbitrary",), has_side_effects=True),
    )(pos_flat, pends, h2)


def _moe_body(ie_ref, is_ref, in_ref, meta_ref, tok_ref, h2_ref, wg_hbm, wu_hbm, wd_hbm, ys_ref,
              xbuf, acc, wgbuf, wubuf, wdbuf, sem_in, sem_out, sem_w):
    s = pl.program_id(0)
    h = pl.program_id(1)
    n_items = pl.num_programs(0)
    n_h = pl.num_programs(1)
    slot = s % 2
    th = MOE_TH
    used = meta_ref[0]
    n_steps = meta_ref[1] * n_h
    t = s * n_h + h

    def n_blocks(item):
        return (in_ref[item] + MOE_SB - 1) // MOE_SB

    def in_copy(item, r):
        sl = item % 2
        tok = tok_ref[is_ref[item] + r]
        return pltpu.make_async_copy(h2_ref.at[pl.ds(tok, 1), :],
                                     xbuf.at[sl, pl.ds(r, 1), :], sem_in.at[sl])

    def for_rows(item, fn):
        def body(g, carry):
            for u in range(ROW_DMA_UNROLL):
                fn(item, g * ROW_DMA_UNROLL + u, u)
            return carry
        lax.fori_loop(0, n_blocks(item) * (MOE_SB // ROW_DMA_UNROLL), body, 0)

    def out_copy(item, b):
        sl = item % 2
        off = pl.multiple_of(b * MOE_SB, MOE_SB)
        dst = pl.multiple_of(is_ref[item] + off, MOE_SB)
        return pltpu.make_async_copy(acc.at[sl, pl.ds(off, MOE_SB), :],
                                     ys_ref.at[pl.ds(dst, MOE_SB), :], sem_out.at[sl])

    def for_blocks(item, fn):
        def body(b, carry):
            fn(item, b)
            return carry
        lax.fori_loop(0, n_blocks(item), body, 0)

    def start_in(item, r, u):
        in_copy(item, r).start(priority=u % 2)

    def wait_in(item, r, u):
        in_copy(item, r).wait()

    def start_out(item, b):
        out_copy(item, b).start()

    def wait_out(item, b):
        out_copy(item, b).wait()

    def weight_copies(step):
        e = ie_ref[step // n_h]
        col = pl.multiple_of((step % n_h) * th, th)
        ws = step % MOE_WBUF
        half = th // 2
        return (
            pltpu.make_async_copy(wg_hbm.at[e, :, pl.ds(col, th)], wgbuf.at[ws], sem_w.at[ws, 0]),
            pltpu.make_async_copy(wu_hbm.at[e, :, pl.ds(col, th)], wubuf.at[ws], sem_w.at[ws, 1]),
            pltpu.make_async_copy(wd_hbm.at[e, pl.ds(col, half), :],
                                  wdbuf.at[ws, pl.ds(0, half), :], sem_w.at[ws, 2]),
            pltpu.make_async_copy(wd_hbm.at[e, pl.ds(col + half, half), :],
                                  wdbuf.at[ws, pl.ds(half, half), :], sem_w.at[ws, 3]),
        )

    def start_weights(step):
        for k, c in enumerate(weight_copies(step)):
            c.start(priority=k % 2)

    @pl.when(t == 0)
    def _():
        for d in range(MOE_WBUF - 1):
            @pl.when(d < n_steps)
            def _(d=d):
                start_weights(d)

        n_slack = (ys_ref.shape[0] - used) // MOE_SB
        acc[1, pl.ds(0, MOE_SB), :] = jnp.zeros((MOE_SB, D_MODEL), F32)

        def slack_copy(b):
            dst = pl.multiple_of(used + b * MOE_SB, MOE_SB)
            return pltpu.make_async_copy(acc.at[1, pl.ds(0, MOE_SB), :],
                                         ys_ref.at[pl.ds(dst, MOE_SB), :], sem_out.at[1])

        def slack_start(b, _):
            slack_copy(b).start()
            return 0

        def slack_wait(b, _):
            slack_copy(b).wait()
            return 0

        for_rows(s, start_in)
        lax.fori_loop(0, n_slack, slack_start, 0)
        lax.fori_loop(0, n_slack, slack_wait, 0)

    @pl.when(t + MOE_WBUF - 1 < n_steps)
    def _():
        start_weights(t + MOE_WBUF - 1)

    @pl.when(h == 0)
    def _():
        for_rows(s, wait_in)

    @pl.when((h == 1) & (s + 1 < n_items))
    def _():
        for_rows(s + 1, start_in)

    @pl.when(t < n_steps)
    def _():
        for c in weight_copies(t):
            c.wait()

    ws = t % MOE_WBUF

    def chunk(off, size, first):
        off = pl.multiple_of(off, MOE_SB)
        xb = xbuf[slot, pl.ds(off, size), :]
        gate = _dot(xb, wgbuf[ws])
        up = _dot(xb, wubuf[ws])
        hidden = gate * jax.nn.sigmoid(gate) * up
        contrib = _dot(hidden, wdbuf[ws])
        if first:
            acc[slot, pl.ds(off, size), :] = contrib
        else:
            acc[slot, pl.ds(off, size), :] += contrib

    n_pad = n_blocks(s) * MOE_SB
    off = 0
    for size in MOE_CHUNKS:
        take = (n_pad & size) != 0
        for first in (True, False):
            @pl.when(take & ((h == 0) if first else (h > 0)))
            def _(off=off, size=size, first=first):
                chunk(off, size, first)

        off = off + jnp.where(take, size, 0)

    @pl.when(h == n_h - 1)
    def _():
        @pl.when(s > 0)
        def _():
            for_blocks(s - 1, wait_out)

        for_blocks(s, start_out)

        @pl.when(s == n_items - 1)
        def _():
            for_blocks(s, wait_out)


def _moe(item_e, item_start, item_n, meta, row_tok, h2, w_gate, w_up, w_down):
    n_rows = row_tok.shape[0]
    n_items = item_e.shape[0]
    th = MOE_TH
    hbm = pl.BlockSpec(memory_space=pl.ANY)
    return pl.pallas_call(
        _moe_body,
        grid_spec=pltpu.PrefetchScalarGridSpec(
            num_scalar_prefetch=5,
            grid=(n_items, D_EXPERT // th),
            in_specs=[hbm, hbm, hbm, hbm],
            out_specs=hbm,
            scratch_shapes=[
                pltpu.VMEM((2, MOE_R, D_MODEL), F32),
                pltpu.VMEM((2, MOE_R, D_MODEL), F32),
                pltpu.VMEM((MOE_WBUF, D_MODEL, th), F32),
                pltpu.VMEM((MOE_WBUF, D_MODEL, th), F32),
                pltpu.VMEM((MOE_WBUF, th, D_MODEL), F32),
                pltpu.SemaphoreType.DMA((2,)),
                pltpu.SemaphoreType.DMA((2,)),
                pltpu.SemaphoreType.DMA((MOE_WBUF, 4)),
            ],
        ),
        out_shape=jax.ShapeDtypeStruct((n_rows, D_MODEL), F32),
        compiler_params=pltpu.CompilerParams(
            dimension_semantics=("arbitrary", "arbitrary"), vmem_limit_bytes=VMEM_LIMIT,
            has_side_effects=True),
    )(item_e, item_start, item_n, meta, row_tok, h2, w_gate, w_up, w_down)


def _combine_body(pos_ref, x1_ref, gw_ref, gf_ref, ys_ref, o_ref, ybuf, sem):
    i = pl.program_id(0)
    tm = x1_ref.shape[0]
    slot = i % 2

    def row_copy(step, r, k):
        p = pos_ref[2 * (step * tm + r) + k]
        sl = step % 2
        return pltpu.make_async_copy(ys_ref.at[pl.ds(p, 1), :],
                                     ybuf.at[sl, k, pl.ds(r, 1), :], sem.at[sl])

    def issue(step):
        def body(r, _):
            row_copy(step, r, 0).start(priority=0)
            row_copy(step, r, 1).start(priority=1)
            return 0
        lax.fori_loop(0, tm, body, 0, unroll=ROW_DMA_UNROLL)

    def drain(step):
        def body(r, _):
            row_copy(step, r, 0).wait()
            row_copy(step, r, 1).wait()
            return 0
        lax.fori_loop(0, tm, body, 0, unroll=ROW_DMA_UNROLL)

    @pl.when(i == 0)
    def _():
        issue(i)

    @pl.when(i + 1 < pl.num_programs(0))
    def _():
        issue(i + 1)

    drain(i)

    gw = gw_ref[...]
    x = x1_ref[...] + gw[:, 0:1] * ybuf[slot, 0] + gw[:, 1:2] * ybuf[slot, 1]
    ms = jnp.mean(x * x, axis=-1, keepdims=True)
    o_ref[...] = x * lax.rsqrt(ms + EPS) * gf_ref[...]


def _combine(pos_flat, x1, gate_w, g_final, ys):
    T = x1.shape[0]
    tm = COMB_TM
    return pl.pallas_call(
        _combine_body,
        grid_spec=pltpu.PrefetchScalarGridSpec(
            num_scalar_prefetch=1,
            grid=(T // tm,),
            in_specs=[
                pl.BlockSpec((tm, D_MODEL), lambda i, pos: (i, 0)),
                pl.BlockSpec((tm, LANES), lambda i, pos: (i, 0)),
                pl.BlockSpec((1, D_MODEL), lambda i, pos: (0, 0)),
                pl.BlockSpec(memory_space=pl.ANY),
            ],
            out_specs=pl.BlockSpec((tm, D_MODEL), lambda i, pos: (i, 0)),
            scratch_shapes=[
                pltpu.VMEM((2, 2, tm, D_MODEL), F32),
                pltpu.SemaphoreType.DMA((2,)),
            ],
        ),
        out_shape=jax.ShapeDtypeStruct((T, D_MODEL), F32),
        compiler_params=pltpu.CompilerParams(
            dimension_semantics=("arbitrary",), vmem_limit_bytes=VMEM_LIMIT),
    )(pos_flat, x1, gate_w, g_final.reshape(1, D_MODEL), ys)


def _moe_schedule(counts, n_items):
    padded = ((counts + MOE_SB - 1) // MOE_SB) * MOE_SB
    pends = jnp.cumsum(padded)
    pstarts = pends - padded
    chunks = (counts + MOE_R - 1) // MOE_R
    cends = jnp.cumsum(chunks)
    slot = jnp.arange(n_items, dtype=I32)
    total = cends[-1]
    live = slot < total
    owner = jnp.sum((slot[:, None] >= cends[None, :]).astype(I32), axis=1)
    last_owner = jnp.sum((total - 1 >= cends).astype(I32))
    owner = jnp.where(live, owner, last_owner)
    owner = jnp.minimum(owner, N_EXPERTS - 1)
    cidx = slot - (cends - chunks)[owner]
    item_start = jnp.where(live, pstarts[owner] + cidx * MOE_R, 0)
    item_n = jnp.where(live, jnp.clip(counts[owner] - cidx * MOE_R, 0, MOE_R), 0)
    meta = jnp.stack([pends[-1], total]).astype(I32)
    return pstarts.astype(I32), pends.astype(I32), owner.astype(I32), item_start.astype(I32), \
        item_n.astype(I32), meta


def kernel(x, g_mix, w_in, ln_v_g, ln_v_b, w_spatial, b_spatial, w_branch_a, w_branch_b, w_out,
           g_ffn, w_router_group, b_router_group, w_router_expert, b_router_expert, w_gate, w_up,
           w_down, g_final):
    B, S, D = x.shape
    T = B * S
    cur = x.reshape(T, D)
    depth = g_mix.shape[0]
    for l in range(depth):
        proj = _proj(_norm(cur, g_mix[l]), w_in[l])
        yb = _attn(proj)
        w_r = jnp.zeros((D, LANES), F32)
        w_r = w_r.at[:, :N_GROUPS].set(w_router_group[l])
        w_r = w_r.at[:, N_GROUPS:N_GROUPS + N_EXPERTS].set(w_router_expert[l])
        b_r = jnp.zeros((1, LANES), F32)
        b_r = b_r.at[0, :N_GROUPS].set(b_router_group[l])
        b_r = b_r.at[0, N_GROUPS:N_GROUPS + N_EXPERTS].set(b_router_expert[l])
        x1, h2, logits = _merge(
            proj, yb, cur, ln_v_g[l].reshape(1, A_WIDTH), ln_v_b[l].reshape(1, A_WIDTH),
            w_spatial[l], b_spatial[l].T, w_branch_a[l].astype(BF16), w_branch_b[l].astype(BF16),
            w_out[l].astype(BF16), g_ffn[l].reshape(1, D), w_r, b_r)
        oi, gate_w, cnt = _route(logits)

        counts = cnt[0, :N_EXPERTS].astype(I32)
        n_rows = 2 * T + N_EXPERTS * MOE_SB
        n_items = N_EXPERTS + -(-(2 * T) // MOE_R)
        pstarts, pends, item_e, item_start, item_n, meta = _moe_schedule(counts, n_items)
        chosen = oi[:, 0:2, None] == jnp.arange(N_EXPERTS, dtype=I32)
        pos = (jnp.sum(jnp.where(chosen, pstarts, 0), axis=-1) + oi[:, 2:4]).reshape(2 * T)

        row_tok = jnp.zeros((n_rows,), I32).at[pos].set(jnp.arange(2 * T, dtype=I32) // 2)
        ys = _moe(item_e, item_start, item_n, meta, row_tok, h2, w_gate[l], w_up[l], w_down[l])
        if l + 1 < depth:
            raise NotImplementedError("only the final layer fuses the closing rms_norm")
        cur = _combine(pos, x1, gate_w, g_final, ys)
    return cur.reshape(B, S, D)
```

```python
import functools

import jax
import jax.numpy as jnp
from jax import lax
from jax.experimental import pallas as pl
from jax.experimental.pallas import tpu as pltpu

F32 = jnp.float32
BF16 = jnp.bfloat16
I32 = jnp.int32

D_MODEL = 2048
CHUNK = 128
A_GROUPS = 8
A_WIDTH = 1024
SB_HEADS = 8
SB_HEAD_DIM = 128
SB_WIDTH = 1024
N_GROUPS = 4
EXPERTS_PER_GROUP = 8
N_EXPERTS = 32
D_EXPERT = 1024
EPS = 1e-6
PROJ_WIDTH = 2 * A_WIDTH + 3 * SB_WIDTH + 2 * D_MODEL
LANES = 128
LOG2E = 1.4426950408889634
EXP2_ZERO_BELOW = -151.0

OFF_U, OFF_V = 0, A_WIDTH
OFF_Q = 2 * A_WIDTH
OFF_K = OFF_Q + SB_WIDTH
OFF_VV = OFF_K + SB_WIDTH
OFF_GA = OFF_VV + SB_WIDTH
OFF_GB = OFF_GA + D_MODEL

VMEM_LIMIT = 56 * 1024 * 1024

NORM_TM = 512
PROJ_TM, PROJ_TN = 1024, 256
ATT_TQ = 256
ATT_HEADS = 4
MERGE_TM = 256
ROUTE_TB = 1024
DISP_TM = 512
MOE_SB = 128
MOE_CHUNKS = (512, 256, 128)
MOE_R = 768
MOE_TH = 256
MOE_WBUF = 3
COMB_TM = 512


def _dot(a, b):
    return jnp.dot(a, b, preferred_element_type=F32)


def _norm_body(x_ref, g_ref, o_ref):
    x = x_ref[...]
    ms = jnp.mean(x * x, axis=-1, keepdims=True)
    o_ref[...] = (x * lax.rsqrt(ms + EPS) * g_ref[...]).astype(BF16)


def _norm(x2, g):
    T = x2.shape[0]
    tm = min(NORM_TM, T)
    return pl.pallas_call(
        _norm_body,
        grid=(T // tm,),
        in_specs=[pl.BlockSpec((tm, D_MODEL), lambda i: (i, 0)),
                  pl.BlockSpec((1, D_MODEL), lambda i: (0, 0))],
        out_specs=pl.BlockSpec((tm, D_MODEL), lambda i: (i, 0)),
        out_shape=jax.ShapeDtypeStruct((T, D_MODEL), BF16),
        compiler_params=pltpu.CompilerParams(dimension_semantics=("arbitrary",)),
    )(x2, g.reshape(1, D_MODEL))


def _proj_body(h_ref, w_ref, o_ref, *, n_gelu, n_plain, rows):
    j = pl.program_id(0)
    n_chunks = h_ref.shape[0] // rows

    def column_block(act):
        w = w_ref[...].astype(BF16)

        def chunk(c, carry):
            r = pl.ds(pl.multiple_of(c * rows, rows), rows)
            o_ref[r, :] = act(_dot(h_ref[r, :], w)).astype(BF16)
            return carry

        lax.fori_loop(0, n_chunks, chunk, 0)

    @pl.when(j < n_gelu)
    def _():
        column_block(jax.nn.gelu)

    @pl.when((j >= n_gelu) & (j < n_gelu + n_plain))
    def _():
        column_block(lambda v: v)

    @pl.when(j >= n_gelu + n_plain)
    def _():
        column_block(jax.nn.sigmoid)


def _proj(h, w_in):
    T = h.shape[0]
    tn = PROJ_TN
    body = functools.partial(_proj_body, n_gelu=OFF_Q // tn, n_plain=(OFF_GA - OFF_Q) // tn,
                             rows=min(PROJ_TM, T))
    return pl.pallas_call(
        body,
        grid=(PROJ_WIDTH // tn,),
        in_specs=[
            pl.BlockSpec((T, D_MODEL), lambda j: (0, 0), pipeline_mode=pl.Buffered(1)),
            pl.BlockSpec((D_MODEL, tn), lambda j: (0, j)),
        ],
        out_specs=pl.BlockSpec((T, tn), lambda j: (0, j)),
        out_shape=jax.ShapeDtypeStruct((T, PROJ_WIDTH), BF16),
        compiler_params=pltpu.CompilerParams(
            dimension_semantics=("arbitrary",), vmem_limit_bytes=VMEM_LIMIT),
    )(h, w_in)


def _attn_tile(q, k, v, upper2, carry, causal, live):
    z = lax.dot_general(q, k, (((1,), (1,)), ((), ())), preferred_element_type=F32)
    z = z * (SB_HEAD_DIM ** -0.5 * LOG2E)
    nz = -z
    soft = jnp.log2(1.0 + jnp.exp2(jnp.minimum(z, nz)))
    log_rem = jnp.minimum(nz, 0.0) - soft
    log_beta = z + log_rem
    if causal is not None:
        log_rem = jnp.where(causal, log_rem, 0.0)
    hi = log_rem.astype(BF16)
    lo = (log_rem - hi.astype(F32)).astype(BF16)
    suffix = _dot(jnp.concatenate([hi, lo], axis=1), upper2)
    w = jnp.exp2(log_beta + suffix + carry)
    if causal is not None:
        w = jnp.where(causal, w, 0.0)
    contrib = _dot(w.astype(BF16), v)
    total = jnp.sum(log_rem, axis=1, keepdims=True)
    if live is not None:
        contrib = contrib * live
        total = total * live
    return contrib, carry + total


def _attn_body(q_ref, k_ref, v_ref, o_ref):
    i = pl.program_id(1)
    t = ATT_TQ
    dh = SB_HEAD_DIM
    heads = range(ATT_HEADS)
    row = lax.broadcasted_iota(I32, (t, t), 0)
    col = lax.broadcasted_iota(I32, (t, t), 1)
    upper = (row > col).astype(BF16)
    upper2 = jnp.concatenate([upper, upper], axis=0)
    diagonal = col < row

    def tile(a, j, carry, causal=None, live=None):
        off = pl.multiple_of(jnp.maximum(j, 0) * t, t)
        hs = slice(a * dh, (a + 1) * dh)
        return _attn_tile(q_ref[:, hs], k_ref[pl.ds(off, t), hs], v_ref[pl.ds(off, t), hs],
                          upper2, carry, causal, live)

    def pair(j, accs, carries, causal):
        live = (j >= 1).astype(F32)
        new_accs, new_carries = [], []
        for a in heads:
            c1, carry = tile(a, j, carries[a], causal=causal)
            c2, carry = tile(a, j - 1, carry, live=live)
            new_accs.append(accs[a] + c1 + c2)
            new_carries.append(carry)
        return tuple(new_accs), tuple(new_carries)

    def cond(state):
        j, _, carries = state
        top = functools.reduce(jnp.maximum, carries)
        return (j >= 0) & (jnp.max(top) > EXP2_ZERO_BELOW)

    def step(state):
        j, accs, carries = state
        accs, carries = pair(j, accs, carries, None)
        return j - 2, accs, carries

    zero_acc = tuple(jnp.zeros((t, dh), F32) for _ in heads)
    zero_carry = tuple(jnp.zeros((t, 1), F32) for _ in heads)
    accs, carries = pair(i, zero_acc, zero_carry, diagonal)
    _, accs, _ = lax.while_loop(cond, step, (i - 2, accs, carries))
    for a in heads:
        o_ref[:, a * dh:(a + 1) * dh] = accs[a].astype(BF16)


def _attn(proj):
    T = proj.shape[0]
    width = ATT_HEADS * SB_HEAD_DIM
    qb, kb, vb = OFF_Q // width, OFF_K // width, OFF_VV // width
    return pl.pallas_call(
        _attn_body,
        grid=(SB_HEADS // ATT_HEADS, T // ATT_TQ),
        in_specs=[
            pl.BlockSpec((ATT_TQ, width), lambda h, i: (i, qb + h)),
            pl.BlockSpec((T, width), lambda h, i: (0, kb + h)),
            pl.BlockSpec((T, width), lambda h, i: (0, vb + h)),
        ],
        out_specs=pl.BlockSpec((ATT_TQ, width), lambda h, i: (i, h)),
        out_shape=jax.ShapeDtypeStruct((T, SB_WIDTH), BF16),
        compiler_params=pltpu.CompilerParams(
            dimension_semantics=("arbitrary", "arbitrary"), vmem_limit_bytes=VMEM_LIMIT),
    )(proj, proj, proj)


def _merge_body(u_ref, v_ref, ga0_ref, ga1_ref, gb0_ref, gb1_ref, yb_ref, x_ref, lng_ref, lnb_ref,
                ws_ref, bst_ref, wa_ref, wb_ref, wo_ref, gffn_ref, wr_ref, br_ref,
                x1_ref, h2_ref, lg_ref, ya_scr):
    tm = u_ref.shape[0]
    vf = v_ref[...].astype(F32)
    mu = jnp.mean(vf, axis=-1, keepdims=True)
    xc = vf - mu
    var = jnp.mean(xc * xc, axis=-1, keepdims=True)
    vln = (xc * lax.rsqrt(var + EPS) * lng_ref[...] + lnb_ref[...]).astype(BF16)

    row = lax.broadcasted_iota(I32, (CHUNK, CHUNK), 0)
    col = lax.broadcasted_iota(I32, (CHUNK, CHUNK), 1)
    tril = col <= row
    for g in range(A_GROUPS):
        wg = jnp.where(tril, ws_ref[g], 0.0).astype(BF16)
        bias = bst_ref[:, g:g + 1]
        cs = slice(g * LANES, (g + 1) * LANES)
        for c in range(tm // CHUNK):
            rs = slice(c * CHUNK, (c + 1) * CHUNK)
            mixed = _dot(wg, vln[rs, cs]) + bias
            ya_scr[rs, cs] = (u_ref[rs, cs].astype(F32) * mixed).astype(BF16)

    a = _dot(ya_scr[...], wa_ref[...])
    b = _dot(yb_ref[...], wb_ref[...])
    ga = jnp.concatenate([ga0_ref[...], ga1_ref[...]], axis=1).astype(F32)
    gb = jnp.concatenate([gb0_ref[...], gb1_ref[...]], axis=1).astype(F32)
    merged = (ga * a + gb * b).astype(BF16)
    x1 = x_ref[...] + _dot(merged, wo_ref[...])
    x1_ref[...] = x1
    ms = jnp.mean(x1 * x1, axis=-1, keepdims=True)
    h2 = x1 * lax.rsqrt(ms + EPS) * gffn_ref[...]
    h2_ref[...] = h2
    hi = h2.astype(BF16)
    lo = (h2 - hi.astype(F32)).astype(BF16)
    wr = wr_ref[...]
    whi = wr.astype(BF16)
    wlo = (wr - whi.astype(F32)).astype(BF16)
    lg_ref[...] = _dot(hi, whi) + _dot(hi, wlo) + _dot(lo, whi) + br_ref[...]


def _merge(proj, yb, x2, ln_g, ln_b, w_s, b_s_t, wa, wb, wo, g_ffn, w_r, b_r):
    T = x2.shape[0]
    tm = MERGE_TM
    const2 = lambda i: (0, 0)
    one = pl.Buffered(1)

    def colblk(width, off):
        return pl.BlockSpec((tm, width), lambda i: (i, off // width))

    return pl.pallas_call(
        _merge_body,
        grid=(T // tm,),
        in_specs=[
            colblk(A_WIDTH, OFF_U), colblk(A_WIDTH, OFF_V),
            colblk(A_WIDTH, OFF_GA), colblk(A_WIDTH, OFF_GA + A_WIDTH),
            colblk(A_WIDTH, OFF_GB), colblk(A_WIDTH, OFF_GB + A_WIDTH),
            pl.BlockSpec((tm, SB_WIDTH), lambda i: (i, 0)),
            pl.BlockSpec((tm, D_MODEL), lambda i: (i, 0)),
            pl.BlockSpec((1, A_WIDTH), const2), pl.BlockSpec((1, A_WIDTH), const2),
            pl.BlockSpec((A_GROUPS, CHUNK, CHUNK), lambda i: (0, 0, 0)),
            pl.BlockSpec((CHUNK, A_GROUPS), const2),
            pl.BlockSpec((A_WIDTH, D_MODEL), const2, pipeline_mode=one),
            pl.BlockSpec((SB_WIDTH, D_MODEL), const2, pipeline_mode=one),
            pl.BlockSpec((D_MODEL, D_MODEL), const2, pipeline_mode=one),
            pl.BlockSpec((1, D_MODEL), const2),
            pl.BlockSpec((D_MODEL, LANES), const2, pipeline_mode=one),
            pl.BlockSpec((1, LANES), const2),
        ],
        out_specs=[
            pl.BlockSpec((tm, D_MODEL), lambda i: (i, 0)),
            pl.BlockSpec((tm, D_MODEL), lambda i: (i, 0)),
            pl.BlockSpec((tm, LANES), lambda i: (i, 0)),
        ],
        out_shape=[
            jax.ShapeDtypeStruct((T, D_MODEL), F32),
            jax.ShapeDtypeStruct((T, D_MODEL), F32),
            jax.ShapeDtypeStruct((T, LANES), F32),
        ],
        scratch_shapes=[pltpu.VMEM((tm, A_WIDTH), BF16)],
        compiler_params=pltpu.CompilerParams(
            dimension_semantics=("arbitrary",), vmem_limit_bytes=VMEM_LIMIT),
    )(proj, proj, proj, proj, proj, proj, yb, x2, ln_g, ln_b, w_s, b_s_t, wa, wb, wo, g_ffn, w_r, b_r)


def _route_body(lg_ref, oi_ref, ow_ref, cnt_ref, carry_scr):
    i = pl.program_id(0)
    tb = lg_ref.shape[0]

    @pl.when(i == 0)
    def _():
        carry_scr[...] = jnp.zeros_like(carry_scr)

    lg = lg_ref[...]
    lane = lax.broadcasted_iota(I32, (tb, LANES), 1)
    neg = jnp.float32(-jnp.inf)
    big = jnp.int32(LANES)

    def first_max(mask):
        m = jnp.max(jnp.where(mask, lg, neg), axis=1, keepdims=True)
        idx = jnp.min(jnp.where(mask & (lg == m), lane, big), axis=1, keepdims=True)
        return m, idx

    gmask = lane < N_GROUPS
    gm, grp = first_max(gmask)
    pg_sel = 1.0 / jnp.sum(jnp.where(gmask, jnp.exp(lg - gm), 0.0), axis=1, keepdims=True)

    eid = lane - N_GROUPS
    emask = (eid >= 0) & (eid < N_EXPERTS) & ((eid // EXPERTS_PER_GROUP) == grp)
    m1, i1 = first_max(emask)
    emask2 = emask & (lane != i1)
    m2, i2 = first_max(emask2)
    t = jnp.exp(m2 - m1)
    p1 = 1.0 / (1.0 + t)
    p2 = t / (1.0 + t)
    e1 = i1 - N_GROUPS
    e2 = i2 - N_GROUPS

    sel1 = lane == e1
    sel2 = lane == e2
    onehot = (sel1 | sel2).astype(BF16)
    r = lax.broadcasted_iota(I32, (tb, tb), 0)
    c = lax.broadcasted_iota(I32, (tb, tb), 1)
    earlier = (c < r).astype(BF16)
    prefix = _dot(earlier, onehot) + carry_scr[...]
    rank1 = jnp.sum(jnp.where(sel1, prefix, 0.0), axis=1, keepdims=True).astype(I32)
    rank2 = jnp.sum(jnp.where(sel2, prefix, 0.0), axis=1, keepdims=True).astype(I32)
    carry_scr[...] = carry_scr[...] + jnp.sum(onehot.astype(F32), axis=0, keepdims=True)

    zero = jnp.zeros((tb, LANES), I32)
    oi_ref[...] = jnp.where(lane == 0, e1, jnp.where(lane == 1, e2,
                            jnp.where(lane == 2, rank1, jnp.where(lane == 3, rank2, zero))))
    ow_ref[...] = jnp.where(lane == 0, pg_sel * p1, jnp.where(lane == 1, pg_sel * p2, 0.0))
    cnt_ref[...] = carry_scr[...]


def _route(logits):
    T = logits.shape[0]
    tb = min(ROUTE_TB, T)
    return pl.pallas_call(
        _route_body,
        grid=(T // tb,),
        in_specs=[pl.BlockSpec((tb, LANES), lambda i: (i, 0))],
        out_specs=[
            pl.BlockSpec((tb, LANES), lambda i: (i, 0)),
            pl.BlockSpec((tb, LANES), lambda i: (i, 0)),
            pl.BlockSpec((1, LANES), lambda i: (0, 0)),
        ],
        out_shape=[
            jax.ShapeDtypeStruct((T, LANES), I32),
            jax.ShapeDtypeStruct((T, LANES), F32),
            jax.ShapeDtypeStruct((1, LANES), F32),
        ],
        scratch_shapes=[pltpu.VMEM((1, LANES), F32)],
        compiler_params=pltpu.CompilerParams(dimension_semantics=("arbitrary",)),
    )(logits)


def _dispatch_body(pos_ref, pend_ref, h2_ref, xs_ref, zbuf, sem, zsem):
    i = pl.program_id(0)
    tm = h2_ref.shape[0]

    def tail_copy(e):
        start = pl.multiple_of(pend_ref[e] - MOE_SB, MOE_SB)
        return pltpu.make_async_copy(zbuf, xs_ref.at[pl.ds(start, MOE_SB), :], zsem)

    @pl.when(i == 0)
    def _():
        zbuf[...] = jnp.zeros_like(zbuf)
        for e in range(N_EXPERTS):
            prev = pend_ref[e - 1] if e > 0 else 0

            @pl.when(pend_ref[e] > prev)
            def _():
                tail_copy(e).start()
        for e in range(N_EXPERTS):
            prev = pend_ref[e - 1] if e > 0 else 0

            @pl.when(pend_ref[e] > prev)
            def _():
                tail_copy(e).wait()

        used = pend_ref[N_EXPERTS - 1]
        n_slack = (xs_ref.shape[0] - used) // MOE_SB

        def slack_copy(b):
            start = pl.multiple_of(used + b * MOE_SB, MOE_SB)
            return pltpu.make_async_copy(zbuf, xs_ref.at[pl.ds(start, MOE_SB), :], zsem)

        def slack_start(b, _):
            slack_copy(b).start()
            return 0

        def slack_wait(b, _):
            slack_copy(b).wait()
            return 0

        lax.fori_loop(0, n_slack, slack_start, 0)
        lax.fori_loop(0, n_slack, slack_wait, 0)

    def row_copy(r, k):
        p = pos_ref[2 * (i * tm + r) + k]
        return pltpu.make_async_copy(h2_ref.at[pl.ds(r, 1), :], xs_ref.at[pl.ds(p, 1), :], sem)

    for r in range(tm):
        row_copy(r, 0).start(priority=0)
        row_copy(r, 1).start(priority=1)
    for _ in range(2):
        pltpu.make_async_copy(h2_ref, xs_ref.at[pl.ds(0, tm), :], sem).wait()


def _dispatch(pos_flat, pends, h2, n_rows):
    T = h2.shape[0]
    tm = DISP_TM
    return pl.pallas_call(
        _dispatch_body,
        grid_spec=pltpu.PrefetchScalarGridSpec(
            num_scalar_prefetch=2,
            grid=(T // tm,),
            in_specs=[pl.BlockSpec((tm, D_MODEL), lambda i, pos, pe: (i, 0))],
            out_specs=pl.BlockSpec(memory_space=pl.ANY),
            scratch_shapes=[
                pltpu.VMEM((MOE_SB, D_MODEL), F32),
                pltpu.SemaphoreType.DMA(()),
                pltpu.SemaphoreType.DMA(()),
            ],
        ),
        out_shape=jax.ShapeDtypeStruct((n_rows, D_MODEL), F32),
        compiler_params=pltpu.CompilerParams(
            dimension_semantics=("arbitrary",), has_side_effects=True),
    )(pos_flat, pends, h2)


def _moe_body(ie_ref, is_ref, in_ref, meta_ref, xs_ref, wg_hbm, wu_hbm, wd_hbm, ys_ref,
              xbuf, acc, wgbuf, wubuf, wdbuf, sem_in, sem_out, sem_w):
    s = pl.program_id(0)
    h = pl.program_id(1)
    n_items = pl.num_programs(0)
    n_h = pl.num_programs(1)
    slot = s % 2
    th = MOE_TH
    used = meta_ref[0]
    n_steps = meta_ref[1] * n_h
    t = s * n_h + h

    def n_blocks(item):
        return (in_ref[item] + MOE_SB - 1) // MOE_SB

    def in_copy(item, b):
        sl = item % 2
        off = pl.multiple_of(b * MOE_SB, MOE_SB)
        src = pl.multiple_of(is_ref[item] + off, MOE_SB)
        return pltpu.make_async_copy(xs_ref.at[pl.ds(src, MOE_SB), :],
                                     xbuf.at[sl, pl.ds(off, MOE_SB), :], sem_in.at[sl])

    def out_copy(item, b):
        sl = item % 2
        off = pl.multiple_of(b * MOE_SB, MOE_SB)
        dst = pl.multiple_of(is_ref[item] + off, MOE_SB)
        return pltpu.make_async_copy(acc.at[sl, pl.ds(off, MOE_SB), :],
                                     ys_ref.at[pl.ds(dst, MOE_SB), :], sem_out.at[sl])

    def for_blocks(item, fn):
        def body(b, carry):
            fn(item, b)
            return carry
        lax.fori_loop(0, n_blocks(item), body, 0)

    def start_in(item, b):
        in_copy(item, b).start()

    def wait_in(item, b):
        in_copy(item, b).wait()

    def start_out(item, b):
        out_copy(item, b).start()

    def wait_out(item, b):
        out_copy(item, b).wait()

    def weight_copies(step):
        e = ie_ref[step // n_h]
        col = pl.multiple_of((step % n_h) * th, th)
        ws = step % MOE_WBUF
        half = th // 2
        return (
            pltpu.make_async_copy(wg_hbm.at[e, :, pl.ds(col, th)], wgbuf.at[ws], sem_w.at[ws, 0]),
            pltpu.make_async_copy(wu_hbm.at[e, :, pl.ds(col, th)], wubuf.at[ws], sem_w.at[ws, 1]),
            pltpu.make_async_copy(wd_hbm.at[e, pl.ds(col, half), :],
                                  wdbuf.at[ws, pl.ds(0, half), :], sem_w.at[ws, 2]),
            pltpu.make_async_copy(wd_hbm.at[e, pl.ds(col + half, half), :],
                                  wdbuf.at[ws, pl.ds(half, half), :], sem_w.at[ws, 3]),
        )

    def start_weights(step):
        for k, c in enumerate(weight_copies(step)):
            c.start(priority=k % 2)

    @pl.when(t == 0)
    def _():
        for d in range(MOE_WBUF - 1):
            @pl.when(d < n_steps)
            def _(d=d):
                start_weights(d)

        n_slack = (ys_ref.shape[0] - used) // MOE_SB
        acc[1, pl.ds(0, MOE_SB), :] = jnp.zeros((MOE_SB, D_MODEL), F32)

        def slack_copy(b):
            dst = pl.multiple_of(used + b * MOE_SB, MOE_SB)
            return pltpu.make_async_copy(acc.at[1, pl.ds(0, MOE_SB), :],
                                         ys_ref.at[pl.ds(dst, MOE_SB), :], sem_out.at[1])

        def slack_start(b, _):
            slack_copy(b).start()
            return 0

        def slack_wait(b, _):
            slack_copy(b).wait()
            return 0

        for_blocks(s, start_in)
        lax.fori_loop(0, n_slack, slack_start, 0)
        lax.fori_loop(0, n_slack, slack_wait, 0)

    @pl.when(t + MOE_WBUF - 1 < n_steps)
    def _():
        start_weights(t + MOE_WBUF - 1)

    @pl.when(h == 0)
    def _():
        for_blocks(s, wait_in)

    @pl.when((h == 1) & (s + 1 < n_items))
    def _():
        for_blocks(s + 1, start_in)

    @pl.when(t < n_steps)
    def _():
        for c in weight_copies(t):
            c.wait()

    ws = t % MOE_WBUF

    def chunk(off, size, first):
        off = pl.multiple_of(off, MOE_SB)
        xb = xbuf[slot, pl.ds(off, size), :]
        gate = _dot(xb, wgbuf[ws])
        up = _dot(xb, wubuf[ws])
        hidden = gate * jax.nn.sigmoid(gate) * up
        contrib = _dot(hidden, wdbuf[ws])
        if first:
            acc[slot, pl.ds(off, size), :] = contrib
        else:
            acc[slot, pl.ds(off, size), :] += contrib

    n_pad = n_blocks(s) * MOE_SB
    off = 0
    for size in MOE_CHUNKS:
        take = (n_pad & size) != 0
        for first in (True, False):
            @pl.when(take & ((h == 0) if first else (h > 0)))
            def _(off=off, size=size, first=first):
                chunk(off, size, first)

        off = off + jnp.where(take, size, 0)

    @pl.when(h == n_h - 1)
    def _():
        @pl.when(s > 0)
        def _():
            for_blocks(s - 1, wait_out)

        for_blocks(s, start_out)

        @pl.when(s == n_items - 1)
        def _():
            for_blocks(s, wait_out)


def _moe(item_e, item_start, item_n, meta, xs, w_gate, w_up, w_down):
    n_rows = xs.shape[0]
    n_items = item_e.shape[0]
    th = MOE_TH
    hbm = pl.BlockSpec(memory_space=pl.ANY)
    return pl.pallas_call(
        _moe_body,
        grid_spec=pltpu.PrefetchScalarGridSpec(
            num_scalar_prefetch=4,
            grid=(n_items, D_EXPERT // th),
            in_specs=[hbm, hbm, hbm, hbm],
            out_specs=hbm,
            scratch_shapes=[
                pltpu.VMEM((2, MOE_R, D_MODEL), F32),
                pltpu.VMEM((2, MOE_R, D_MODEL), F32),
                pltpu.VMEM((MOE_WBUF, D_MODEL, th), F32),
                pltpu.VMEM((MOE_WBUF, D_MODEL, th), F32),
                pltpu.VMEM((MOE_WBUF, th, D_MODEL), F32),
                pltpu.SemaphoreType.DMA((2,)),
                pltpu.SemaphoreType.DMA((2,)),
                pltpu.SemaphoreType.DMA((MOE_WBUF, 4)),
            ],
        ),
        out_shape=jax.ShapeDtypeStruct((n_rows, D_MODEL), F32),
        compiler_params=pltpu.CompilerParams(
            dimension_semantics=("arbitrary", "arbitrary"), vmem_limit_bytes=VMEM_LIMIT,
            has_side_effects=True),
    )(item_e, item_start, item_n, meta, xs, w_gate, w_up, w_down)


def _combine_body(pos_ref, x1_ref, gw_ref, gf_ref, ys_ref, o_ref, ybuf, sem):
    i = pl.program_id(0)
    tm = x1_ref.shape[0]
    slot = i % 2

    def row_copy(step, r, k):
        p = pos_ref[2 * (step * tm + r) + k]
        sl = step % 2
        return pltpu.make_async_copy(ys_ref.at[pl.ds(p, 1), :],
                                     ybuf.at[sl, k, pl.ds(r, 1), :], sem.at[sl])

    def issue(step):
        for r in range(tm):
            row_copy(step, r, 0).start(priority=0)
            row_copy(step, r, 1).start(priority=1)

    def drain(step):
        sl = step % 2
        for k in range(2):
            pltpu.make_async_copy(ys_ref.at[pl.ds(0, tm), :], ybuf.at[sl, k], sem.at[sl]).wait()

    @pl.when(i == 0)
    def _():
        issue(i)

    @pl.when(i + 1 < pl.num_programs(0))
    def _():
        issue(i + 1)

    drain(i)

    gw = gw_ref[...]
    x = x1_ref[...] + gw[:, 0:1] * ybuf[slot, 0] + gw[:, 1:2] * ybuf[slot, 1]
    ms = jnp.mean(x * x, axis=-1, keepdims=True)
    o_ref[...] = x * lax.rsqrt(ms + EPS) * gf_ref[...]


def _combine(pos_flat, x1, gate_w, g_final, ys):
    T = x1.shape[0]
    tm = COMB_TM
    return pl.pallas_call(
        _combine_body,
        grid_spec=pltpu.PrefetchScalarGridSpec(
            num_scalar_prefetch=1,
            grid=(T // tm,),
            in_specs=[
                pl.BlockSpec((tm, D_MODEL), lambda i, pos: (i, 0)),
                pl.BlockSpec((tm, LANES), lambda i, pos: (i, 0)),
                pl.BlockSpec((1, D_MODEL), lambda i, pos: (0, 0)),
                pl.BlockSpec(memory_space=pl.ANY),
            ],
            out_specs=pl.BlockSpec((tm, D_MODEL), lambda i, pos: (i, 0)),
            scratch_shapes=[
                pltpu.VMEM((2, 2, tm, D_MODEL), F32),
                pltpu.SemaphoreType.DMA((2,)),
            ],
        ),
        out_shape=jax.ShapeDtypeStruct((T, D_MODEL), F32),
        compiler_params=pltpu.CompilerParams(
            dimension_semantics=("arbitrary",), vmem_limit_bytes=VMEM_LIMIT),
    )(pos_flat, x1, gate_w, g_final.reshape(1, D_MODEL), ys)


def _moe_schedule(counts, n_items):
    padded = ((counts + MOE_SB - 1) // MOE_SB) * MOE_SB
    pends = jnp.cumsum(padded)
    pstarts = pends - padded
    chunks = (counts + MOE_R - 1) // MOE_R
    cends = jnp.cumsum(chunks)
    slot = jnp.arange(n_items, dtype=I32)
    total = cends[-1]
    live = slot < total
    owner = jnp.sum((slot[:, None] >= cends[None, :]).astype(I32), axis=1)
    last_owner = jnp.sum((total - 1 >= cends).astype(I32))
    owner = jnp.where(live, owner, last_owner)
    owner = jnp.minimum(owner, N_EXPERTS - 1)
    cidx = slot - (cends - chunks)[owner]
    item_start = jnp.where(live, pstarts[owner] + cidx * MOE_R, 0)
    item_n = jnp.where(live, jnp.clip(counts[owner] - cidx * MOE_R, 0, MOE_R), 0)
    meta = jnp.stack([pends[-1], total]).astype(I32)
    return pstarts.astype(I32), pends.astype(I32), owner.astype(I32), item_start.astype(I32), \
        item_n.astype(I32), meta


def kernel(x, g_mix, w_in, ln_v_g, ln_v_b, w_spatial, b_spatial, w_branch_a, w_branch_b, w_out,
           g_ffn, w_router_group, b_router_group, w_router_expert, b_router_expert, w_gate, w_up,
           w_down, g_final):
    B, S, D = x.shape
    T = B * S
    cur = x.reshape(T, D)
    depth = g_mix.shape[0]
    for l in range(depth):
        proj = _proj(_norm(cur, g_mix[l]), w_in[l])
        yb = _attn(proj)
        w_r = jnp.zeros((D, LANES), F32)
        w_r = w_r.at[:, :N_GROUPS].set(w_router_group[l])
        w_r = w_r.at[:, N_GROUPS:N_GROUPS + N_EXPERTS].set(w_router_expert[l])
        b_r = jnp.zeros((1, LANES), F32)
        b_r = b_r.at[0, :N_GROUPS].set(b_router_group[l])
        b_r = b_r.at[0, N_GROUPS:N_GROUPS + N_EXPERTS].set(b_router_expert[l])
        x1, h2, logits = _merge(
            proj, yb, cur, ln_v_g[l].reshape(1, A_WIDTH), ln_v_b[l].reshape(1, A_WIDTH),
            w_spatial[l], b_spatial[l].T, w_branch_a[l].astype(BF16), w_branch_b[l].astype(BF16),
            w_out[l].astype(BF16), g_ffn[l].reshape(1, D), w_r, b_r)
        oi, gate_w, cnt = _route(logits)

        counts = cnt[0, :N_EXPERTS].astype(I32)
        n_rows = 2 * T + N_EXPERTS * MOE_SB
        n_items = N_EXPERTS + -(-(2 * T) // MOE_R)
        pstarts, pends, item_e, item_start, item_n, meta = _moe_schedule(counts, n_items)
        chosen = oi[:, 0:2, None] == jnp.arange(N_EXPERTS, dtype=I32)
        pos = (jnp.sum(jnp.where(chosen, pstarts, 0), axis=-1) + oi[:, 2:4]).reshape(2 * T)

        xs = _dispatch(pos, pends, h2, n_rows)
        ys = _moe(item_e, item_start, item_n, meta, xs, w_gate[l], w_up[l], w_down[l])
        if l + 1 < depth:
            raise NotImplementedError("only the final layer fuses the closing rms_norm")
        cur = _combine(pos, x1, gate_w, g_final, ys)
    return cur.reshape(B, S, D)
```

```python
import functools

import jax
import jax.numpy as jnp
from jax import lax
from jax.experimental import pallas as pl
from jax.experimental.pallas import tpu as pltpu

F32 = jnp.float32
BF16 = jnp.bfloat16
I32 = jnp.int32

D_MODEL = 2048
CHUNK = 128
A_GROUPS = 8
A_WIDTH = 1024
SB_HEADS = 8
SB_HEAD_DIM = 128
SB_WIDTH = 1024
N_GROUPS = 4
EXPERTS_PER_GROUP = 8
N_EXPERTS = 32
D_EXPERT = 1024
EPS = 1e-6
PROJ_WIDTH = 2 * A_WIDTH + 3 * SB_WIDTH + 2 * D_MODEL
LANES = 128
LOG2E = 1.4426950408889634
EXP2_ZERO_BELOW = -151.0

OFF_U, OFF_V = 0, A_WIDTH
OFF_Q = 2 * A_WIDTH
OFF_K = OFF_Q + SB_WIDTH
OFF_VV = OFF_K + SB_WIDTH
OFF_GA = OFF_VV + SB_WIDTH
OFF_GB = OFF_GA + D_MODEL

VMEM_LIMIT = 56 * 1024 * 1024

NORM_TM = 512
PROJ_TM, PROJ_TN = 2048, 256
ATT_TQ = 256
ATT_HEADS = 8
MERGE_TM = 256
ROUTE_TB = 1024
DISP_TM = 1024
MOE_SB = 128
MOE_CHUNKS = (512, 256, 128)
MOE_R = 768
MOE_TH = 256
MOE_WBUF = 4
COMB_TM = 512


def _dot(a, b):
    return jnp.dot(a, b, preferred_element_type=F32)


def _norm_body(x_ref, g_ref, o_ref):
    x = x_ref[...]
    ms = jnp.mean(x * x, axis=-1, keepdims=True)
    o_ref[...] = (x * lax.rsqrt(ms + EPS) * g_ref[...]).astype(BF16)


def _norm(x2, g):
    T = x2.shape[0]
    tm = min(NORM_TM, T)
    return pl.pallas_call(
        _norm_body,
        grid=(T // tm,),
        in_specs=[pl.BlockSpec((tm, D_MODEL), lambda i: (i, 0)),
                  pl.BlockSpec((1, D_MODEL), lambda i: (0, 0))],
        out_specs=pl.BlockSpec((tm, D_MODEL), lambda i: (i, 0)),
        out_shape=jax.ShapeDtypeStruct((T, D_MODEL), BF16),
        compiler_params=pltpu.CompilerParams(dimension_semantics=("arbitrary",)),
    )(x2, g.reshape(1, D_MODEL))


def _proj_body(h_ref, w_ref, o_ref, *, n_gelu, n_plain, rows):
    j = pl.program_id(0)
    n_chunks = h_ref.shape[0] // rows

    def column_block(act):
        w = w_ref[...].astype(BF16)

        def chunk(c, carry):
            r = pl.ds(pl.multiple_of(c * rows, rows), rows)
            o_ref[r, :] = act(_dot(h_ref[r, :], w)).astype(BF16)
            return carry

        lax.fori_loop(0, n_chunks, chunk, 0)

    @pl.when(j < n_gelu)
    def _():
        column_block(jax.nn.gelu)

    @pl.when((j >= n_gelu) & (j < n_gelu + n_plain))
    def _():
        column_block(lambda v: v)

    @pl.when(j >= n_gelu + n_plain)
    def _():
        column_block(jax.nn.sigmoid)


def _proj(h, w_in):
    T = h.shape[0]
    tn = PROJ_TN
    body = functools.partial(_proj_body, n_gelu=OFF_Q // tn, n_plain=(OFF_GA - OFF_Q) // tn,
                             rows=min(PROJ_TM, T))
    return pl.pallas_call(
        body,
        grid=(PROJ_WIDTH // tn,),
        in_specs=[
            pl.BlockSpec((T, D_MODEL), lambda j: (0, 0), pipeline_mode=pl.Buffered(1)),
            pl.BlockSpec((D_MODEL, tn), lambda j: (0, j)),
        ],
        out_specs=pl.BlockSpec((T, tn), lambda j: (0, j)),
        out_shape=jax.ShapeDtypeStruct((T, PROJ_WIDTH), BF16),
        compiler_params=pltpu.CompilerParams(
            dimension_semantics=("arbitrary",), vmem_limit_bytes=VMEM_LIMIT),
    )(h, w_in)


def _attn_tile(q, k, v, upper2, carry, causal, live):
    z = lax.dot_general(q, k, (((1,), (1,)), ((), ())), preferred_element_type=F32)
    z = z * (SB_HEAD_DIM ** -0.5 * LOG2E)
    nz = -z
    soft = jnp.log2(1.0 + jnp.exp2(jnp.minimum(z, nz)))
    log_rem = jnp.minimum(nz, 0.0) - soft
    log_beta = z + log_rem
    if causal is not None:
        log_rem = jnp.where(causal, log_rem, 0.0)
    hi = log_rem.astype(BF16)
    lo = (log_rem - hi.astype(F32)).astype(BF16)
    suffix = _dot(jnp.concatenate([hi, lo], axis=1), upper2)
    w = jnp.exp2(log_beta + suffix + carry)
    if causal is not None:
        w = jnp.where(causal, w, 0.0)
    contrib = _dot(w.astype(BF16), v)
    total = jnp.sum(log_rem, axis=1, keepdims=True)
    if live is not None:
        contrib = contrib * live
        total = total * live
    return contrib, carry + total


def _attn_body(q_ref, k_ref, v_ref, o_ref):
    i = pl.program_id(1)
    t = ATT_TQ
    dh = SB_HEAD_DIM
    heads = range(ATT_HEADS)
    row = lax.broadcasted_iota(I32, (t, t), 0)
    col = lax.broadcasted_iota(I32, (t, t), 1)
    upper = (row > col).astype(BF16)
    upper2 = jnp.concatenate([upper, upper], axis=0)
    diagonal = col < row

    def tile(a, j, carry, causal=None, live=None):
        off = pl.multiple_of(jnp.maximum(j, 0) * t, t)
        hs = slice(a * dh, (a + 1) * dh)
        return _attn_tile(q_ref[:, hs], k_ref[pl.ds(off, t), hs], v_ref[pl.ds(off, t), hs],
                          upper2, carry, causal, live)

    def pair(j, accs, carries, causal):
        live = (j >= 1).astype(F32)
        new_accs, new_carries = [], []
        for a in heads:
            c1, carry = tile(a, j, carries[a], causal=causal)
            c2, carry = tile(a, j - 1, carry, live=live)
            new_accs.append(accs[a] + c1 + c2)
            new_carries.append(carry)
        return tuple(new_accs), tuple(new_carries)

    def cond(state):
        j, _, carries = state
        top = functools.reduce(jnp.maximum, carries)
        return (j >= 0) & (jnp.max(top) > EXP2_ZERO_BELOW)

    def step(state):
        j, accs, carries = state
        accs, carries = pair(j, accs, carries, None)
        return j - 2, accs, carries

    zero_acc = tuple(jnp.zeros((t, dh), F32) for _ in heads)
    zero_carry = tuple(jnp.zeros((t, 1), F32) for _ in heads)
    accs, carries = pair(i, zero_acc, zero_carry, diagonal)
    _, accs, _ = lax.while_loop(cond, step, (i - 2, accs, carries))
    for a in heads:
        o_ref[:, a * dh:(a + 1) * dh] = accs[a].astype(BF16)


def _attn(proj):
    T = proj.shape[0]
    width = ATT_HEADS * SB_HEAD_DIM
    qb, kb, vb = OFF_Q // width, OFF_K // width, OFF_VV // width
    return pl.pallas_call(
        _attn_body,
        grid=(SB_HEADS // ATT_HEADS, T // ATT_TQ),
        in_specs=[
            pl.BlockSpec((ATT_TQ, width), lambda h, i: (i, qb + h)),
            pl.BlockSpec((T, width), lambda h, i: (0, kb + h), pipeline_mode=pl.Buffered(1)),
            pl.BlockSpec((T, width), lambda h, i: (0, vb + h), pipeline_mode=pl.Buffered(1)),
        ],
        out_specs=pl.BlockSpec((ATT_TQ, width), lambda h, i: (i, h)),
        out_shape=jax.ShapeDtypeStruct((T, SB_WIDTH), BF16),
        compiler_params=pltpu.CompilerParams(
            dimension_semantics=("arbitrary", "arbitrary"), vmem_limit_bytes=VMEM_LIMIT),
    )(proj, proj, proj)


def _merge_body(u_ref, v_ref, ga0_ref, ga1_ref, gb0_ref, gb1_ref, yb_ref, x_ref, lng_ref, lnb_ref,
                ws_ref, bst_ref, wa_ref, wb_ref, wo_ref, gffn_ref, wr_ref, br_ref,
                x1_ref, h2_ref, lg_ref, ya_scr):
    tm = u_ref.shape[0]
    vf = v_ref[...].astype(F32)
    mu = jnp.mean(vf, axis=-1, keepdims=True)
    xc = vf - mu
    var = jnp.mean(xc * xc, axis=-1, keepdims=True)
    vln = (xc * lax.rsqrt(var + EPS) * lng_ref[...] + lnb_ref[...]).astype(BF16)

    row = lax.broadcasted_iota(I32, (CHUNK, CHUNK), 0)
    col = lax.broadcasted_iota(I32, (CHUNK, CHUNK), 1)
    tril = col <= row
    for g in range(A_GROUPS):
        wg = jnp.where(tril, ws_ref[g], 0.0).astype(BF16)
        bias = bst_ref[:, g:g + 1]
        cs = slice(g * LANES, (g + 1) * LANES)
        for c in range(tm // CHUNK):
            rs = slice(c * CHUNK, (c + 1) * CHUNK)
            mixed = _dot(wg, vln[rs, cs]) + bias
            ya_scr[rs, cs] = (u_ref[rs, cs].astype(F32) * mixed).astype(BF16)

    a = _dot(ya_scr[...], wa_ref[...])
    b = _dot(yb_ref[...], wb_ref[...])
    ga = jnp.concatenate([ga0_ref[...], ga1_ref[...]], axis=1).astype(F32)
    gb = jnp.concatenate([gb0_ref[...], gb1_ref[...]], axis=1).astype(F32)
    merged = (ga * a + gb * b).astype(BF16)
    x1 = x_ref[...] + _dot(merged, wo_ref[...])
    x1_ref[...] = x1
    ms = jnp.mean(x1 * x1, axis=-1, keepdims=True)
    h2 = x1 * lax.rsqrt(ms + EPS) * gffn_ref[...]
    h2_ref[...] = h2
    hi = h2.astype(BF16)
    lo = (h2 - hi.astype(F32)).astype(BF16)
    wr = wr_ref[...]
    whi = wr.astype(BF16)
    wlo = (wr - whi.astype(F32)).astype(BF16)
    lg_ref[...] = _dot(hi, whi) + _dot(hi, wlo) + _dot(lo, whi) + br_ref[...]


def _merge(proj, yb, x2, ln_g, ln_b, w_s, b_s_t, wa, wb, wo, g_ffn, w_r, b_r):
    T = x2.shape[0]
    tm = MERGE_TM
    const2 = lambda i: (0, 0)
    one = pl.Buffered(1)

    def colblk(width, off):
        return pl.BlockSpec((tm, width), lambda i: (i, off // width))

    return pl.pallas_call(
        _merge_body,
        grid=(T // tm,),
        in_specs=[
            colblk(A_WIDTH, OFF_U), colblk(A_WIDTH, OFF_V),
            colblk(A_WIDTH, OFF_GA), colblk(A_WIDTH, OFF_GA + A_WIDTH),
            colblk(A_WIDTH, OFF_GB), colblk(A_WIDTH, OFF_GB + A_WIDTH),
            pl.BlockSpec((tm, SB_WIDTH), lambda i: (i, 0)),
            pl.BlockSpec((tm, D_MODEL), lambda i: (i, 0)),
            pl.BlockSpec((1, A_WIDTH), const2), pl.BlockSpec((1, A_WIDTH), const2),
            pl.BlockSpec((A_GROUPS, CHUNK, CHUNK), lambda i: (0, 0, 0)),
            pl.BlockSpec((CHUNK, A_GROUPS), const2),
            pl.BlockSpec((A_WIDTH, D_MODEL), const2, pipeline_mode=one),
            pl.BlockSpec((SB_WIDTH, D_MODEL), const2, pipeline_mode=one),
            pl.BlockSpec((D_MODEL, D_MODEL), const2, pipeline_mode=one),
            pl.BlockSpec((1, D_MODEL), const2),
            pl.BlockSpec((D_MODEL, LANES), const2, pipeline_mode=one),
            pl.BlockSpec((1, LANES), const2),
        ],
        out_specs=[
            pl.BlockSpec((tm, D_MODEL), lambda i: (i, 0)),
            pl.BlockSpec((tm, D_MODEL), lambda i: (i, 0)),
            pl.BlockSpec((tm, LANES), lambda i: (i, 0)),
        ],
        out_shape=[
            jax.ShapeDtypeStruct((T, D_MODEL), F32),
            jax.ShapeDtypeStruct((T, D_MODEL), F32),
            jax.ShapeDtypeStruct((T, LANES), F32),
        ],
        scratch_shapes=[pltpu.VMEM((tm, A_WIDTH), BF16)],
        compiler_params=pltpu.CompilerParams(
            dimension_semantics=("arbitrary",), vmem_limit_bytes=VMEM_LIMIT),
    )(proj, proj, proj, proj, proj, proj, yb, x2, ln_g, ln_b, w_s, b_s_t, wa, wb, wo, g_ffn, w_r, b_r)


def _route_body(lg_ref, oi_ref, ow_ref, cnt_ref, carry_scr):
    i = pl.program_id(0)
    tb = lg_ref.shape[0]

    @pl.when(i == 0)
    def _():
        carry_scr[...] = jnp.zeros_like(carry_scr)

    lg = lg_ref[...]
    lane = lax.broadcasted_iota(I32, (tb, LANES), 1)
    neg = jnp.float32(-jnp.inf)
    big = jnp.int32(LANES)

    def first_max(mask):
        m = jnp.max(jnp.where(mask, lg, neg), axis=1, keepdims=True)
        idx = jnp.min(jnp.where(mask & (lg == m), lane, big), axis=1, keepdims=True)
        return m, idx

    gmask = lane < N_GROUPS
    gm, grp = first_max(gmask)
    pg_sel = 1.0 / jnp.sum(jnp.where(gmask, jnp.exp(lg - gm), 0.0), axis=1, keepdims=True)

    eid = lane - N_GROUPS
    emask = (eid >= 0) & (eid < N_EXPERTS) & ((eid // EXPERTS_PER_GROUP) == grp)
    m1, i1 = first_max(emask)
    emask2 = emask & (lane != i1)
    m2, i2 = first_max(emask2)
    t = jnp.exp(m2 - m1)
    p1 = 1.0 / (1.0 + t)
    p2 = t / (1.0 + t)
    e1 = i1 - N_GROUPS
    e2 = i2 - N_GROUPS

    sel1 = lane == e1
    sel2 = lane == e2
    onehot = (sel1 | sel2).astype(BF16)
    r = lax.broadcasted_iota(I32, (tb, tb), 0)
    c = lax.broadcasted_iota(I32, (tb, tb), 1)
    earlier = (c < r).astype(BF16)
    prefix = _dot(earlier, onehot) + carry_scr[...]
    rank1 = jnp.sum(jnp.where(sel1, prefix, 0.0), axis=1, keepdims=True).astype(I32)
    rank2 = jnp.sum(jnp.where(sel2, prefix, 0.0), axis=1, keepdims=True).astype(I32)
    carry_scr[...] = carry_scr[...] + jnp.sum(onehot.astype(F32), axis=0, keepdims=True)

    zero = jnp.zeros((tb, LANES), I32)
    oi_ref[...] = jnp.where(lane == 0, e1, jnp.where(lane == 1, e2,
                            jnp.where(lane == 2, rank1, jnp.where(lane == 3, rank2, zero))))
    ow_ref[...] = jnp.where(lane == 0, pg_sel * p1, jnp.where(lane == 1, pg_sel * p2, 0.0))
    cnt_ref[...] = carry_scr[...]


def _route(logits):
    T = logits.shape[0]
    tb = min(ROUTE_TB, T)
    return pl.pallas_call(
        _route_body,
        grid=(T // tb,),
        in_specs=[pl.BlockSpec((tb, LANES), lambda i: (i, 0))],
        out_specs=[
            pl.BlockSpec((tb, LANES), lambda i: (i, 0)),
            pl.BlockSpec((tb, LANES), lambda i: (i, 0)),
            pl.BlockSpec((1, LANES), lambda i: (0, 0)),
        ],
        out_shape=[
            jax.ShapeDtypeStruct((T, LANES), I32),
            jax.ShapeDtypeStruct((T, LANES), F32),
            jax.ShapeDtypeStruct((1, LANES), F32),
        ],
        scratch_shapes=[pltpu.VMEM((1, LANES), F32)],
        compiler_params=pltpu.CompilerParams(dimension_semantics=("arbitrary",)),
    )(logits)


def _dispatch_body(pos_ref, pend_ref, h2_ref, xs_ref, zbuf, sem, zsem):
    i = pl.program_id(0)
    tm = h2_ref.shape[0]

    def tail_copy(e):
        start = pl.multiple_of(pend_ref[e] - MOE_SB, MOE_SB)
        return pltpu.make_async_copy(zbuf, xs_ref.at[pl.ds(start, MOE_SB), :], zsem)

    @pl.when(i == 0)
    def _():
        zbuf[...] = jnp.zeros_like(zbuf)
        for e in range(N_EXPERTS):
            prev = pend_ref[e - 1] if e > 0 else 0

            @pl.when(pend_ref[e] > prev)
            def _():
                tail_copy(e).start()
        for e in range(N_EXPERTS):
            prev = pend_ref[e - 1] if e > 0 else 0

            @pl.when(pend_ref[e] > prev)
            def _():
                tail_copy(e).wait()

        used = pend_ref[N_EXPERTS - 1]
        n_slack = (xs_ref.shape[0] - used) // MOE_SB

        def slack_copy(b):
            start = pl.multiple_of(used + b * MOE_SB, MOE_SB)
            return pltpu.make_async_copy(zbuf, xs_ref.at[pl.ds(start, MOE_SB), :], zsem)

        def slack_start(b, _):
            slack_copy(b).start()
            return 0

        def slack_wait(b, _):
            slack_copy(b).wait()
            return 0

        lax.fori_loop(0, n_slack, slack_start, 0)
        lax.fori_loop(0, n_slack, slack_wait, 0)

    def row_copy(r, k):
        p = pos_ref[2 * (i * tm + r) + k]
        return pltpu.make_async_copy(h2_ref.at[pl.ds(r, 1), :], xs_ref.at[pl.ds(p, 1), :], sem)

    for r in range(tm):
        row_copy(r, 0).start(priority=0)
        row_copy(r, 1).start(priority=1)
    for _ in range(2):
        pltpu.make_async_copy(h2_ref, xs_ref.at[pl.ds(0, tm), :], sem).wait()


def _dispatch(pos_flat, pends, h2, n_rows):
    T = h2.shape[0]
    tm = DISP_TM
    return pl.pallas_call(
        _dispatch_body,
        grid_spec=pltpu.PrefetchScalarGridSpec(
            num_scalar_prefetch=2,
            grid=(T // tm,),
            in_specs=[pl.BlockSpec((tm, D_MODEL), lambda i, pos, pe: (i, 0))],
            out_specs=pl.BlockSpec(memory_space=pl.ANY),
            scratch_shapes=[
                pltpu.VMEM((MOE_SB, D_MODEL), F32),
                pltpu.SemaphoreType.DMA(()),
                pltpu.SemaphoreType.DMA(()),
            ],
        ),
        out_shape=jax.ShapeDtypeStruct((n_rows, D_MODEL), F32),
        compiler_params=pltpu.CompilerParams(
            dimension_semantics=("arbitrary",), has_side_effects=True),
    )(pos_flat, pends, h2)


def _moe_body(ie_ref, is_ref, in_ref, meta_ref, xs_ref, wg_hbm, wu_hbm, wd_hbm, ys_ref,
              xbuf, acc, wgbuf, wubuf, wdbuf, sem_in, sem_out, sem_w):
    s = pl.program_id(0)
    h = pl.program_id(1)
    n_items = pl.num_programs(0)
    n_h = pl.num_programs(1)
    slot = s % 2
    th = MOE_TH
    used = meta_ref[0]
    n_steps = meta_ref[1] * n_h
    t = s * n_h + h

    def n_blocks(item):
        return (in_ref[item] + MOE_SB - 1) // MOE_SB

    def in_copy(item, b):
        sl = item % 2
        off = pl.multiple_of(b * MOE_SB, MOE_SB)
        src = pl.multiple_of(is_ref[item] + off, MOE_SB)
        return pltpu.make_async_copy(xs_ref.at[pl.ds(src, MOE_SB), :],
                                     xbuf.at[sl, pl.ds(off, MOE_SB), :], sem_in.at[sl])

    def out_copy(item, b):
        sl = item % 2
        off = pl.multiple_of(b * MOE_SB, MOE_SB)
        dst = pl.multiple_of(is_ref[item] + off, MOE_SB)
        return pltpu.make_async_copy(acc.at[sl, pl.ds(off, MOE_SB), :],
                                     ys_ref.at[pl.ds(dst, MOE_SB), :], sem_out.at[sl])

    def for_blocks(item, fn):
        def body(b, carry):
            fn(item, b)
            return carry
        lax.fori_loop(0, n_blocks(item), body, 0)

    def start_in(item, b):
        in_copy(item, b).start()

    def wait_in(item, b):
        in_copy(item, b).wait()

    def start_out(item, b):
        out_copy(item, b).start()

    def wait_out(item, b):
        out_copy(item, b).wait()

    def weight_copies(step):
        e = ie_ref[step // n_h]
        col = pl.multiple_of((step % n_h) * th, th)
        ws = step % MOE_WBUF
        half = th // 2
        return (
            pltpu.make_async_copy(wg_hbm.at[e, :, pl.ds(col, th)], wgbuf.at[ws], sem_w.at[ws, 0]),
            pltpu.make_async_copy(wu_hbm.at[e, :, pl.ds(col, th)], wubuf.at[ws], sem_w.at[ws, 1]),
            pltpu.make_async_copy(wd_hbm.at[e, pl.ds(col, half), :],
                                  wdbuf.at[ws, pl.ds(0, half), :], sem_w.at[ws, 2]),
            pltpu.make_async_copy(wd_hbm.at[e, pl.ds(col + half, half), :],
                                  wdbuf.at[ws, pl.ds(half, half), :], sem_w.at[ws, 3]),
        )

    def start_weights(step):
        for k, c in enumerate(weight_copies(step)):
            c.start(priority=k % 2)

    @pl.when(t == 0)
    def _():
        for d in range(MOE_WBUF - 1):
            @pl.when(d < n_steps)
            def _(d=d):
                start_weights(d)

        n_slack = (ys_ref.shape[0] - used) // MOE_SB
        acc[1, pl.ds(0, MOE_SB), :] = jnp.zeros((MOE_SB, D_MODEL), F32)

        def slack_copy(b):
            dst = pl.multiple_of(used + b * MOE_SB, MOE_SB)
            return pltpu.make_async_copy(acc.at[1, pl.ds(0, MOE_SB), :],
                                         ys_ref.at[pl.ds(dst, MOE_SB), :], sem_out.at[1])

        def slack_start(b, _):
            slack_copy(b).start()
            return 0

        def slack_wait(b, _):
            slack_copy(b).wait()
            return 0

        for_blocks(s, start_in)
        lax.fori_loop(0, n_slack, slack_start, 0)
        lax.fori_loop(0, n_slack, slack_wait, 0)

    @pl.when(t + MOE_WBUF - 1 < n_steps)
    def _():
        start_weights(t + MOE_WBUF - 1)

    @pl.when(h == 0)
    def _():
        for_blocks(s, wait_in)

    @pl.when((h == 1) & (s + 1 < n_items))
    def _():
        for_blocks(s + 1, start_in)

    @pl.when(t < n_steps)
    def _():
        for c in weight_copies(t):
            c.wait()

    ws = t % MOE_WBUF

    def chunk(off, size, first):
        off = pl.multiple_of(off, MOE_SB)
        xb = xbuf[slot, pl.ds(off, size), :]
        gate = _dot(xb, wgbuf[ws])
        up = _dot(xb, wubuf[ws])
        hidden = gate * jax.nn.sigmoid(gate) * up
        contrib = _dot(hidden, wdbuf[ws])
        if first:
            acc[slot, pl.ds(off, size), :] = contrib
        else:
            acc[slot, pl.ds(off, size), :] += contrib

    n_pad = n_blocks(s) * MOE_SB
    off = 0
    for size in MOE_CHUNKS:
        take = (n_pad & size) != 0
        for first in (True, False):
            @pl.when(take & ((h == 0) if first else (h > 0)))
            def _(off=off, size=size, first=first):
                chunk(off, size, first)

        off = off + jnp.where(take, size, 0)

    @pl.when(h == n_h - 1)
    def _():
        @pl.when(s > 0)
        def _():
            for_blocks(s - 1, wait_out)

        for_blocks(s, start_out)

        @pl.when(s == n_items - 1)
        def _():
            for_blocks(s, wait_out)


def _moe(item_e, item_start, item_n, meta, xs, w_gate, w_up, w_down):
    n_rows = xs.shape[0]
    n_items = item_e.shape[0]
    th = MOE_TH
    hbm = pl.BlockSpec(memory_space=pl.ANY)
    return pl.pallas_call(
        _moe_body,
        grid_spec=pltpu.PrefetchScalarGridSpec(
            num_scalar_prefetch=4,
            grid=(n_items, D_EXPERT // th),
            in_specs=[hbm, hbm, hbm, hbm],
            out_specs=hbm,
            scratch_shapes=[
                pltpu.VMEM((2, MOE_R, D_MODEL), F32),
                pltpu.VMEM((2, MOE_R, D_MODEL), F32),
                pltpu.VMEM((MOE_WBUF, D_MODEL, th), F32),
                pltpu.VMEM((MOE_WBUF, D_MODEL, th), F32),
                pltpu.VMEM((MOE_WBUF, th, D_MODEL), F32),
                pltpu.SemaphoreType.DMA((2,)),
                pltpu.SemaphoreType.DMA((2,)),
                pltpu.SemaphoreType.DMA((MOE_WBUF, 4)),
            ],
        ),
        out_shape=jax.ShapeDtypeStruct((n_rows, D_MODEL), F32),
        compiler_params=pltpu.CompilerParams(
            dimension_semantics=("arbitrary", "arbitrary"), vmem_limit_bytes=VMEM_LIMIT,
            has_side_effects=True),
    )(item_e, item_start, item_n, meta, xs, w_gate, w_up, w_down)


def _combine_body(pos_ref, x1_ref, gw_ref, gf_ref, ys_ref, o_ref, ybuf, sem):
    i = pl.program_id(0)
    tm = x1_ref.shape[0]
    slot = i % 2

    def row_copy(step, r, k):
        p = pos_ref[2 * (step * tm + r) + k]
        sl = step % 2
        return pltpu.make_async_copy(ys_ref.at[pl.ds(p, 1), :],
                                     ybuf.at[sl, k, pl.ds(r, 1), :], sem.at[sl])

    def issue(step):
        for r in range(tm):
            row_copy(step, r, 0).start(priority=0)
            row_copy(step, r, 1).start(priority=1)

    def drain(step):
        sl = step % 2
        for k in range(2):
            pltpu.make_async_copy(ys_ref.at[pl.ds(0, tm), :], ybuf.at[sl, k], sem.at[sl]).wait()

    @pl.when(i == 0)
    def _():
        issue(i)

    @pl.when(i + 1 < pl.num_programs(0))
    def _():
        issue(i + 1)

    drain(i)

    gw = gw_ref[...]
    x = x1_ref[...] + gw[:, 0:1] * ybuf[slot, 0] + gw[:, 1:2] * ybuf[slot, 1]
    ms = jnp.mean(x * x, axis=-1, keepdims=True)
    o_ref[...] = x * lax.rsqrt(ms + EPS) * gf_ref[...]


def _combine(pos_flat, x1, gate_w, g_final, ys):
    T = x1.shape[0]
    tm = COMB_TM
    return pl.pallas_call(
        _combine_body,
        grid_spec=pltpu.PrefetchScalarGridSpec(
            num_scalar_prefetch=1,
            grid=(T // tm,),
            in_specs=[
                pl.BlockSpec((tm, D_MODEL), lambda i, pos: (i, 0)),
                pl.BlockSpec((tm, LANES), lambda i, pos: (i, 0)),
                pl.BlockSpec((1, D_MODEL), lambda i, pos: (0, 0)),
                pl.BlockSpec(memory_space=pl.ANY),
            ],
            out_specs=pl.BlockSpec((tm, D_MODEL), lambda i, pos: (i, 0)),
            scratch_shapes=[
                pltpu.VMEM((2, 2, tm, D_MODEL), F32),
                pltpu.SemaphoreType.DMA((2,)),
            ],
        ),
        out_shape=jax.ShapeDtypeStruct((T, D_MODEL), F32),
        compiler_params=pltpu.CompilerParams(
            dimension_semantics=("arbitrary",), vmem_limit_bytes=VMEM_LIMIT),
    )(pos_flat, x1, gate_w, g_final.reshape(1, D_MODEL), ys)


def _moe_schedule(counts, n_items):
    padded = ((counts + MOE_SB - 1) // MOE_SB) * MOE_SB
    pends = jnp.cumsum(padded)
    pstarts = pends - padded
    chunks = (counts + MOE_R - 1) // MOE_R
    cends = jnp.cumsum(chunks)
    slot = jnp.arange(n_items, dtype=I32)
    total = cends[-1]
    live = slot < total
    owner = jnp.sum((slot[:, None] >= cends[None, :]).astype(I32), axis=1)
    last_owner = jnp.sum((total - 1 >= cends).astype(I32))
    owner = jnp.where(live, owner, last_owner)
    owner = jnp.minimum(owner, N_EXPERTS - 1)
    cidx = slot - (cends - chunks)[owner]
    item_start = jnp.where(live, pstarts[owner] + cidx * MOE_R, 0)
    item_n = jnp.where(live, jnp.clip(counts[owner] - cidx * MOE_R, 0, MOE_R), 0)
    meta = jnp.stack([pends[-1], total]).astype(I32)
    return pstarts.astype(I32), pends.astype(I32), owner.astype(I32), item_start.astype(I32), \
        item_n.astype(I32), meta


def kernel(x, g_mix, w_in, ln_v_g, ln_v_b, w_spatial, b_spatial, w_branch_a, w_branch_b, w_out,
           g_ffn, w_router_group, b_router_group, w_router_expert, b_router_expert, w_gate, w_up,
           w_down, g_final):
    B, S, D = x.shape
    T = B * S
    cur = x.reshape(T, D)
    depth = g_mix.shape[0]
    for l in range(depth):
        proj = _proj(_norm(cur, g_mix[l]), w_in[l])
        yb = _attn(proj)
        w_r = jnp.zeros((D, LANES), F32)
        w_r = w_r.at[:, :N_GROUPS].set(w_router_group[l])
        w_r = w_r.at[:, N_GROUPS:N_GROUPS + N_EXPERTS].set(w_router_expert[l])
        b_r = jnp.zeros((1, LANES), F32)
        b_r = b_r.at[0, :N_GROUPS].set(b_router_group[l])
        b_r = b_r.at[0, N_GROUPS:N_GROUPS + N_EXPERTS].set(b_router_expert[l])
        x1, h2, logits = _merge(
            proj, yb, cur, ln_v_g[l].reshape(1, A_WIDTH), ln_v_b[l].reshape(1, A_WIDTH),
            w_spatial[l], b_spatial[l].T, w_branch_a[l].astype(BF16), w_branch_b[l].astype(BF16),
            w_out[l].astype(BF16), g_ffn[l].reshape(1, D), w_r, b_r)
        oi, gate_w, cnt = _route(logits)

        counts = cnt[0, :N_EXPERTS].astype(I32)
        n_rows = 2 * T + N_EXPERTS * MOE_SB
        n_items = N_EXPERTS + -(-(2 * T) // MOE_R)
        pstarts, pends, item_e, item_start, item_n, meta = _moe_schedule(counts, n_items)
        chosen = oi[:, 0:2, None] == jnp.arange(N_EXPERTS, dtype=I32)
        pos = (jnp.sum(jnp.where(chosen, pstarts, 0), axis=-1) + oi[:, 2:4]).reshape(2 * T)

        xs = _dispatch(pos, pends, h2, n_rows)
        ys = _moe(item_e, item_start, item_n, meta, xs, w_gate[l], w_up[l], w_down[l])
        if l + 1 < depth:
            raise NotImplementedError("only the final layer fuses the closing rms_norm")
        cur = _combine(pos, x1, gate_w, g_final, ys)
    return cur.reshape(B, S, D)
```

```python
import functools

import jax
import jax.numpy as jnp
from jax import lax
from jax.experimental import pallas as pl
from jax.experimental.pallas import tpu as pltpu

F32 = jnp.float32
BF16 = jnp.bfloat16
I32 = jnp.int32

D_MODEL = 2048
CHUNK = 128
A_GROUPS = 8
A_WIDTH = 1024
SB_HEADS = 8
SB_HEAD_DIM = 128
SB_WIDTH = 1024
N_GROUPS = 4
EXPERTS_PER_GROUP = 8
N_EXPERTS = 32
D_EXPERT = 1024
EPS = 1e-6
PROJ_WIDTH = 2 * A_WIDTH + 3 * SB_WIDTH + 2 * D_MODEL
LANES = 128
LOG2E = 1.4426950408889634
EXP2_ZERO_BELOW = -151.0

OFF_U, OFF_V = 0, A_WIDTH
OFF_Q = 2 * A_WIDTH
OFF_K = OFF_Q + SB_WIDTH
OFF_VV = OFF_K + SB_WIDTH
OFF_GA = OFF_VV + SB_WIDTH
OFF_GB = OFF_GA + D_MODEL

VMEM_LIMIT = 56 * 1024 * 1024

NORM_TM = 512
PROJ_TM, PROJ_TN = 2048, 256
ATT_TQ = 256
ATT_HEADS = 8
MERGE_TM = 256
ROUTE_TB = 1024
DISP_TM = 1024
MOE_SB = 128
MOE_CHUNKS = (512, 256, 128)
MOE_R = 768
MOE_TH = 256
MOE_WBUF = 4
COMB_TM = 512


def _dot(a, b):
    return jnp.dot(a, b, preferred_element_type=F32)


def _norm_body(x_ref, g_ref, o_ref):
    x = x_ref[...]
    ms = jnp.mean(x * x, axis=-1, keepdims=True)
    o_ref[...] = (x * lax.rsqrt(ms + EPS) * g_ref[...]).astype(BF16)


def _norm(x2, g):
    T = x2.shape[0]
    tm = min(NORM_TM, T)
    return pl.pallas_call(
        _norm_body,
        grid=(T // tm,),
        in_specs=[pl.BlockSpec((tm, D_MODEL), lambda i: (i, 0)),
                  pl.BlockSpec((1, D_MODEL), lambda i: (0, 0))],
        out_specs=pl.BlockSpec((tm, D_MODEL), lambda i: (i, 0)),
        out_shape=jax.ShapeDtypeStruct((T, D_MODEL), BF16),
        compiler_params=pltpu.CompilerParams(dimension_semantics=("arbitrary",)),
    )(x2, g.reshape(1, D_MODEL))


def _proj_body(h_ref, w_ref, o_ref, *, n_gelu, n_plain, rows):
    j = pl.program_id(0)
    n_chunks = h_ref.shape[0] // rows

    def column_block(act):
        w = w_ref[...].astype(BF16)

        def chunk(c, carry):
            r = pl.ds(pl.multiple_of(c * rows, rows), rows)
            o_ref[r, :] = act(_dot(h_ref[r, :], w)).astype(BF16)
            return carry

        lax.fori_loop(0, n_chunks, chunk, 0)

    @pl.when(j < n_gelu)
    def _():
        column_block(jax.nn.gelu)

    @pl.when((j >= n_gelu) & (j < n_gelu + n_plain))
    def _():
        column_block(lambda v: v)

    @pl.when(j >= n_gelu + n_plain)
    def _():
        column_block(jax.nn.sigmoid)


def _proj(h, w_in):
    T = h.shape[0]
    tn = PROJ_TN
    body = functools.partial(_proj_body, n_gelu=OFF_Q // tn, n_plain=(OFF_GA - OFF_Q) // tn,
                             rows=min(PROJ_TM, T))
    return pl.pallas_call(
        body,
        grid=(PROJ_WIDTH // tn,),
        in_specs=[
            pl.BlockSpec((T, D_MODEL), lambda j: (0, 0), pipeline_mode=pl.Buffered(1)),
            pl.BlockSpec((D_MODEL, tn), lambda j: (0, j)),
        ],
        out_specs=pl.BlockSpec((T, tn), lambda j: (0, j)),
        out_shape=jax.ShapeDtypeStruct((T, PROJ_WIDTH), BF16),
        compiler_params=pltpu.CompilerParams(
            dimension_semantics=("arbitrary",), vmem_limit_bytes=VMEM_LIMIT),
    )(h, w_in)


def _attn_tile(q, k, v, upper2, carry, causal, live):
    z = lax.dot_general(q, k, (((1,), (1,)), ((), ())), preferred_element_type=F32)
    z = z * (SB_HEAD_DIM ** -0.5 * LOG2E)
    nz = -z
    soft = jnp.log2(1.0 + jnp.exp2(jnp.minimum(z, nz)))
    log_rem = jnp.minimum(nz, 0.0) - soft
    log_beta = z + log_rem
    if causal is not None:
        log_rem = jnp.where(causal, log_rem, 0.0)
    hi = log_rem.astype(BF16)
    lo = (log_rem - hi.astype(F32)).astype(BF16)
    suffix = _dot(jnp.concatenate([hi, lo], axis=1), upper2)
    w = jnp.exp2(log_beta + suffix + carry)
    if causal is not None:
        w = jnp.where(causal, w, 0.0)
    contrib = _dot(w.astype(BF16), v)
    total = jnp.sum(log_rem, axis=1, keepdims=True)
    if live is not None:
        contrib = contrib * live
        total = total * live
    return contrib, carry + total


def _attn_body(q_ref, k_ref, v_ref, o_ref):
    i = pl.program_id(1)
    t = ATT_TQ
    dh = SB_HEAD_DIM
    heads = range(ATT_HEADS)
    row = lax.broadcasted_iota(I32, (t, t), 0)
    col = lax.broadcasted_iota(I32, (t, t), 1)
    upper = (row > col).astype(BF16)
    upper2 = jnp.concatenate([upper, upper], axis=0)
    diagonal = col < row

    def tile(a, j, carry, causal=None, live=None):
        off = pl.multiple_of(jnp.maximum(j, 0) * t, t)
        hs = slice(a * dh, (a + 1) * dh)
        return _attn_tile(q_ref[:, hs], k_ref[pl.ds(off, t), hs], v_ref[pl.ds(off, t), hs],
                          upper2, carry, causal, live)

    def pair(j, accs, carries, causal):
        live = (j >= 1).astype(F32)
        new_accs, new_carries = [], []
        for a in heads:
            c1, carry = tile(a, j, carries[a], causal=causal)
            c2, carry = tile(a, j - 1, carry, live=live)
            new_accs.append(accs[a] + c1 + c2)
            new_carries.append(carry)
        return tuple(new_accs), tuple(new_carries)

    def cond(state):
        j, _, carries = state
        top = functools.reduce(jnp.maximum, carries)
        return (j >= 0) & (jnp.max(top) > EXP2_ZERO_BELOW)

    def step(state):
        j, accs, carries = state
        accs, carries = pair(j, accs, carries, None)
        return j - 2, accs, carries

    zero_acc = tuple(jnp.zeros((t, dh), F32) for _ in heads)
    zero_carry = tuple(jnp.zeros((t, 1), F32) for _ in heads)
    accs, carries = pair(i, zero_acc, zero_carry, diagonal)
    _, accs, _ = lax.while_loop(cond, step, (i - 2, accs, carries))
    for a in heads:
        o_ref[:, a * dh:(a + 1) * dh] = accs[a].astype(BF16)


def _attn(proj):
    T = proj.shape[0]
    width = ATT_HEADS * SB_HEAD_DIM
    qb, kb, vb = OFF_Q // width, OFF_K // width, OFF_VV // width
    return pl.pallas_call(
        _attn_body,
        grid=(SB_HEADS // ATT_HEADS, T // ATT_TQ),
        in_specs=[
            pl.BlockSpec((ATT_TQ, width), lambda h, i: (i, qb + h)),
            pl.BlockSpec((T, width), lambda h, i: (0, kb + h), pipeline_mode=pl.Buffered(1)),
            pl.BlockSpec((T, width), lambda h, i: (0, vb + h), pipeline_mode=pl.Buffered(1)),
        ],
        out_specs=pl.BlockSpec((ATT_TQ, width), lambda h, i: (i, h)),
        out_shape=jax.ShapeDtypeStruct((T, SB_WIDTH), BF16),
        compiler_params=pltpu.CompilerParams(
            dimension_semantics=("arbitrary", "arbitrary"), vmem_limit_bytes=VMEM_LIMIT),
    )(proj, proj, proj)


def _merge_body(u_ref, v_ref, ga0_ref, ga1_ref, gb0_ref, gb1_ref, yb_ref, x_ref, lng_ref, lnb_ref,
                ws_ref, bst_ref, wa_ref, wb_ref, wo_ref, gffn_ref, wr_ref, br_ref,
                x1_ref, h2_ref, lg_ref, ya_scr):
    tm = u_ref.shape[0]
    vf = v_ref[...].astype(F32)
    mu = jnp.mean(vf, axis=-1, keepdims=True)
    xc = vf - mu
    var = jnp.mean(xc * xc, axis=-1, keepdims=True)
    vln = (xc * lax.rsqrt(var + EPS) * lng_ref[...] + lnb_ref[...]).astype(BF16)

    row = lax.broadcasted_iota(I32, (CHUNK, CHUNK), 0)
    col = lax.broadcasted_iota(I32, (CHUNK, CHUNK), 1)
    tril = col <= row
    for g in range(A_GROUPS):
        wg = jnp.where(tril, ws_ref[g], 0.0).astype(BF16)
        bias = bst_ref[:, g:g + 1]
        cs = slice(g * LANES, (g + 1) * LANES)
        for c in range(tm // CHUNK):
            rs = slice(c * CHUNK, (c + 1) * CHUNK)
            mixed = _dot(wg, vln[rs, cs]) + bias
            ya_scr[rs, cs] = (u_ref[rs, cs].astype(F32) * mixed).astype(BF16)

    a = _dot(ya_scr[...], wa_ref[...].astype(BF16))
    b = _dot(yb_ref[...], wb_ref[...].astype(BF16))
    ga = jnp.concatenate([ga0_ref[...], ga1_ref[...]], axis=1).astype(F32)
    gb = jnp.concatenate([gb0_ref[...], gb1_ref[...]], axis=1).astype(F32)
    merged = (ga * a + gb * b).astype(BF16)
    x1 = x_ref[...] + _dot(merged, wo_ref[...])
    x1_ref[...] = x1
    ms = jnp.mean(x1 * x1, axis=-1, keepdims=True)
    h2 = x1 * lax.rsqrt(ms + EPS) * gffn_ref[...]
    h2_ref[...] = h2
    hi = h2.astype(BF16)
    lo = (h2 - hi.astype(F32)).astype(BF16)
    wr = wr_ref[...]
    whi = wr.astype(BF16)
    wlo = (wr - whi.astype(F32)).astype(BF16)
    lg_ref[...] = _dot(hi, whi) + _dot(hi, wlo) + _dot(lo, whi) + br_ref[...]


def _merge(proj, yb, x2, ln_g, ln_b, w_s, b_s_t, wa, wb, wo, g_ffn, w_r, b_r):
    T = x2.shape[0]
    tm = MERGE_TM
    const2 = lambda i: (0, 0)
    one = pl.Buffered(1)

    def colblk(width, off):
        return pl.BlockSpec((tm, width), lambda i: (i, off // width))

    return pl.pallas_call(
        _merge_body,
        grid=(T // tm,),
        in_specs=[
            colblk(A_WIDTH, OFF_U), colblk(A_WIDTH, OFF_V),
            colblk(A_WIDTH, OFF_GA), colblk(A_WIDTH, OFF_GA + A_WIDTH),
            colblk(A_WIDTH, OFF_GB), colblk(A_WIDTH, OFF_GB + A_WIDTH),
            pl.BlockSpec((tm, SB_WIDTH), lambda i: (i, 0)),
            pl.BlockSpec((tm, D_MODEL), lambda i: (i, 0)),
            pl.BlockSpec((1, A_WIDTH), const2), pl.BlockSpec((1, A_WIDTH), const2),
            pl.BlockSpec((A_GROUPS, CHUNK, CHUNK), lambda i: (0, 0, 0)),
            pl.BlockSpec((CHUNK, A_GROUPS), const2),
            pl.BlockSpec((A_WIDTH, D_MODEL), const2, pipeline_mode=one),
            pl.BlockSpec((SB_WIDTH, D_MODEL), const2, pipeline_mode=one),
            pl.BlockSpec((D_MODEL, D_MODEL), const2, pipeline_mode=one),
            pl.BlockSpec((1, D_MODEL), const2),
            pl.BlockSpec((D_MODEL, LANES), const2, pipeline_mode=one),
            pl.BlockSpec((1, LANES), const2),
        ],
        out_specs=[
            pl.BlockSpec((tm, D_MODEL), lambda i: (i, 0)),
            pl.BlockSpec((tm, D_MODEL), lambda i: (i, 0)),
            pl.BlockSpec((tm, LANES), lambda i: (i, 0)),
        ],
        out_shape=[
            jax.ShapeDtypeStruct((T, D_MODEL), F32),
            jax.ShapeDtypeStruct((T, D_MODEL), F32),
            jax.ShapeDtypeStruct((T, LANES), F32),
        ],
        scratch_shapes=[pltpu.VMEM((tm, A_WIDTH), BF16)],
        compiler_params=pltpu.CompilerParams(
            dimension_semantics=("arbitrary",), vmem_limit_bytes=VMEM_LIMIT),
    )(proj, proj, proj, proj, proj, proj, yb, x2, ln_g, ln_b, w_s, b_s_t, wa, wb, wo, g_ffn, w_r, b_r)


def _route_body(lg_ref, oi_ref, ow_ref, cnt_ref, carry_scr):
    i = pl.program_id(0)
    tb = lg_ref.shape[0]

    @pl.when(i == 0)
    def _():
        carry_scr[...] = jnp.zeros_like(carry_scr)

    lg = lg_ref[...]
    lane = lax.broadcasted_iota(I32, (tb, LANES), 1)
    neg = jnp.float32(-jnp.inf)
    big = jnp.int32(LANES)

    def first_max(mask):
        m = jnp.max(jnp.where(mask, lg, neg), axis=1, keepdims=True)
        idx = jnp.min(jnp.where(mask & (lg == m), lane, big), axis=1, keepdims=True)
        return m, idx

    gmask = lane < N_GROUPS
    gm, grp = first_max(gmask)
    pg_sel = 1.0 / jnp.sum(jnp.where(gmask, jnp.exp(lg - gm), 0.0), axis=1, keepdims=True)

    eid = lane - N_GROUPS
    emask = (eid >= 0) & (eid < N_EXPERTS) & ((eid // EXPERTS_PER_GROUP) == grp)
    m1, i1 = first_max(emask)
    emask2 = emask & (lane != i1)
    m2, i2 = first_max(emask2)
    t = jnp.exp(m2 - m1)
    p1 = 1.0 / (1.0 + t)
    p2 = t / (1.0 + t)
    e1 = i1 - N_GROUPS
    e2 = i2 - N_GROUPS

    sel1 = lane == e1
    sel2 = lane == e2
    onehot = (sel1 | sel2).astype(BF16)
    r = lax.broadcasted_iota(I32, (tb, tb), 0)
    c = lax.broadcasted_iota(I32, (tb, tb), 1)
    earlier = (c < r).astype(BF16)
    prefix = _dot(earlier, onehot) + carry_scr[...]
    rank1 = jnp.sum(jnp.where(sel1, prefix, 0.0), axis=1, keepdims=True).astype(I32)
    rank2 = jnp.sum(jnp.where(sel2, prefix, 0.0), axis=1, keepdims=True).astype(I32)
    carry_scr[...] = carry_scr[...] + jnp.sum(onehot.astype(F32), axis=0, keepdims=True)

    zero = jnp.zeros((tb, LANES), I32)
    oi_ref[...] = jnp.where(lane == 0, e1, jnp.where(lane == 1, e2,
                            jnp.where(lane == 2, rank1, jnp.where(lane == 3, rank2, zero))))
    ow_ref[...] = jnp.where(lane == 0, pg_sel * p1, jnp.where(lane == 1, pg_sel * p2, 0.0))
    cnt_ref[...] = carry_scr[...]


def _route(logits):
    T = logits.shape[0]
    tb = min(ROUTE_TB, T)
    return pl.pallas_call(
        _route_body,
        grid=(T // tb,),
        in_specs=[pl.BlockSpec((tb, LANES), lambda i: (i, 0))],
        out_specs=[
            pl.BlockSpec((tb, LANES), lambda i: (i, 0)),
            pl.BlockSpec((tb, LANES), lambda i: (i, 0)),
            pl.BlockSpec((1, LANES), lambda i: (0, 0)),
        ],
        out_shape=[
            jax.ShapeDtypeStruct((T, LANES), I32),
            jax.ShapeDtypeStruct((T, LANES), F32),
            jax.ShapeDtypeStruct((1, LANES), F32),
        ],
        scratch_shapes=[pltpu.VMEM((1, LANES), F32)],
        compiler_params=pltpu.CompilerParams(dimension_semantics=("arbitrary",)),
    )(logits)


def _dispatch_body(pos_ref, pend_ref, h2_ref, xs_ref, zbuf, sem, zsem):
    i = pl.program_id(0)
    tm = h2_ref.shape[0]

    def tail_copy(e):
        start = pl.multiple_of(pend_ref[e] - MOE_SB, MOE_SB)
        return pltpu.make_async_copy(zbuf, xs_ref.at[pl.ds(start, MOE_SB), :], zsem)

    @pl.when(i == 0)
    def _():
        zbuf[...] = jnp.zeros_like(zbuf)
        for e in range(N_EXPERTS):
            prev = pend_ref[e - 1] if e > 0 else 0

            @pl.when(pend_ref[e] > prev)
            def _():
                tail_copy(e).start()
        for e in range(N_EXPERTS):
            prev = pend_ref[e - 1] if e > 0 else 0

            @pl.when(pend_ref[e] > prev)
            def _():
                tail_copy(e).wait()

        used = pend_ref[N_EXPERTS - 1]
        n_slack = (xs_ref.shape[0] - used) // MOE_SB

        def slack_copy(b):
            start = pl.multiple_of(used + b * MOE_SB, MOE_SB)
            return pltpu.make_async_copy(zbuf, xs_ref.at[pl.ds(start, MOE_SB), :], zsem)

        def slack_start(b, _):
            slack_copy(b).start()
            return 0

        def slack_wait(b, _):
            slack_copy(b).wait()
            return 0

        lax.fori_loop(0, n_slack, slack_start, 0)
        lax.fori_loop(0, n_slack, slack_wait, 0)

    def row_copy(r, k):
        p = pos_ref[2 * (i * tm + r) + k]
        return pltpu.make_async_copy(h2_ref.at[pl.ds(r, 1), :], xs_ref.at[pl.ds(p, 1), :], sem)

    for r in range(tm):
        row_copy(r, 0).start(priority=0)
        row_copy(r, 1).start(priority=1)
    for _ in range(2):
        pltpu.make_async_copy(h2_ref, xs_ref.at[pl.ds(0, tm), :], sem).wait()


def _dispatch(pos_flat, pends, h2, n_rows):
    T = h2.shape[0]
    tm = DISP_TM
    return pl.pallas_call(
        _dispatch_body,
        grid_spec=pltpu.PrefetchScalarGridSpec(
            num_scalar_prefetch=2,
            grid=(T // tm,),
            in_specs=[pl.BlockSpec((tm, D_MODEL), lambda i, pos, pe: (i, 0))],
            out_specs=pl.BlockSpec(memory_space=pl.ANY),
            scratch_shapes=[
                pltpu.VMEM((MOE_SB, D_MODEL), F32),
                pltpu.SemaphoreType.DMA(()),
                pltpu.SemaphoreType.DMA(()),
            ],
        ),
        out_shape=jax.ShapeDtypeStruct((n_rows, D_MODEL), F32),
        compiler_params=pltpu.CompilerParams(
            dimension_semantics=("arbitrary",), has_side_effects=True),
    )(pos_flat, pends, h2)


def _moe_body(ie_ref, is_ref, in_ref, meta_ref, xs_ref, wg_hbm, wu_hbm, wd_hbm, ys_ref,
              xbuf, acc, wgbuf, wubuf, wdbuf, sem_in, sem_out, sem_w):
    s = pl.program_id(0)
    h = pl.program_id(1)
    n_items = pl.num_programs(0)
    n_h = pl.num_programs(1)
    slot = s % 2
    th = MOE_TH
    used = meta_ref[0]
    n_steps = meta_ref[1] * n_h
    t = s * n_h + h

    def n_blocks(item):
        return (in_ref[item] + MOE_SB - 1) // MOE_SB

    def in_copy(item, b):
        sl = item % 2
        off = pl.multiple_of(b * MOE_SB, MOE_SB)
        src = pl.multiple_of(is_ref[item] + off, MOE_SB)
        return pltpu.make_async_copy(xs_ref.at[pl.ds(src, MOE_SB), :],
                                     xbuf.at[sl, pl.ds(off, MOE_SB), :], sem_in.at[sl])

    def out_copy(item, b):
        sl = item % 2
        off = pl.multiple_of(b * MOE_SB, MOE_SB)
        dst = pl.multiple_of(is_ref[item] + off, MOE_SB)
        return pltpu.make_async_copy(acc.at[sl, pl.ds(off, MOE_SB), :],
                                     ys_ref.at[pl.ds(dst, MOE_SB), :], sem_out.at[sl])

    def for_blocks(item, fn):
        def body(b, carry):
            fn(item, b)
            return carry
        lax.fori_loop(0, n_blocks(item), body, 0)

    def start_in(item, b):
        in_copy(item, b).start()

    def wait_in(item, b):
        in_copy(item, b).wait()

    def start_out(item, b):
        out_copy(item, b).start()

    def wait_out(item, b):
        out_copy(item, b).wait()

    def weight_copies(step):
        e = ie_ref[step // n_h]
        col = pl.multiple_of((step % n_h) * th, th)
        ws = step % MOE_WBUF
        half = th // 2
        return (
            pltpu.make_async_copy(wg_hbm.at[e, :, pl.ds(col, th)], wgbuf.at[ws], sem_w.at[ws, 0]),
            pltpu.make_async_copy(wu_hbm.at[e, :, pl.ds(col, th)], wubuf.at[ws], sem_w.at[ws, 1]),
            pltpu.make_async_copy(wd_hbm.at[e, pl.ds(col, half), :],
                                  wdbuf.at[ws, pl.ds(0, half), :], sem_w.at[ws, 2]),
            pltpu.make_async_copy(wd_hbm.at[e, pl.ds(col + half, half), :],
                                  wdbuf.at[ws, pl.ds(half, half), :], sem_w.at[ws, 3]),
        )

    def start_weights(step):
        for k, c in enumerate(weight_copies(step)):
            c.start(priority=k % 2)

    @pl.when(t == 0)
    def _():
        for d in range(MOE_WBUF - 1):
            @pl.when(d < n_steps)
            def _(d=d):
                start_weights(d)

        n_slack = (ys_ref.shape[0] - used) // MOE_SB
        acc[1, pl.ds(0, MOE_SB), :] = jnp.zeros((MOE_SB, D_MODEL), F32)

        def slack_copy(b):
            dst = pl.multiple_of(used + b * MOE_SB, MOE_SB)
            return pltpu.make_async_copy(acc.at[1, pl.ds(0, MOE_SB), :],
                                         ys_ref.at[pl.ds(dst, MOE_SB), :], sem_out.at[1])

        def slack_start(b, _):
            slack_copy(b).start()
            return 0

        def slack_wait(b, _):
            slack_copy(b).wait()
            return 0

        for_blocks(s, start_in)
        lax.fori_loop(0, n_slack, slack_start, 0)
        lax.fori_loop(0, n_slack, slack_wait, 0)

    @pl.when(t + MOE_WBUF - 1 < n_steps)
    def _():
        start_weights(t + MOE_WBUF - 1)

    @pl.when(h == 0)
    def _():
        for_blocks(s, wait_in)

    @pl.when((h == 1) & (s + 1 < n_items))
    def _():
        for_blocks(s + 1, start_in)

    @pl.when(t < n_steps)
    def _():
        for c in weight_copies(t):
            c.wait()

    ws = t % MOE_WBUF

    def chunk(off, size, first):
        off = pl.multiple_of(off, MOE_SB)
        xb = xbuf[slot, pl.ds(off, size), :]
        gate = _dot(xb, wgbuf[ws])
        up = _dot(xb, wubuf[ws])
        hidden = gate * jax.nn.sigmoid(gate) * up
        contrib = _dot(hidden, wdbuf[ws])
        if first:
            acc[slot, pl.ds(off, size), :] = contrib
        else:
            acc[slot, pl.ds(off, size), :] += contrib

    n_pad = n_blocks(s) * MOE_SB
    off = 0
    for size in MOE_CHUNKS:
        take = (n_pad & size) != 0
        for first in (True, False):
            @pl.when(take & ((h == 0) if first else (h > 0)))
            def _(off=off, size=size, first=first):
                chunk(off, size, first)

        off = off + jnp.where(take, size, 0)

    @pl.when(h == n_h - 1)
    def _():
        @pl.when(s > 0)
        def _():
            for_blocks(s - 1, wait_out)

        for_blocks(s, start_out)

        @pl.when(s == n_items - 1)
        def _():
            for_blocks(s, wait_out)


def _moe(item_e, item_start, item_n, meta, xs, w_gate, w_up, w_down):
    n_rows = xs.shape[0]
    n_items = item_e.shape[0]
    th = MOE_TH
    hbm = pl.BlockSpec(memory_space=pl.ANY)
    return pl.pallas_call(
        _moe_body,
        grid_spec=pltpu.PrefetchScalarGridSpec(
            num_scalar_prefetch=4,
            grid=(n_items, D_EXPERT // th),
            in_specs=[hbm, hbm, hbm, hbm],
            out_specs=hbm,
            scratch_shapes=[
                pltpu.VMEM((2, MOE_R, D_MODEL), F32),
                pltpu.VMEM((2, MOE_R, D_MODEL), F32),
                pltpu.VMEM((MOE_WBUF, D_MODEL, th), F32),
                pltpu.VMEM((MOE_WBUF, D_MODEL, th), F32),
                pltpu.VMEM((MOE_WBUF, th, D_MODEL), F32),
                pltpu.SemaphoreType.DMA((2,)),
                pltpu.SemaphoreType.DMA((2,)),
                pltpu.SemaphoreType.DMA((MOE_WBUF, 4)),
            ],
        ),
        out_shape=jax.ShapeDtypeStruct((n_rows, D_MODEL), F32),
        compiler_params=pltpu.CompilerParams(
            dimension_semantics=("arbitrary", "arbitrary"), vmem_limit_bytes=VMEM_LIMIT,
            has_side_effects=True),
    )(item_e, item_start, item_n, meta, xs, w_gate, w_up, w_down)


def _combine_body(pos_ref, x1_ref, gw_ref, gf_ref, ys_ref, o_ref, ybuf_even, ybuf_odd, sem):
    i = pl.program_id(0)
    tm = x1_ref.shape[0]
    last = pl.num_programs(0) - 1

    def issue(step, buf, sl):
        for r in range(tm):
            for k in range(2):
                p = pos_ref[2 * (step * tm + r) + k]
                pltpu.make_async_copy(ys_ref.at[pl.ds(p, 1), :], buf.at[k, pl.ds(r, 1), :],
                                      sem.at[sl]).start(priority=k)

    def drain(buf, sl):
        for k in range(2):
            pltpu.make_async_copy(ys_ref.at[pl.ds(0, tm), :], buf.at[k], sem.at[sl]).wait()

    def block(cur, cur_sl, nxt, nxt_sl):
        drain(cur, cur_sl)
        issue(jnp.minimum(i + 1, last), nxt, nxt_sl)
        gw = gw_ref[...]
        x = x1_ref[...] + gw[:, 0:1] * cur[0] + gw[:, 1:2] * cur[1]
        ms = jnp.mean(x * x, axis=-1, keepdims=True)
        o_ref[...] = x * lax.rsqrt(ms + EPS) * gf_ref[...]

        @pl.when(i == last)
        def _():
            drain(nxt, nxt_sl)

    @pl.when(i == 0)
    def _():
        issue(i, ybuf_even, 0)

    @pl.when(i % 2 == 0)
    def _():
        block(ybuf_even, 0, ybuf_odd, 1)

    @pl.when(i % 2 == 1)
    def _():
        block(ybuf_odd, 1, ybuf_even, 0)


def _combine(pos_flat, x1, gate_w, g_final, ys):
    T = x1.shape[0]
    tm = COMB_TM
    return pl.pallas_call(
        _combine_body,
        grid_spec=pltpu.PrefetchScalarGridSpec(
            num_scalar_prefetch=1,
            grid=(T // tm,),
            in_specs=[
                pl.BlockSpec((tm, D_MODEL), lambda i, pos: (i, 0)),
                pl.BlockSpec((tm, LANES), lambda i, pos: (i, 0)),
                pl.BlockSpec((1, D_MODEL), lambda i, pos: (0, 0)),
                pl.BlockSpec(memory_space=pl.ANY),
            ],
            out_specs=pl.BlockSpec((tm, D_MODEL), lambda i, pos: (i, 0)),
            scratch_shapes=[
                pltpu.VMEM((2, tm, D_MODEL), F32),
                pltpu.VMEM((2, tm, D_MODEL), F32),
                pltpu.SemaphoreType.DMA((2,)),
            ],
        ),
        out_shape=jax.ShapeDtypeStruct((T, D_MODEL), F32),
        compiler_params=pltpu.CompilerParams(
            dimension_semantics=("arbitrary",), vmem_limit_bytes=VMEM_LIMIT),
    )(pos_flat, x1, gate_w, g_final.reshape(1, D_MODEL), ys)


def _moe_schedule(counts, n_items):
    padded = ((counts + MOE_SB - 1) // MOE_SB) * MOE_SB
    pends = jnp.cumsum(padded)
    pstarts = pends - padded
    chunks = (counts + MOE_R - 1) // MOE_R
    cends = jnp.cumsum(chunks)
    slot = jnp.arange(n_items, dtype=I32)
    total = cends[-1]
    live = slot < total
    owner = jnp.sum((slot[:, None] >= cends[None, :]).astype(I32), axis=1)
    last_owner = jnp.sum((total - 1 >= cends).astype(I32))
    owner = jnp.where(live, owner, last_owner)
    owner = jnp.minimum(owner, N_EXPERTS - 1)
    cidx = slot - (cends - chunks)[owner]
    item_start = jnp.where(live, pstarts[owner] + cidx * MOE_R, 0)
    item_n = jnp.where(live, jnp.clip(counts[owner] - cidx * MOE_R, 0, MOE_R), 0)
    meta = jnp.stack([pends[-1], total]).astype(I32)
    return pstarts.astype(I32), pends.astype(I32), owner.astype(I32), item_start.astype(I32), \
        item_n.astype(I32), meta


def kernel(x, g_mix, w_in, ln_v_g, ln_v_b, w_spatial, b_spatial, w_branch_a, w_branch_b, w_out,
           g_ffn, w_router_group, b_router_group, w_router_expert, b_router_expert, w_gate, w_up,
           w_down, g_final):
    B, S, D = x.shape
    T = B * S
    cur = x.reshape(T, D)
    depth = g_mix.shape[0]
    for l in range(depth):
        proj = _proj(_norm(cur, g_mix[l]), w_in[l])
        yb = _attn(proj)
        lane_pad = LANES - N_GROUPS - N_EXPERTS
        w_r = jnp.pad(jnp.concatenate([w_router_group[l], w_router_expert[l]], axis=1),
                      ((0, 0), (0, lane_pad)))
        b_r = jnp.pad(jnp.concatenate([b_router_group[l], b_router_expert[l]]),
                      (0, lane_pad)).reshape(1, LANES)
        x1, h2, logits = _merge(
            proj, yb, cur, ln_v_g[l].reshape(1, A_WIDTH), ln_v_b[l].reshape(1, A_WIDTH),
            w_spatial[l], b_spatial[l].T, w_branch_a[l], w_branch_b[l],
            w_out[l].astype(BF16), g_ffn[l].reshape(1, D), w_r, b_r)
        oi, gate_w, cnt = _route(logits)

        counts = cnt[0, :N_EXPERTS].astype(I32)
        n_rows = 2 * T + N_EXPERTS * MOE_SB
        n_items = N_EXPERTS + -(-(2 * T) // MOE_R)
        pstarts, pends, item_e, item_start, item_n, meta = _moe_schedule(counts, n_items)
        chosen = oi[:, 0:2, None] == jnp.arange(N_EXPERTS, dtype=I32)
        pos = (jnp.sum(jnp.where(chosen, pstarts, 0), axis=-1) + oi[:, 2:4]).reshape(2 * T)

        xs = _dispatch(pos, pends, h2, n_rows)
        ys = _moe(item_e, item_start, item_n, meta, xs, w_gate[l], w_up[l], w_down[l])
        if l + 1 < depth:
            raise NotImplementedError("only the final layer fuses the closing rms_norm")
        cur = _combine(pos, x1, gate_w, g_final, ys)
    return cur.reshape(B, S, D)
```

```python
import functools

import jax
import jax.numpy as jnp
from jax import lax
from jax.experimental import pallas as pl
from jax.experimental.pallas import tpu as pltpu

F32 = jnp.float32
BF16 = jnp.bfloat16
I32 = jnp.int32

D_MODEL = 2048
CHUNK = 128
A_GROUPS = 8
A_WIDTH = 1024
SB_HEADS = 8
SB_HEAD_DIM = 128
SB_WIDTH = 1024
N_GROUPS = 4
EXPERTS_PER_GROUP = 8
N_EXPERTS = 32
D_EXPERT = 1024
EPS = 1e-6
PROJ_WIDTH = 2 * A_WIDTH + 3 * SB_WIDTH + 2 * D_MODEL
LANES = 128
LOG2E = 1.4426950408889634
EXP2_ZERO_BELOW = -151.0

OFF_U, OFF_V = 0, A_WIDTH
OFF_Q = 2 * A_WIDTH
OFF_K = OFF_Q + SB_WIDTH
OFF_VV = OFF_K + SB_WIDTH
OFF_GA = OFF_VV + SB_WIDTH
OFF_GB = OFF_GA + D_MODEL

VMEM_LIMIT = 56 * 1024 * 1024

NORM_TM = 512
PROJ_TM, PROJ_TN = 2048, 256
ATT_TQ = 256
ATT_HEADS = 8
MERGE_TM = 256
ROUTE_TB = 1024
DISP_TM = 1024
MOE_SB = 128
MOE_CHUNKS = (512, 256, 128)
MOE_R = 768
assert MOE_R % MOE_SB == 0 and MOE_R <= sum(MOE_CHUNKS) and MOE_CHUNKS[-1] == MOE_SB
MOE_TH = 256
MOE_WBUF = 4
COMB_TM = 512


def _dot(a, b):
    return jnp.dot(a, b, preferred_element_type=F32)


def _norm_body(x_ref, g_ref, o_ref):
    x = x_ref[...]
    ms = jnp.mean(x * x, axis=-1, keepdims=True)
    o_ref[...] = (x * lax.rsqrt(ms + EPS) * g_ref[...]).astype(BF16)


def _norm(x2, g):
    T = x2.shape[0]
    tm = min(NORM_TM, T)
    return pl.pallas_call(
        _norm_body,
        grid=(T // tm,),
        in_specs=[pl.BlockSpec((tm, D_MODEL), lambda i: (i, 0)),
                  pl.BlockSpec((1, D_MODEL), lambda i: (0, 0))],
        out_specs=pl.BlockSpec((tm, D_MODEL), lambda i: (i, 0)),
        out_shape=jax.ShapeDtypeStruct((T, D_MODEL), BF16),
        compiler_params=pltpu.CompilerParams(dimension_semantics=("arbitrary",)),
    )(x2, g.reshape(1, D_MODEL))


def _proj_body(h_ref, w_ref, o_ref, *, n_gelu, n_plain, rows):
    j = pl.program_id(0)
    n_chunks = h_ref.shape[0] // rows

    def column_block(act):
        w = w_ref[...].astype(BF16)

        def chunk(c, carry):
            r = pl.ds(pl.multiple_of(c * rows, rows), rows)
            o_ref[r, :] = act(_dot(h_ref[r, :], w)).astype(BF16)
            return carry

        lax.fori_loop(0, n_chunks, chunk, 0)

    @pl.when(j < n_gelu)
    def _():
        column_block(jax.nn.gelu)

    @pl.when((j >= n_gelu) & (j < n_gelu + n_plain))
    def _():
        column_block(lambda v: v)

    @pl.when(j >= n_gelu + n_plain)
    def _():
        column_block(jax.nn.sigmoid)


def _proj(h, w_in):
    T = h.shape[0]
    tn = PROJ_TN
    body = functools.partial(_proj_body, n_gelu=OFF_Q // tn, n_plain=(OFF_GA - OFF_Q) // tn,
                             rows=min(PROJ_TM, T))
    return pl.pallas_call(
        body,
        grid=(PROJ_WIDTH // tn,),
        in_specs=[
            pl.BlockSpec((T, D_MODEL), lambda j: (0, 0), pipeline_mode=pl.Buffered(1)),
            pl.BlockSpec((D_MODEL, tn), lambda j: (0, j)),
        ],
        out_specs=pl.BlockSpec((T, tn), lambda j: (0, j)),
        out_shape=jax.ShapeDtypeStruct((T, PROJ_WIDTH), BF16),
        compiler_params=pltpu.CompilerParams(
            dimension_semantics=("arbitrary",), vmem_limit_bytes=VMEM_LIMIT),
    )(h, w_in)


def _attn_tile(q, k, v, upper2, carry, causal, live):
    z = lax.dot_general(q, k, (((1,), (1,)), ((), ())), preferred_element_type=F32)
    z = z * (SB_HEAD_DIM ** -0.5 * LOG2E)
    nz = -z
    soft = jnp.log2(1.0 + jnp.exp2(jnp.minimum(z, nz)))
    log_rem = jnp.minimum(nz, 0.0) - soft
    log_beta = z + log_rem
    if causal is not None:
        log_rem = jnp.where(causal, log_rem, 0.0)
    hi = log_rem.astype(BF16)
    lo = (log_rem - hi.astype(F32)).astype(BF16)
    suffix = _dot(jnp.concatenate([hi, lo], axis=1), upper2)
    w = jnp.exp2(log_beta + suffix + carry)
    if causal is not None:
        w = jnp.where(causal, w, 0.0)
    contrib = _dot(w.astype(BF16), v)
    total = jnp.sum(log_rem, axis=1, keepdims=True)
    if live is not None:
        contrib = contrib * live
        total = total * live
    return contrib, carry + total


def _attn_body(q_ref, k_ref, v_ref, o_ref):
    i = pl.program_id(1)
    t = ATT_TQ
    dh = SB_HEAD_DIM
    heads = range(ATT_HEADS)
    row = lax.broadcasted_iota(I32, (t, t), 0)
    col = lax.broadcasted_iota(I32, (t, t), 1)
    upper = (row > col).astype(BF16)
    upper2 = jnp.concatenate([upper, upper], axis=0)
    diagonal = col < row

    def tile(a, j, carry, causal=None, live=None):
        off = pl.multiple_of(jnp.maximum(j, 0) * t, t)
        hs = slice(a * dh, (a + 1) * dh)
        return _attn_tile(q_ref[:, hs], k_ref[pl.ds(off, t), hs], v_ref[pl.ds(off, t), hs],
                          upper2, carry, causal, live)

    def pair(j, accs, carries, causal):
        live = (j >= 1).astype(F32)
        new_accs, new_carries = [], []
        for a in heads:
            c1, carry = tile(a, j, carries[a], causal=causal)
            c2, carry = tile(a, j - 1, carry, live=live)
            new_accs.append(accs[a] + c1 + c2)
            new_carries.append(carry)
        return tuple(new_accs), tuple(new_carries)

    def cond(state):
        j, _, carries = state
        top = functools.reduce(jnp.maximum, carries)
        return (j >= 0) & (jnp.max(top) > EXP2_ZERO_BELOW)

    def step(state):
        j, accs, carries = state
        accs, carries = pair(j, accs, carries, None)
        return j - 2, accs, carries

    zero_acc = tuple(jnp.zeros((t, dh), F32) for _ in heads)
    zero_carry = tuple(jnp.zeros((t, 1), F32) for _ in heads)
    accs, carries = pair(i, zero_acc, zero_carry, diagonal)
    _, accs, _ = lax.while_loop(cond, step, (i - 2, accs, carries))
    for a in heads:
        o_ref[:, a * dh:(a + 1) * dh] = accs[a].astype(BF16)


def _attn(proj):
    T = proj.shape[0]
    width = ATT_HEADS * SB_HEAD_DIM
    qb, kb, vb = OFF_Q // width, OFF_K // width, OFF_VV // width
    return pl.pallas_call(
        _attn_body,
        grid=(SB_HEADS // ATT_HEADS, T // ATT_TQ),
        in_specs=[
            pl.BlockSpec((ATT_TQ, width), lambda h, i: (i, qb + h)),
            pl.BlockSpec((T, width), lambda h, i: (0, kb + h), pipeline_mode=pl.Buffered(1)),
            pl.BlockSpec((T, width), lambda h, i: (0, vb + h), pipeline_mode=pl.Buffered(1)),
        ],
        out_specs=pl.BlockSpec((ATT_TQ, width), lambda h, i: (i, h)),
        out_shape=jax.ShapeDtypeStruct((T, SB_WIDTH), BF16),
        compiler_params=pltpu.CompilerParams(
            dimension_semantics=("arbitrary", "arbitrary"), vmem_limit_bytes=VMEM_LIMIT),
    )(proj, proj, proj)


def _merge_body(u_ref, v_ref, ga0_ref, ga1_ref, gb0_ref, gb1_ref, yb_ref, x_ref, lng_ref, lnb_ref,
                ws_ref, bst_ref, wa_ref, wb_ref, wo_ref, gffn_ref, wr_ref, br_ref,
                x1_ref, h2_ref, lg_ref, ya_scr):
    tm = u_ref.shape[0]
    vf = v_ref[...].astype(F32)
    mu = jnp.mean(vf, axis=-1, keepdims=True)
    xc = vf - mu
    var = jnp.mean(xc * xc, axis=-1, keepdims=True)
    vln = (xc * lax.rsqrt(var + EPS) * lng_ref[...] + lnb_ref[...]).astype(BF16)

    row = lax.broadcasted_iota(I32, (CHUNK, CHUNK), 0)
    col = lax.broadcasted_iota(I32, (CHUNK, CHUNK), 1)
    tril = col <= row
    for g in range(A_GROUPS):
        wg = jnp.where(tril, ws_ref[g], 0.0).astype(BF16)
        bias = bst_ref[:, g:g + 1]
        cs = slice(g * LANES, (g + 1) * LANES)
        for c in range(tm // CHUNK):
            rs = slice(c * CHUNK, (c + 1) * CHUNK)
            mixed = _dot(wg, vln[rs, cs]) + bias
            ya_scr[rs, cs] = (u_ref[rs, cs].astype(F32) * mixed).astype(BF16)

    a = _dot(ya_scr[...], wa_ref[...].astype(BF16))
    b = _dot(yb_ref[...], wb_ref[...].astype(BF16))
    ga = jnp.concatenate([ga0_ref[...], ga1_ref[...]], axis=1).astype(F32)
    gb = jnp.concatenate([gb0_ref[...], gb1_ref[...]], axis=1).astype(F32)
    merged = (ga * a + gb * b).astype(BF16)
    x1 = x_ref[...] + _dot(merged, wo_ref[...])
    x1_ref[...] = x1
    ms = jnp.mean(x1 * x1, axis=-1, keepdims=True)
    h2 = x1 * lax.rsqrt(ms + EPS) * gffn_ref[...]
    h2_ref[...] = h2
    hi = h2.astype(BF16)
    lo = (h2 - hi.astype(F32)).astype(BF16)
    wr = wr_ref[...]
    whi = wr.astype(BF16)
    wlo = (wr - whi.astype(F32)).astype(BF16)
    lg_ref[...] = _dot(hi, whi) + _dot(hi, wlo) + _dot(lo, whi) + br_ref[...]


def _merge(proj, yb, x2, ln_g, ln_b, w_s, b_s_t, wa, wb, wo, g_ffn, w_r, b_r):
    T = x2.shape[0]
    tm = MERGE_TM
    const2 = lambda i: (0, 0)
    one = pl.Buffered(1)

    def colblk(width, off):
        return pl.BlockSpec((tm, width), lambda i: (i, off // width))

    return pl.pallas_call(
        _merge_body,
        grid=(T // tm,),
        in_specs=[
            colblk(A_WIDTH, OFF_U), colblk(A_WIDTH, OFF_V),
            colblk(A_WIDTH, OFF_GA), colblk(A_WIDTH, OFF_GA + A_WIDTH),
            colblk(A_WIDTH, OFF_GB), colblk(A_WIDTH, OFF_GB + A_WIDTH),
            pl.BlockSpec((tm, SB_WIDTH), lambda i: (i, 0)),
            pl.BlockSpec((tm, D_MODEL), lambda i: (i, 0)),
            pl.BlockSpec((1, A_WIDTH), const2), pl.BlockSpec((1, A_WIDTH), const2),
            pl.BlockSpec((A_GROUPS, CHUNK, CHUNK), lambda i: (0, 0, 0)),
            pl.BlockSpec((CHUNK, A_GROUPS), const2),
            pl.BlockSpec((A_WIDTH, D_MODEL), const2, pipeline_mode=one),
            pl.BlockSpec((SB_WIDTH, D_MODEL), const2, pipeline_mode=one),
            pl.BlockSpec((D_MODEL, D_MODEL), const2, pipeline_mode=one),
            pl.BlockSpec((1, D_MODEL), const2),
            pl.BlockSpec((D_MODEL, LANES), const2, pipeline_mode=one),
            pl.BlockSpec((1, LANES), const2),
        ],
        out_specs=[
            pl.BlockSpec((tm, D_MODEL), lambda i: (i, 0)),
            pl.BlockSpec((tm, D_MODEL), lambda i: (i, 0)),
            pl.BlockSpec((tm, LANES), lambda i: (i, 0)),
        ],
        out_shape=[
            jax.ShapeDtypeStruct((T, D_MODEL), F32),
            jax.ShapeDtypeStruct((T, D_MODEL), F32),
            jax.ShapeDtypeStruct((T, LANES), F32),
        ],
        scratch_shapes=[pltpu.VMEM((tm, A_WIDTH), BF16)],
        compiler_params=pltpu.CompilerParams(
            dimension_semantics=("arbitrary",), vmem_limit_bytes=VMEM_LIMIT),
    )(proj, proj, proj, proj, proj, proj, yb, x2, ln_g, ln_b, w_s, b_s_t, wa, wb, wo, g_ffn, w_r, b_r)


def _route_body(lg_ref, oi_ref, ow_ref, cnt_ref, carry_scr):
    i = pl.program_id(0)
    tb = lg_ref.shape[0]

    @pl.when(i == 0)
    def _():
        carry_scr[...] = jnp.zeros_like(carry_scr)

    lg = lg_ref[...]
    lane = lax.broadcasted_iota(I32, (tb, LANES), 1)
    neg = jnp.float32(-jnp.inf)
    big = jnp.int32(LANES)

    def first_max(mask):
        m = jnp.max(jnp.where(mask, lg, neg), axis=1, keepdims=True)
        idx = jnp.min(jnp.where(mask & (lg == m), lane, big), axis=1, keepdims=True)
        return m, idx

    gmask = lane < N_GROUPS
    gm, grp = first_max(gmask)
    pg_sel = 1.0 / jnp.sum(jnp.where(gmask, jnp.exp(lg - gm), 0.0), axis=1, keepdims=True)

    eid = lane - N_GROUPS
    emask = (eid >= 0) & (eid < N_EXPERTS) & ((eid // EXPERTS_PER_GROUP) == grp)
    m1, i1 = first_max(emask)
    emask2 = emask & (lane != i1)
    m2, i2 = first_max(emask2)
    t = jnp.exp(m2 - m1)
    p1 = 1.0 / (1.0 + t)
    p2 = t / (1.0 + t)
    e1 = i1 - N_GROUPS
    e2 = i2 - N_GROUPS

    sel1 = lane == e1
    sel2 = lane == e2
    onehot = (sel1 | sel2).astype(BF16)
    r = lax.broadcasted_iota(I32, (tb, tb), 0)
    c = lax.broadcasted_iota(I32, (tb, tb), 1)
    earlier = (c < r).astype(BF16)
    prefix = _dot(earlier, onehot) + carry_scr[...]
    rank1 = jnp.sum(jnp.where(sel1, prefix, 0.0), axis=1, keepdims=True).astype(I32)
    rank2 = jnp.sum(jnp.where(sel2, prefix, 0.0), axis=1, keepdims=True).astype(I32)
    carry_scr[...] = carry_scr[...] + jnp.sum(onehot.astype(F32), axis=0, keepdims=True)

    zero = jnp.zeros((tb, LANES), I32)
    oi_ref[...] = jnp.where(lane == 0, e1, jnp.where(lane == 1, e2,
                            jnp.where(lane == 2, rank1, jnp.where(lane == 3, rank2, zero))))
    ow_ref[...] = jnp.where(lane == 0, pg_sel * p1, jnp.where(lane == 1, pg_sel * p2, 0.0))
    cnt_ref[...] = carry_scr[...]


def _route(logits):
    T = logits.shape[0]
    tb = min(ROUTE_TB, T)
    return pl.pallas_call(
        _route_body,
        grid=(T // tb,),
        in_specs=[pl.BlockSpec((tb, LANES), lambda i: (i, 0))],
        out_specs=[
            pl.BlockSpec((tb, LANES), lambda i: (i, 0)),
            pl.BlockSpec((tb, LANES), lambda i: (i, 0)),
            pl.BlockSpec((1, LANES), lambda i: (0, 0)),
        ],
        out_shape=[
            jax.ShapeDtypeStruct((T, LANES), I32),
            jax.ShapeDtypeStruct((T, LANES), F32),
            jax.ShapeDtypeStruct((1, LANES), F32),
        ],
        scratch_shapes=[pltpu.VMEM((1, LANES), F32)],
        compiler_params=pltpu.CompilerParams(dimension_semantics=("arbitrary",)),
    )(logits)


def _dispatch_body(pos_ref, pend_ref, h2_ref, xs_ref, zbuf, sem, zsem):
    i = pl.program_id(0)
    tm = h2_ref.shape[0]

    def tail_copy(e):
        start = pl.multiple_of(pend_ref[e] - MOE_SB, MOE_SB)
        return pltpu.make_async_copy(zbuf, xs_ref.at[pl.ds(start, MOE_SB), :], zsem)

    @pl.when(i == 0)
    def _():
        zbuf[...] = jnp.zeros_like(zbuf)
        for e in range(N_EXPERTS):
            prev = pend_ref[e - 1] if e > 0 else 0

            @pl.when(pend_ref[e] > prev)
            def _():
                tail_copy(e).start()
        for e in range(N_EXPERTS):
            prev = pend_ref[e - 1] if e > 0 else 0

            @pl.when(pend_ref[e] > prev)
            def _():
                tail_copy(e).wait()

        used = pend_ref[N_EXPERTS - 1]
        n_slack = (xs_ref.shape[0] - used) // MOE_SB

        def slack_copy(b):
            start = pl.multiple_of(used + b * MOE_SB, MOE_SB)
            return pltpu.make_async_copy(zbuf, xs_ref.at[pl.ds(start, MOE_SB), :], zsem)

        def slack_start(b, _):
            slack_copy(b).start()
            return 0

        def slack_wait(b, _):
            slack_copy(b).wait()
            return 0

        lax.fori_loop(0, n_slack, slack_start, 0)
        lax.fori_loop(0, n_slack, slack_wait, 0)

    def row_copy(r, k):
        p = pos_ref[2 * (i * tm + r) + k]
        return pltpu.make_async_copy(h2_ref.at[pl.ds(r, 1), :], xs_ref.at[pl.ds(p, 1), :], sem)

    for r in range(tm):
        row_copy(r, 0).start(priority=0)
        row_copy(r, 1).start(priority=1)
    for _ in range(2):
        pltpu.make_async_copy(h2_ref, xs_ref.at[pl.ds(0, tm), :], sem).wait()


def _dispatch(pos_flat, pends, h2, n_rows):
    T = h2.shape[0]
    tm = DISP_TM
    return pl.pallas_call(
        _dispatch_body,
        grid_spec=pltpu.PrefetchScalarGridSpec(
            num_scalar_prefetch=2,
            grid=(T // tm,),
            in_specs=[pl.BlockSpec((tm, D_MODEL), lambda i, pos, pe: (i, 0))],
            out_specs=pl.BlockSpec(memory_space=pl.ANY),
            scratch_shapes=[
                pltpu.VMEM((MOE_SB, D_MODEL), F32),
                pltpu.SemaphoreType.DMA(()),
                pltpu.SemaphoreType.DMA(()),
            ],
        ),
        out_shape=jax.ShapeDtypeStruct((n_rows, D_MODEL), F32),
        compiler_params=pltpu.CompilerParams(
            dimension_semantics=("arbitrary",), has_side_effects=True),
    )(pos_flat, pends, h2)


def _moe_body(ie_ref, is_ref, in_ref, meta_ref, xs_ref, wg_hbm, wu_hbm, wd_hbm, ys_ref,
              xbuf, acc, wgbuf, wubuf, wdbuf, sem_in, sem_out, sem_w):
    s = pl.program_id(0)
    h = pl.program_id(1)
    n_items = pl.num_programs(0)
    n_h = pl.num_programs(1)
    slot = s % 2
    th = MOE_TH
    used = meta_ref[0]
    n_steps = meta_ref[1] * n_h
    t = s * n_h + h

    def n_blocks(item):
        return (in_ref[item] + MOE_SB - 1) // MOE_SB

    def in_copy(item, b):
        sl = item % 2
        off = pl.multiple_of(b * MOE_SB, MOE_SB)
        src = pl.multiple_of(is_ref[item] + off, MOE_SB)
        return pltpu.make_async_copy(xs_ref.at[pl.ds(src, MOE_SB), :],
                                     xbuf.at[sl, pl.ds(off, MOE_SB), :], sem_in.at[sl])

    def out_copy(item, b):
        sl = item % 2
        off = pl.multiple_of(b * MOE_SB, MOE_SB)
        dst = pl.multiple_of(is_ref[item] + off, MOE_SB)
        return pltpu.make_async_copy(acc.at[sl, pl.ds(off, MOE_SB), :],
                                     ys_ref.at[pl.ds(dst, MOE_SB), :], sem_out.at[sl])

    def for_blocks(item, fn):
        def body(b, carry):
            fn(item, b)
            return carry
        lax.fori_loop(0, n_blocks(item), body, 0)

    def start_in(item, b):
        in_copy(item, b).start()

    def wait_in(item, b):
        in_copy(item, b).wait()

    def start_out(item, b):
        out_copy(item, b).start()

    def wait_out(item, b):
        out_copy(item, b).wait()

    def weight_copies(step):
        e = ie_ref[step // n_h]
        col = pl.multiple_of((step % n_h) * th, th)
        ws = step % MOE_WBUF
        half = th // 2
        return (
            pltpu.make_async_copy(wg_hbm.at[e, :, pl.ds(col, th)], wgbuf.at[ws], sem_w.at[ws, 0]),
            pltpu.make_async_copy(wu_hbm.at[e, :, pl.ds(col, th)], wubuf.at[ws], sem_w.at[ws, 1]),
            pltpu.make_async_copy(wd_hbm.at[e, pl.ds(col, half), :],
                                  wdbuf.at[ws, pl.ds(0, half), :], sem_w.at[ws, 2]),
            pltpu.make_async_copy(wd_hbm.at[e, pl.ds(col + half, half), :],
                                  wdbuf.at[ws, pl.ds(half, half), :], sem_w.at[ws, 3]),
        )

    def start_weights(step):
        for k, c in enumerate(weight_copies(step)):
            c.start(priority=k % 2)

    @pl.when(t == 0)
    def _():
        for d in range(MOE_WBUF - 1):
            @pl.when(d < n_steps)
            def _(d=d):
                start_weights(d)

        n_slack = (ys_ref.shape[0] - used) // MOE_SB
        acc[1, pl.ds(0, MOE_SB), :] = jnp.zeros((MOE_SB, D_MODEL), F32)

        def slack_copy(b):
            dst = pl.multiple_of(used + b * MOE_SB, MOE_SB)
            return pltpu.make_async_copy(acc.at[1, pl.ds(0, MOE_SB), :],
                                         ys_ref.at[pl.ds(dst, MOE_SB), :], sem_out.at[1])

        def slack_start(b, _):
            slack_copy(b).start()
            return 0

        def slack_wait(b, _):
            slack_copy(b).wait()
            return 0

        for_blocks(s, start_in)
        lax.fori_loop(0, n_slack, slack_start, 0)
        lax.fori_loop(0, n_slack, slack_wait, 0)

    @pl.when(t + MOE_WBUF - 1 < n_steps)
    def _():
        start_weights(t + MOE_WBUF - 1)

    @pl.when(h == 0)
    def _():
        for_blocks(s, wait_in)

    @pl.when((h == 1) & (s + 1 < n_items))
    def _():
        for_blocks(s + 1, start_in)

    @pl.when(t < n_steps)
    def _():
        for c in weight_copies(t):
            c.wait()

    ws = t % MOE_WBUF

    def chunk(off, size, first):
        off = pl.multiple_of(off, MOE_SB)
        xb = xbuf[slot, pl.ds(off, size), :]
        gate = _dot(xb, wgbuf[ws])
        up = _dot(xb, wubuf[ws])
        hidden = gate * jax.nn.sigmoid(gate) * up
        contrib = _dot(hidden, wdbuf[ws])
        if first:
            acc[slot, pl.ds(off, size), :] = contrib
        else:
            acc[slot, pl.ds(off, size), :] += contrib

    n_pad = n_blocks(s) * MOE_SB
    off = 0
    for size in MOE_CHUNKS:
        take = (n_pad & size) != 0
        for first in (True, False):
            @pl.when(take & ((h == 0) if first else (h > 0)))
            def _(off=off, size=size, first=first):
                chunk(off, size, first)

        off = off + jnp.where(take, size, 0)

    @pl.when(h == n_h - 1)
    def _():
        @pl.when(s > 0)
        def _():
            for_blocks(s - 1, wait_out)

        for_blocks(s, start_out)

        @pl.when(s == n_items - 1)
        def _():
            for_blocks(s, wait_out)


def _moe(item_e, item_start, item_n, meta, xs, w_gate, w_up, w_down):
    n_rows = xs.shape[0]
    n_items = item_e.shape[0]
    th = MOE_TH
    hbm = pl.BlockSpec(memory_space=pl.ANY)
    return pl.pallas_call(
        _moe_body,
        grid_spec=pltpu.PrefetchScalarGridSpec(
            num_scalar_prefetch=4,
            grid=(n_items, D_EXPERT // th),
            in_specs=[hbm, hbm, hbm, hbm],
            out_specs=hbm,
            scratch_shapes=[
                pltpu.VMEM((2, MOE_R, D_MODEL), F32),
                pltpu.VMEM((2, MOE_R, D_MODEL), F32),
                pltpu.VMEM((MOE_WBUF, D_MODEL, th), F32),
                pltpu.VMEM((MOE_WBUF, D_MODEL, th), F32),
                pltpu.VMEM((MOE_WBUF, th, D_MODEL), F32),
                pltpu.SemaphoreType.DMA((2,)),
                pltpu.SemaphoreType.DMA((2,)),
                pltpu.SemaphoreType.DMA((MOE_WBUF, 4)),
            ],
        ),
        out_shape=jax.ShapeDtypeStruct((n_rows, D_MODEL), F32),
        compiler_params=pltpu.CompilerParams(
            dimension_semantics=("arbitrary", "arbitrary"), vmem_limit_bytes=VMEM_LIMIT,
            has_side_effects=True),
    )(item_e, item_start, item_n, meta, xs, w_gate, w_up, w_down)


def _combine_body(pos_ref, x1_ref, gw_ref, gf_ref, ys_ref, o_ref, ybuf_even, ybuf_odd, sem):
    i = pl.program_id(0)
    tm = x1_ref.shape[0]
    last = pl.num_programs(0) - 1

    def issue(step, buf, sl):
        for r in range(tm):
            for k in range(2):
                p = pos_ref[2 * (step * tm + r) + k]
                pltpu.make_async_copy(ys_ref.at[pl.ds(p, 1), :], buf.at[k, pl.ds(r, 1), :],
                                      sem.at[sl]).start(priority=k)

    def drain(buf, sl):
        for k in range(2):
            pltpu.make_async_copy(ys_ref.at[pl.ds(0, tm), :], buf.at[k], sem.at[sl]).wait()

    def block(cur, cur_sl, nxt, nxt_sl):
        drain(cur, cur_sl)
        issue(jnp.minimum(i + 1, last), nxt, nxt_sl)
        gw = gw_ref[...]
        x = x1_ref[...] + gw[:, 0:1] * cur[0] + gw[:, 1:2] * cur[1]
        ms = jnp.mean(x * x, axis=-1, keepdims=True)
        o_ref[...] = x * lax.rsqrt(ms + EPS) * gf_ref[...]

        @pl.when(i == last)
        def _():
            drain(nxt, nxt_sl)

    @pl.when(i == 0)
    def _():
        issue(i, ybuf_even, 0)

    @pl.when(i % 2 == 0)
    def _():
        block(ybuf_even, 0, ybuf_odd, 1)

    @pl.when(i % 2 == 1)
    def _():
        block(ybuf_odd, 1, ybuf_even, 0)


def _combine(pos_flat, x1, gate_w, g_final, ys):
    T = x1.shape[0]
    tm = COMB_TM
    return pl.pallas_call(
        _combine_body,
        grid_spec=pltpu.PrefetchScalarGridSpec(
            num_scalar_prefetch=1,
            grid=(T // tm,),
            in_specs=[
                pl.BlockSpec((tm, D_MODEL), lambda i, pos: (i, 0)),
                pl.BlockSpec((tm, LANES), lambda i, pos: (i, 0)),
                pl.BlockSpec((1, D_MODEL), lambda i, pos: (0, 0)),
                pl.BlockSpec(memory_space=pl.ANY),
            ],
            out_specs=pl.BlockSpec((tm, D_MODEL), lambda i, pos: (i, 0)),
            scratch_shapes=[
                pltpu.VMEM((2, tm, D_MODEL), F32),
                pltpu.VMEM((2, tm, D_MODEL), F32),
                pltpu.SemaphoreType.DMA((2,)),
            ],
        ),
        out_shape=jax.ShapeDtypeStruct((T, D_MODEL), F32),
        compiler_params=pltpu.CompilerParams(
            dimension_semantics=("arbitrary",), vmem_limit_bytes=VMEM_LIMIT),
    )(pos_flat, x1, gate_w, g_final.reshape(1, D_MODEL), ys)


def _moe_schedule(counts, n_items):
    padded = ((counts + MOE_SB - 1) // MOE_SB) * MOE_SB
    pends = jnp.cumsum(padded)
    pstarts = pends - padded
    chunks = (counts + MOE_R - 1) // MOE_R
    cends = jnp.cumsum(chunks)
    slot = jnp.arange(n_items, dtype=I32)
    total = cends[-1]
    live = slot < total
    owner = jnp.sum((slot[:, None] >= cends[None, :]).astype(I32), axis=1)
    last_owner = jnp.sum((total - 1 >= cends).astype(I32))
    owner = jnp.where(live, owner, last_owner)
    owner = jnp.minimum(owner, N_EXPERTS - 1)
    cidx = slot - (cends - chunks)[owner]
    item_start = jnp.where(live, pstarts[owner] + cidx * MOE_R, 0)
    item_n = jnp.where(live, jnp.clip(counts[owner] - cidx * MOE_R, 0, MOE_R), 0)
    meta = jnp.stack([pends[-1], total]).astype(I32)
    return pstarts.astype(I32), pends.astype(I32), owner.astype(I32), item_start.astype(I32), \
        item_n.astype(I32), meta


def kernel(x, g_mix, w_in, ln_v_g, ln_v_b, w_spatial, b_spatial, w_branch_a, w_branch_b, w_out,
           g_ffn, w_router_group, b_router_group, w_router_expert, b_router_expert, w_gate, w_up,
           w_down, g_final):
    B, S, D = x.shape
    T = B * S
    assert B == 1 and D == D_MODEL, "the calls below treat the tokens as one causal sequence"
    assert T % max(DISP_TM, ROUTE_TB, COMB_TM, MERGE_TM, ATT_TQ, NORM_TM) == 0
    assert T % min(PROJ_TM, T) == 0
    cur = x.reshape(T, D)
    depth = g_mix.shape[0]
    for l in range(depth):
        proj = _proj(_norm(cur, g_mix[l]), w_in[l])
        yb = _attn(proj)
        lane_pad = LANES - N_GROUPS - N_EXPERTS
        w_r = jnp.pad(jnp.concatenate([w_router_group[l], w_router_expert[l]], axis=1),
                      ((0, 0), (0, lane_pad)))
        b_r = jnp.pad(jnp.concatenate([b_router_group[l], b_router_expert[l]]),
                      (0, lane_pad)).reshape(1, LANES)
        x1, h2, logits = _merge(
            proj, yb, cur, ln_v_g[l].reshape(1, A_WIDTH), ln_v_b[l].reshape(1, A_WIDTH),
            w_spatial[l], b_spatial[l].T, w_branch_a[l], w_branch_b[l],
            w_out[l].astype(BF16), g_ffn[l].reshape(1, D), w_r, b_r)
        oi, gate_w, cnt = _route(logits)

        counts = cnt[0, :N_EXPERTS].astype(I32)
        n_rows = 2 * T + N_EXPERTS * MOE_SB
        n_items = N_EXPERTS + -(-(2 * T) // MOE_R)
        pstarts, pends, item_e, item_start, item_n, meta = _moe_schedule(counts, n_items)
        chosen = oi[:, 0:2, None] == jnp.arange(N_EXPERTS, dtype=I32)
        pos = (jnp.sum(jnp.where(chosen, pstarts, 0), axis=-1) + oi[:, 2:4]).reshape(2 * T)

        xs = _dispatch(pos, pends, h2, n_rows)
        ys = _moe(item_e, item_start, item_n, meta, xs, w_gate[l], w_up[l], w_down[l])
        if l + 1 < depth:
            raise NotImplementedError("only the final layer fuses the closing rms_norm")
        cur = _combine(pos, x1, gate_w, g_final, ys)
    return cur.reshape(B, S, D)
```

```python
import functools

import jax
import jax.numpy as jnp
from jax import lax
from jax.experimental import pallas as pl
from jax.experimental.pallas import tpu as pltpu

F32 = jnp.float32
BF16 = jnp.bfloat16
I32 = jnp.int32

D_MODEL = 2048
CHUNK = 128
A_GROUPS = 8
A_WIDTH = 1024
SB_HEADS = 8
SB_HEAD_DIM = 128
SB_WIDTH = 1024
N_GROUPS = 4
EXPERTS_PER_GROUP = 8
N_EXPERTS = 32
D_EXPERT = 1024
EPS = 1e-6
PROJ_WIDTH = 2 * A_WIDTH + 3 * SB_WIDTH + 2 * D_MODEL
LANES = 128
LOG2E = 1.4426950408889634
EXP2_ZERO_BELOW = -151.0

OFF_U, OFF_V = 0, A_WIDTH
OFF_Q = 2 * A_WIDTH
OFF_K = OFF_Q + SB_WIDTH
OFF_VV = OFF_K + SB_WIDTH
OFF_GA = OFF_VV + SB_WIDTH
OFF_GB = OFF_GA + D_MODEL

VMEM_LIMIT = 56 * 1024 * 1024

NORM_TM = 512
PROJ_TM, PROJ_TN = 2048, 256
ATT_TQ = 256
ATT_HEADS = 8
MERGE_TM = 256
ROUTE_TB = 1024
DISP_TM = 1024
MOE_SB = 128
MOE_CHUNKS = (512, 256, 128)
MOE_WHOLE = 640
MOE_R = 768
assert MOE_R % MOE_SB == 0 and MOE_R <= sum(MOE_CHUNKS) and MOE_CHUNKS[-1] == MOE_SB
MOE_TH = 256
MOE_WBUF = 4
COMB_TM = 512


def _dot(a, b):
    return jnp.dot(a, b, preferred_element_type=F32)


def _norm_body(x_ref, g_ref, o_ref):
    x = x_ref[...]
    ms = jnp.mean(x * x, axis=-1, keepdims=True)
    o_ref[...] = (x * lax.rsqrt(ms + EPS) * g_ref[...]).astype(BF16)


def _norm(x2, g):
    T = x2.shape[0]
    tm = min(NORM_TM, T)
    return pl.pallas_call(
        _norm_body,
        grid=(T // tm,),
        in_specs=[pl.BlockSpec((tm, D_MODEL), lambda i: (i, 0)),
                  pl.BlockSpec((1, D_MODEL), lambda i: (0, 0))],
        out_specs=pl.BlockSpec((tm, D_MODEL), lambda i: (i, 0)),
        out_shape=jax.ShapeDtypeStruct((T, D_MODEL), BF16),
        compiler_params=pltpu.CompilerParams(dimension_semantics=("arbitrary",)),
    )(x2, g.reshape(1, D_MODEL))


def _proj_body(h_ref, w_ref, o_ref, *, n_gelu, n_plain, rows):
    j = pl.program_id(0)
    n_chunks = h_ref.shape[0] // rows

    def column_block(act):
        w = w_ref[...].astype(BF16)

        def chunk(c, carry):
            r = pl.ds(pl.multiple_of(c * rows, rows), rows)
            o_ref[r, :] = act(_dot(h_ref[r, :], w)).astype(BF16)
            return carry

        lax.fori_loop(0, n_chunks, chunk, 0)

    @pl.when(j < n_gelu)
    def _():
        column_block(jax.nn.gelu)

    @pl.when((j >= n_gelu) & (j < n_gelu + n_plain))
    def _():
        column_block(lambda v: v)

    @pl.when(j >= n_gelu + n_plain)
    def _():
        column_block(jax.nn.sigmoid)


def _proj(h, w_in):
    T = h.shape[0]
    tn = PROJ_TN
    body = functools.partial(_proj_body, n_gelu=OFF_Q // tn, n_plain=(OFF_GA - OFF_Q) // tn,
                             rows=min(PROJ_TM, T))
    return pl.pallas_call(
        body,
        grid=(PROJ_WIDTH // tn,),
        in_specs=[
            pl.BlockSpec((T, D_MODEL), lambda j: (0, 0), pipeline_mode=pl.Buffered(1)),
            pl.BlockSpec((D_MODEL, tn), lambda j: (0, j)),
        ],
        out_specs=pl.BlockSpec((T, tn), lambda j: (0, j)),
        out_shape=jax.ShapeDtypeStruct((T, PROJ_WIDTH), BF16),
        compiler_params=pltpu.CompilerParams(
            dimension_semantics=("arbitrary",), vmem_limit_bytes=VMEM_LIMIT),
    )(h, w_in)


def _attn_tile(q, k, v, upper2, carry, causal, live):
    z = lax.dot_general(q, k, (((1,), (1,)), ((), ())), preferred_element_type=F32)
    z = z * (SB_HEAD_DIM ** -0.5 * LOG2E)
    nz = -z
    soft = jnp.log2(1.0 + jnp.exp2(jnp.minimum(z, nz)))
    log_rem = jnp.minimum(nz, 0.0) - soft
    log_beta = z + log_rem
    if causal is not None:
        log_rem = jnp.where(causal, log_rem, 0.0)
    hi = log_rem.astype(BF16)
    lo = (log_rem - hi.astype(F32)).astype(BF16)
    suffix = _dot(jnp.concatenate([hi, lo], axis=1), upper2)
    w = jnp.exp2(log_beta + suffix + carry)
    if causal is not None:
        w = jnp.where(causal, w, 0.0)
    contrib = _dot(w.astype(BF16), v)
    total = jnp.sum(log_rem, axis=1, keepdims=True)
    if live is not None:
        contrib = contrib * live
        total = total * live
    return contrib, carry + total


def _attn_body(q_ref, k_ref, v_ref, o_ref):
    i = pl.program_id(1)
    t = ATT_TQ
    dh = SB_HEAD_DIM
    heads = range(ATT_HEADS)
    row = lax.broadcasted_iota(I32, (t, t), 0)
    col = lax.broadcasted_iota(I32, (t, t), 1)
    upper = (row > col).astype(BF16)
    upper2 = jnp.concatenate([upper, upper], axis=0)
    diagonal = col < row

    def tile(a, j, carry, causal=None, live=None):
        off = pl.multiple_of(jnp.maximum(j, 0) * t, t)
        hs = slice(a * dh, (a + 1) * dh)
        return _attn_tile(q_ref[:, hs], k_ref[pl.ds(off, t), hs], v_ref[pl.ds(off, t), hs],
                          upper2, carry, causal, live)

    def pair(j, accs, carries, causal):
        live = (j >= 1).astype(F32)
        new_accs, new_carries = [], []
        for a in heads:
            c1, carry = tile(a, j, carries[a], causal=causal)
            c2, carry = tile(a, j - 1, carry, live=live)
            new_accs.append(accs[a] + c1 + c2)
            new_carries.append(carry)
        return tuple(new_accs), tuple(new_carries)

    def cond(state):
        j, _, carries = state
        top = functools.reduce(jnp.maximum, carries)
        return (j >= 0) & (jnp.max(top) > EXP2_ZERO_BELOW)

    def step(state):
        j, accs, carries = state
        accs, carries = pair(j, accs, carries, None)
        return j - 2, accs, carries

    zero_acc = tuple(jnp.zeros((t, dh), F32) for _ in heads)
    zero_carry = tuple(jnp.zeros((t, 1), F32) for _ in heads)
    accs, carries = pair(i, zero_acc, zero_carry, diagonal)
    _, accs, _ = lax.while_loop(cond, step, (i - 2, accs, carries))
    for a in heads:
        o_ref[:, a * dh:(a + 1) * dh] = accs[a].astype(BF16)


def _attn(proj):
    T = proj.shape[0]
    width = ATT_HEADS * SB_HEAD_DIM
    qb, kb, vb = OFF_Q // width, OFF_K // width, OFF_VV // width
    return pl.pallas_call(
        _attn_body,
        grid=(SB_HEADS // ATT_HEADS, T // ATT_TQ),
        in_specs=[
            pl.BlockSpec((ATT_TQ, width), lambda h, i: (i, qb + h)),
            pl.BlockSpec((T, width), lambda h, i: (0, kb + h), pipeline_mode=pl.Buffered(1)),
            pl.BlockSpec((T, width), lambda h, i: (0, vb + h), pipeline_mode=pl.Buffered(1)),
        ],
        out_specs=pl.BlockSpec((ATT_TQ, width), lambda h, i: (i, h)),
        out_shape=jax.ShapeDtypeStruct((T, SB_WIDTH), BF16),
        compiler_params=pltpu.CompilerParams(
            dimension_semantics=("arbitrary", "arbitrary"), vmem_limit_bytes=VMEM_LIMIT),
    )(proj, proj, proj)


def _merge_body(u_ref, v_ref, ga0_ref, ga1_ref, gb0_ref, gb1_ref, yb_ref, x_ref, lng_ref, lnb_ref,
                ws_ref, bst_ref, wa_ref, wb_ref, wo_ref, gffn_ref, wr_ref, br_ref,
                x1_ref, h2_ref, lg_ref, ya_scr):
    tm = u_ref.shape[0]
    vf = v_ref[...].astype(F32)
    mu = jnp.mean(vf, axis=-1, keepdims=True)
    xc = vf - mu
    var = jnp.mean(xc * xc, axis=-1, keepdims=True)
    vln = (xc * lax.rsqrt(var + EPS) * lng_ref[...] + lnb_ref[...]).astype(BF16)

    row = lax.broadcasted_iota(I32, (CHUNK, CHUNK), 0)
    col = lax.broadcasted_iota(I32, (CHUNK, CHUNK), 1)
    tril = col <= row
    for g in range(A_GROUPS):
        wg = jnp.where(tril, ws_ref[g], 0.0).astype(BF16)
        bias = bst_ref[:, g:g + 1]
        cs = slice(g * LANES, (g + 1) * LANES)
        for c in range(tm // CHUNK):
            rs = slice(c * CHUNK, (c + 1) * CHUNK)
            mixed = _dot(wg, vln[rs, cs]) + bias
            ya_scr[rs, cs] = (u_ref[rs, cs].astype(F32) * mixed).astype(BF16)

    a = _dot(ya_scr[...], wa_ref[...].astype(BF16))
    b = _dot(yb_ref[...], wb_ref[...].astype(BF16))
    ga = jnp.concatenate([ga0_ref[...], ga1_ref[...]], axis=1).astype(F32)
    gb = jnp.concatenate([gb0_ref[...], gb1_ref[...]], axis=1).astype(F32)
    merged = (ga * a + gb * b).astype(BF16)
    x1 = x_ref[...] + _dot(merged, wo_ref[...])
    x1_ref[...] = x1
    ms = jnp.mean(x1 * x1, axis=-1, keepdims=True)
    h2 = x1 * lax.rsqrt(ms + EPS) * gffn_ref[...]
    h2_ref[...] = h2
    hi = h2.astype(BF16)
    lo = (h2 - hi.astype(F32)).astype(BF16)
    wr = wr_ref[...]
    whi = wr.astype(BF16)
    wlo = (wr - whi.astype(F32)).astype(BF16)
    lg_ref[...] = _dot(hi, whi) + _dot(hi, wlo) + _dot(lo, whi) + br_ref[...]


def _merge(proj, yb, x2, ln_g, ln_b, w_s, b_s_t, wa, wb, wo, g_ffn, w_r, b_r):
    T = x2.shape[0]
    tm = MERGE_TM
    const2 = lambda i: (0, 0)
    one = pl.Buffered(1)

    def colblk(width, off):
        return pl.BlockSpec((tm, width), lambda i: (i, off // width))

    return pl.pallas_call(
        _merge_body,
        grid=(T // tm,),
        in_specs=[
            colblk(A_WIDTH, OFF_U), colblk(A_WIDTH, OFF_V),
            colblk(A_WIDTH, OFF_GA), colblk(A_WIDTH, OFF_GA + A_WIDTH),
            colblk(A_WIDTH, OFF_GB), colblk(A_WIDTH, OFF_GB + A_WIDTH),
            pl.BlockSpec((tm, SB_WIDTH), lambda i: (i, 0)),
            pl.BlockSpec((tm, D_MODEL), lambda i: (i, 0)),
            pl.BlockSpec((1, A_WIDTH), const2), pl.BlockSpec((1, A_WIDTH), const2),
            pl.BlockSpec((A_GROUPS, CHUNK, CHUNK), lambda i: (0, 0, 0)),
            pl.BlockSpec((CHUNK, A_GROUPS), const2),
            pl.BlockSpec((A_WIDTH, D_MODEL), const2, pipeline_mode=one),
            pl.BlockSpec((SB_WIDTH, D_MODEL), const2, pipeline_mode=one),
            pl.BlockSpec((D_MODEL, D_MODEL), const2, pipeline_mode=one),
            pl.BlockSpec((1, D_MODEL), const2),
            pl.BlockSpec((D_MODEL, LANES), const2, pipeline_mode=one),
            pl.BlockSpec((1, LANES), const2),
        ],
        out_specs=[
            pl.BlockSpec((tm, D_MODEL), lambda i: (i, 0)),
            pl.BlockSpec((tm, D_MODEL), lambda i: (i, 0)),
            pl.BlockSpec((tm, LANES), lambda i: (i, 0)),
        ],
        out_shape=[
            jax.ShapeDtypeStruct((T, D_MODEL), F32),
            jax.ShapeDtypeStruct((T, D_MODEL), F32),
            jax.ShapeDtypeStruct((T, LANES), F32),
        ],
        scratch_shapes=[pltpu.VMEM((tm, A_WIDTH), BF16)],
        compiler_params=pltpu.CompilerParams(
            dimension_semantics=("arbitrary",), vmem_limit_bytes=VMEM_LIMIT),
    )(proj, proj, proj, proj, proj, proj, yb, x2, ln_g, ln_b, w_s, b_s_t, wa, wb, wo, g_ffn, w_r, b_r)


def _route_body(lg_ref, oi_ref, ow_ref, cnt_ref, carry_scr):
    i = pl.program_id(0)
    tb = lg_ref.shape[0]

    @pl.when(i == 0)
    def _():
        carry_scr[...] = jnp.zeros_like(carry_scr)

    lg = lg_ref[...]
    lane = lax.broadcasted_iota(I32, (tb, LANES), 1)
    neg = jnp.float32(-jnp.inf)
    big = jnp.int32(LANES)

    def first_max(mask):
        m = jnp.max(jnp.where(mask, lg, neg), axis=1, keepdims=True)
        idx = jnp.min(jnp.where(mask & (lg == m), lane, big), axis=1, keepdims=True)
        return m, idx

    gmask = lane < N_GROUPS
    gm, grp = first_max(gmask)
    pg_sel = 1.0 / jnp.sum(jnp.where(gmask, jnp.exp(lg - gm), 0.0), axis=1, keepdims=True)

    eid = lane - N_GROUPS
    emask = (eid >= 0) & (eid < N_EXPERTS) & ((eid // EXPERTS_PER_GROUP) == grp)
    m1, i1 = first_max(emask)
    emask2 = emask & (lane != i1)
    m2, i2 = first_max(emask2)
    t = jnp.exp(m2 - m1)
    p1 = 1.0 / (1.0 + t)
    p2 = t / (1.0 + t)
    e1 = i1 - N_GROUPS
    e2 = i2 - N_GROUPS

    sel1 = lane == e1
    sel2 = lane == e2
    onehot = (sel1 | sel2).astype(BF16)
    r = lax.broadcasted_iota(I32, (tb, tb), 0)
    c = lax.broadcasted_iota(I32, (tb, tb), 1)
    earlier = (c < r).astype(BF16)
    prefix = _dot(earlier, onehot) + carry_scr[...]
    rank1 = jnp.sum(jnp.where(sel1, prefix, 0.0), axis=1, keepdims=True).astype(I32)
    rank2 = jnp.sum(jnp.where(sel2, prefix, 0.0), axis=1, keepdims=True).astype(I32)
    carry_scr[...] = carry_scr[...] + jnp.sum(onehot.astype(F32), axis=0, keepdims=True)

    zero = jnp.zeros((tb, LANES), I32)
    oi_ref[...] = jnp.where(lane == 0, e1, jnp.where(lane == 1, e2,
                            jnp.where(lane == 2, rank1, jnp.where(lane == 3, rank2, zero))))
    ow_ref[...] = jnp.where(lane == 0, pg_sel * p1, jnp.where(lane == 1, pg_sel * p2, 0.0))
    cnt_ref[...] = carry_scr[...]


def _route(logits):
    T = logits.shape[0]
    tb = min(ROUTE_TB, T)
    return pl.pallas_call(
        _route_body,
        grid=(T // tb,),
        in_specs=[pl.BlockSpec((tb, LANES), lambda i: (i, 0))],
        out_specs=[
            pl.BlockSpec((tb, LANES), lambda i: (i, 0)),
            pl.BlockSpec((tb, LANES), lambda i: (i, 0)),
            pl.BlockSpec((1, LANES), lambda i: (0, 0)),
        ],
        out_shape=[
            jax.ShapeDtypeStruct((T, LANES), I32),
            jax.ShapeDtypeStruct((T, LANES), F32),
            jax.ShapeDtypeStruct((1, LANES), F32),
        ],
        scratch_shapes=[pltpu.VMEM((1, LANES), F32)],
        compiler_params=pltpu.CompilerParams(dimension_semantics=("arbitrary",)),
    )(logits)


def _dispatch_body(pos_ref, pend_ref, h2_ref, xs_ref, zbuf, sem, zsem):
    i = pl.program_id(0)
    tm = h2_ref.shape[0]

    def tail_copy(e):
        start = pl.multiple_of(pend_ref[e] - MOE_SB, MOE_SB)
        return pltpu.make_async_copy(zbuf, xs_ref.at[pl.ds(start, MOE_SB), :], zsem)

    @pl.when(i == 0)
    def _():
        zbuf[...] = jnp.zeros_like(zbuf)
        for e in range(N_EXPERTS):
            prev = pend_ref[e - 1] if e > 0 else 0

            @pl.when(pend_ref[e] > prev)
            def _():
                tail_copy(e).start()
        for e in range(N_EXPERTS):
            prev = pend_ref[e - 1] if e > 0 else 0

            @pl.when(pend_ref[e] > prev)
            def _():
                tail_copy(e).wait()

        used = pend_ref[N_EXPERTS - 1]
        n_slack = (xs_ref.shape[0] - used) // MOE_SB

        def slack_copy(b):
            start = pl.multiple_of(used + b * MOE_SB, MOE_SB)
            return pltpu.make_async_copy(zbuf, xs_ref.at[pl.ds(start, MOE_SB), :], zsem)

        def slack_start(b, _):
            slack_copy(b).start()
            return 0

        def slack_wait(b, _):
            slack_copy(b).wait()
            return 0

        lax.fori_loop(0, n_slack, slack_start, 0)
        lax.fori_loop(0, n_slack, slack_wait, 0)

    def row_copy(r, k):
        p = pos_ref[2 * (i * tm + r) + k]
        return pltpu.make_async_copy(h2_ref.at[pl.ds(r, 1), :], xs_ref.at[pl.ds(p, 1), :], sem)

    for r in range(tm):
        row_copy(r, 0).start(priority=0)
        row_copy(r, 1).start(priority=1)
    for _ in range(2):
        pltpu.make_async_copy(h2_ref, xs_ref.at[pl.ds(0, tm), :], sem).wait()


def _dispatch(pos_flat, pends, h2, n_rows):
    T = h2.shape[0]
    tm = DISP_TM
    return pl.pallas_call(
        _dispatch_body,
        grid_spec=pltpu.PrefetchScalarGridSpec(
            num_scalar_prefetch=2,
            grid=(T // tm,),
            in_specs=[pl.BlockSpec((tm, D_MODEL), lambda i, pos, pe: (i, 0))],
            out_specs=pl.BlockSpec(memory_space=pl.ANY),
            scratch_shapes=[
                pltpu.VMEM((MOE_SB, D_MODEL), F32),
                pltpu.SemaphoreType.DMA(()),
                pltpu.SemaphoreType.DMA(()),
            ],
        ),
        out_shape=jax.ShapeDtypeStruct((n_rows, D_MODEL), F32),
        compiler_params=pltpu.CompilerParams(
            dimension_semantics=("arbitrary",), has_side_effects=True),
    )(pos_flat, pends, h2)


def _moe_body(ie_ref, is_ref, in_ref, meta_ref, xs_ref, wg_hbm, wu_hbm, wd_hbm, ys_ref,
              xbuf, acc, wgbuf, wubuf, wdbuf, sem_in, sem_out, sem_w):
    s = pl.program_id(0)
    h = pl.program_id(1)
    n_items = pl.num_programs(0)
    n_h = pl.num_programs(1)
    slot = s % 2
    th = MOE_TH
    used = meta_ref[0]
    n_steps = meta_ref[1] * n_h
    t = s * n_h + h

    def n_blocks(item):
        return (in_ref[item] + MOE_SB - 1) // MOE_SB

    def in_copy(item, b):
        sl = item % 2
        off = pl.multiple_of(b * MOE_SB, MOE_SB)
        src = pl.multiple_of(is_ref[item] + off, MOE_SB)
        return pltpu.make_async_copy(xs_ref.at[pl.ds(src, MOE_SB), :],
                                     xbuf.at[sl, pl.ds(off, MOE_SB), :], sem_in.at[sl])

    def out_copy(item, b):
        sl = item % 2
        off = pl.multiple_of(b * MOE_SB, MOE_SB)
        dst = pl.multiple_of(is_ref[item] + off, MOE_SB)
        return pltpu.make_async_copy(acc.at[sl, pl.ds(off, MOE_SB), :],
                                     ys_ref.at[pl.ds(dst, MOE_SB), :], sem_out.at[sl])

    def for_blocks(item, fn):
        def body(b, carry):
            fn(item, b)
            return carry
        lax.fori_loop(0, n_blocks(item), body, 0)

    def start_in(item, b):
        in_copy(item, b).start()

    def wait_in(item, b):
        in_copy(item, b).wait()

    def start_out(item, b):
        out_copy(item, b).start()

    def wait_out(item, b):
        out_copy(item, b).wait()

    def weight_copies(step):
        e = ie_ref[step // n_h]
        col = pl.multiple_of((step % n_h) * th, th)
        ws = step % MOE_WBUF
        half = th // 2
        return (
            pltpu.make_async_copy(wg_hbm.at[e, :, pl.ds(col, th)], wgbuf.at[ws], sem_w.at[ws, 0]),
            pltpu.make_async_copy(wu_hbm.at[e, :, pl.ds(col, th)], wubuf.at[ws], sem_w.at[ws, 1]),
            pltpu.make_async_copy(wd_hbm.at[e, pl.ds(col, half), :],
                                  wdbuf.at[ws, pl.ds(0, half), :], sem_w.at[ws, 2]),
            pltpu.make_async_copy(wd_hbm.at[e, pl.ds(col + half, half), :],
                                  wdbuf.at[ws, pl.ds(half, half), :], sem_w.at[ws, 3]),
        )

    def start_weights(step):
        for k, c in enumerate(weight_copies(step)):
            c.start(priority=k % 2)

    @pl.when(t == 0)
    def _():
        for d in range(MOE_WBUF - 1):
            @pl.when(d < n_steps)
            def _(d=d):
                start_weights(d)

        n_slack = (ys_ref.shape[0] - used) // MOE_SB
        acc[1, pl.ds(0, MOE_SB), :] = jnp.zeros((MOE_SB, D_MODEL), F32)

        def slack_copy(b):
            dst = pl.multiple_of(used + b * MOE_SB, MOE_SB)
            return pltpu.make_async_copy(acc.at[1, pl.ds(0, MOE_SB), :],
                                         ys_ref.at[pl.ds(dst, MOE_SB), :], sem_out.at[1])

        def slack_start(b, _):
            slack_copy(b).start()
            return 0

        def slack_wait(b, _):
            slack_copy(b).wait()
            return 0

        for_blocks(s, start_in)
        lax.fori_loop(0, n_slack, slack_start, 0)
        lax.fori_loop(0, n_slack, slack_wait, 0)

    @pl.when(t + MOE_WBUF - 1 < n_steps)
    def _():
        start_weights(t + MOE_WBUF - 1)

    @pl.when(h == 0)
    def _():
        for_blocks(s, wait_in)

    @pl.when((h == 1) & (s + 1 < n_items))
    def _():
        for_blocks(s + 1, start_in)

    @pl.when(t < n_steps)
    def _():
        for c in weight_copies(t):
            c.wait()

    ws = t % MOE_WBUF

    def chunk(off, size, first):
        off = pl.multiple_of(off, MOE_SB)
        xb = xbuf[slot, pl.ds(off, size), :]
        gate = _dot(xb, wgbuf[ws])
        up = _dot(xb, wubuf[ws])
        hidden = gate * jax.nn.sigmoid(gate) * up
        contrib = _dot(hidden, wdbuf[ws])
        if first:
            acc[slot, pl.ds(off, size), :] = contrib
        else:
            acc[slot, pl.ds(off, size), :] += contrib

    n_pad = n_blocks(s) * MOE_SB
    whole = n_pad == MOE_WHOLE
    for first in (True, False):
        @pl.when(whole & ((h == 0) if first else (h > 0)))
        def _(first=first):
            chunk(0, MOE_WHOLE, first)

    off = 0
    for size in MOE_CHUNKS:
        take = ((n_pad & size) != 0) & jnp.logical_not(whole)
        for first in (True, False):
            @pl.when(take & ((h == 0) if first else (h > 0)))
            def _(off=off, size=size, first=first):
                chunk(off, size, first)

        off = off + jnp.where(take, size, 0)

    @pl.when(h == n_h - 1)
    def _():
        @pl.when(s > 0)
        def _():
            for_blocks(s - 1, wait_out)

        for_blocks(s, start_out)

        @pl.when(s == n_items - 1)
        def _():
            for_blocks(s, wait_out)


def _moe(item_e, item_start, item_n, meta, xs, w_gate, w_up, w_down):
    n_rows = xs.shape[0]
    n_items = item_e.shape[0]
    th = MOE_TH
    hbm = pl.BlockSpec(memory_space=pl.ANY)
    return pl.pallas_call(
        _moe_body,
        grid_spec=pltpu.PrefetchScalarGridSpec(
            num_scalar_prefetch=4,
            grid=(n_items, D_EXPERT // th),
            in_specs=[hbm, hbm, hbm, hbm],
            out_specs=hbm,
            scratch_shapes=[
                pltpu.VMEM((2, MOE_R, D_MODEL), F32),
                pltpu.VMEM((2, MOE_R, D_MODEL), F32),
                pltpu.VMEM((MOE_WBUF, D_MODEL, th), F32),
                pltpu.VMEM((MOE_WBUF, D_MODEL, th), F32),
                pltpu.VMEM((MOE_WBUF, th, D_MODEL), F32),
                pltpu.SemaphoreType.DMA((2,)),
                pltpu.SemaphoreType.DMA((2,)),
                pltpu.SemaphoreType.DMA((MOE_WBUF, 4)),
            ],
        ),
        out_shape=jax.ShapeDtypeStruct((n_rows, D_MODEL), F32),
        compiler_params=pltpu.CompilerParams(
            dimension_semantics=("arbitrary", "arbitrary"), vmem_limit_bytes=VMEM_LIMIT,
            has_side_effects=True),
    )(item_e, item_start, item_n, meta, xs, w_gate, w_up, w_down)


def _combine_body(pos_ref, x1_ref, gw_ref, gf_ref, ys_ref, o_ref, ybuf_even, ybuf_odd, sem):
    i = pl.program_id(0)
    tm = x1_ref.shape[0]
    last = pl.num_programs(0) - 1

    def issue(step, buf, sl):
        for r in range(tm):
            for k in range(2):
                p = pos_ref[2 * (step * tm + r) + k]
                pltpu.make_async_copy(ys_ref.at[pl.ds(p, 1), :], buf.at[k, pl.ds(r, 1), :],
                                      sem.at[sl]).start(priority=k)

    def drain(buf, sl):
        for k in range(2):
            pltpu.make_async_copy(ys_ref.at[pl.ds(0, tm), :], buf.at[k], sem.at[sl]).wait()

    def block(cur, cur_sl, nxt, nxt_sl):
        drain(cur, cur_sl)
        issue(jnp.minimum(i + 1, last), nxt, nxt_sl)
        gw = gw_ref[...]
        x = x1_ref[...] + gw[:, 0:1] * cur[0] + gw[:, 1:2] * cur[1]
        ms = jnp.mean(x * x, axis=-1, keepdims=True)
        o_ref[...] = x * lax.rsqrt(ms + EPS) * gf_ref[...]

        @pl.when(i == last)
        def _():
            drain(nxt, nxt_sl)

    @pl.when(i == 0)
    def _():
        issue(i, ybuf_even, 0)

    @pl.when(i % 2 == 0)
    def _():
        block(ybuf_even, 0, ybuf_odd, 1)

    @pl.when(i % 2 == 1)
    def _():
        block(ybuf_odd, 1, ybuf_even, 0)


def _combine(pos_flat, x1, gate_w, g_final, ys):
    T = x1.shape[0]
    tm = COMB_TM
    return pl.pallas_call(
        _combine_body,
        grid_spec=pltpu.PrefetchScalarGridSpec(
            num_scalar_prefetch=1,
            grid=(T // tm,),
            in_specs=[
                pl.BlockSpec((tm, D_MODEL), lambda i, pos: (i, 0)),
                pl.BlockSpec((tm, LANES), lambda i, pos: (i, 0)),
                pl.BlockSpec((1, D_MODEL), lambda i, pos: (0, 0)),
                pl.BlockSpec(memory_space=pl.ANY),
            ],
            out_specs=pl.BlockSpec((tm, D_MODEL), lambda i, pos: (i, 0)),
            scratch_shapes=[
                pltpu.VMEM((2, tm, D_MODEL), F32),
                pltpu.VMEM((2, tm, D_MODEL), F32),
                pltpu.SemaphoreType.DMA((2,)),
            ],
        ),
        out_shape=jax.ShapeDtypeStruct((T, D_MODEL), F32),
        compiler_params=pltpu.CompilerParams(
            dimension_semantics=("arbitrary",), vmem_limit_bytes=VMEM_LIMIT),
    )(pos_flat, x1, gate_w, g_final.reshape(1, D_MODEL), ys)


def _moe_schedule(counts, n_items):
    padded = ((counts + MOE_SB - 1) // MOE_SB) * MOE_SB
    pends = jnp.cumsum(padded)
    pstarts = pends - padded
    chunks = (counts + MOE_R - 1) // MOE_R
    cends = jnp.cumsum(chunks)
    slot = jnp.arange(n_items, dtype=I32)
    total = cends[-1]
    live = slot < total
    owner = jnp.sum((slot[:, None] >= cends[None, :]).astype(I32), axis=1)
    last_owner = jnp.sum((total - 1 >= cends).astype(I32))
    owner = jnp.where(live, owner, last_owner)
    owner = jnp.minimum(owner, N_EXPERTS - 1)
    cidx = slot - (cends - chunks)[owner]
    item_start = jnp.where(live, pstarts[owner] + cidx * MOE_R, 0)
    item_n = jnp.where(live, jnp.clip(counts[owner] - cidx * MOE_R, 0, MOE_R), 0)
    meta = jnp.stack([pends[-1], total]).astype(I32)
    return pstarts.astype(I32), pends.astype(I32), owner.astype(I32), item_start.astype(I32), \
        item_n.astype(I32), meta


def kernel(x, g_mix, w_in, ln_v_g, ln_v_b, w_spatial, b_spatial, w_branch_a, w_branch_b, w_out,
           g_ffn, w_router_group, b_router_group, w_router_expert, b_router_expert, w_gate, w_up,
           w_down, g_final):
    B, S, D = x.shape
    T = B * S
    assert B == 1 and D == D_MODEL, "the calls below treat the tokens as one causal sequence"
    assert T % max(DISP_TM, ROUTE_TB, COMB_TM, MERGE_TM, ATT_TQ, NORM_TM) == 0
    assert T % min(PROJ_TM, T) == 0
    cur = x.reshape(T, D)
    depth = g_mix.shape[0]
    for l in range(depth):
        proj = _proj(_norm(cur, g_mix[l]), w_in[l])
        yb = _attn(proj)
        lane_pad = LANES - N_GROUPS - N_EXPERTS
        w_r = jnp.pad(jnp.concatenate([w_router_group[l], w_router_expert[l]], axis=1),
                      ((0, 0), (0, lane_pad)))
        b_r = jnp.pad(jnp.concatenate([b_router_group[l], b_router_expert[l]]),
                      (0, lane_pad)).reshape(1, LANES)
        x1, h2, logits = _merge(
            proj, yb, cur, ln_v_g[l].reshape(1, A_WIDTH), ln_v_b[l].reshape(1, A_WIDTH),
            w_spatial[l], b_spatial[l].T, w_branch_a[l], w_branch_b[l],
            w_out[l].astype(BF16), g_ffn[l].reshape(1, D), w_r, b_r)
        oi, gate_w, cnt = _route(logits)

        counts = cnt[0, :N_EXPERTS].astype(I32)
        n_rows = 2 * T + N_EXPERTS * MOE_SB
        n_items = N_EXPERTS + -(-(2 * T) // MOE_R)
        pstarts, pends, item_e, item_start, item_n, meta = _moe_schedule(counts, n_items)
        chosen = oi[:, 0:2, None] == jnp.arange(N_EXPERTS, dtype=I32)
        pos = (jnp.sum(jnp.where(chosen, pstarts, 0), axis=-1) + oi[:, 2:4]).reshape(2 * T)

        xs = _dispatch(pos, pends, h2, n_rows)
        ys = _moe(item_e, item_start, item_n, meta, xs, w_gate[l], w_up[l], w_down[l])
        if l + 1 < depth:
            raise NotImplementedError("only the final layer fuses the closing rms_norm")
        cur = _combine(pos, x1, gate_w, g_final, ys)
    return cur.reshape(B, S, D)
```
